```python
import jax, jax.numpy as jnp
from jax import lax
import numpy as np

D_MODEL = 1024
BATCH = 16
SEQ = 2048
DEPTH = 2

GRID_W = 64
CTX_LEN = 256
D_MIX = D_MODEL
D_CONF = D_MIX // 4
D_SC = D_MIX // 4
GLA_HEADS = 4
GLA_DK = D_MIX // 4 // GLA_HEADS
GLA_DV = D_MIX // 2 // GLA_HEADS
GLA_RANK = 16
GLA_TAU = 16.0
GLA_CHUNK = 32
CONF_K = 31
SC_K = 3
N_EXPERTS = 16
EC_FACTOR = 2
D_EXPERT = D_MODEL
EPS = 1e-6

CONF_IN = 2 * D_CONF
GLA_QK = GLA_HEADS * GLA_DK
GLA_V = GLA_HEADS * GLA_DV
SC_IN = 3 * D_SC
OFF_Q = CONF_IN
OFF_K = OFF_Q + GLA_QK
OFF_V = OFF_K + GLA_QK
OFF_AF = OFF_V + GLA_V
OFF_AB = OFF_AF + GLA_RANK
OFF_R = OFF_AB + GLA_RANK
OFF_SC = OFF_R + GLA_V
D_IN = OFF_SC + SC_IN
SPLITS = [OFF_Q, OFF_K, OFF_V, OFF_AF, OFF_AB, OFF_R, OFF_SC]

kernel_name = 'hybrid_conformer_gla_shortconv_ec_moe_dit'


def rmsnorm(x, w):
    xf = x.astype(jnp.float32)
    y = xf * lax.rsqrt(jnp.mean(xf * xf, axis=-1, keepdims=True) + EPS)
    return (y * w.astype(jnp.float32)).astype(x.dtype)


def layernorm(x, w, b):
    xf = x.astype(jnp.float32)
    mu = jnp.mean(xf, axis=-1, keepdims=True)
    var = jnp.mean(jnp.square(xf - mu), axis=-1, keepdims=True)
    y = (xf - mu) * lax.rsqrt(var + EPS)
    return (y * w.astype(jnp.float32) + b.astype(jnp.float32)).astype(x.dtype)


def modulate(h, shift, scale):
    return h * (1 + scale) + shift


def depthwise_conv(u, w, b):
    K, C = w.shape
    out = lax.conv_general_dilated(u, w[:, None, :].astype(u.dtype), window_strides=(1,),
                                   padding=[(K // 2, K // 2)],
                                   dimension_numbers=('NWC', 'WIO', 'NWC'),
                                   feature_group_count=C)
    return out + b.astype(u.dtype)


def conformer_conv(u, w, b, ln_w, ln_b, grid):
    a, gte = jnp.split(u, 2, axis=-1)
    z = a * jax.nn.sigmoid(gte)
    if grid:
        B, L, C = z.shape
        rows = L // GRID_W
        z = depthwise_conv(z.reshape(B * rows, GRID_W, C), w, b).reshape(B, L, C)
    else:
        z = depthwise_conv(z, w, b)
    return jax.nn.silu(layernorm(z, ln_w, ln_b))


def short_conv(u, w, b):
    bg, cg, xh = jnp.split(u, 3, axis=-1)
    return bg * depthwise_conv(cg * xh, w, b)


def gla_scan(q, k, v, log_a, s0):
    B, H, L, DK = k.shape
    DV = v.shape[-1]
    n = L // GLA_CHUNK
    chunk = lambda t: t.astype(jnp.float32).reshape(B, H, n, GLA_CHUNK, t.shape[-1])
    k_c, v_c, a_c = chunk(k), chunk(v), chunk(log_a)
    b = jnp.cumsum(a_c, axis=3)
    g = b[:, :, :, -1]
    u = jnp.einsum('bhnck,bhncv->bhnkv', k_c * jnp.exp(g[:, :, :, None] - b), v_c)

    def step(s, inp):
        g_n, u_n = inp
        return jnp.exp(g_n)[..., None] * s + u_n, s

    s_fin, s_prev = lax.scan(step, s0.astype(jnp.float32),
                             (jnp.moveaxis(g, 2, 0), jnp.moveaxis(u, 2, 0)))
    if q is None:
        return None, s_fin
    s_prev = jnp.moveaxis(s_prev, 0, 2)
    q_c = chunk(q) * jnp.exp(b)
    k_rel = k_c * jnp.exp(-b)
    scores = jnp.einsum('bhnik,bhnjk->bhnij', q_c, k_rel)
    lower_tri = jnp.tril(jnp.ones((GLA_CHUNK, GLA_CHUNK), dtype=bool))
    scores = jnp.where(lower_tri, scores, 0.0)
    o = (jnp.einsum('bhnij,bhnjv->bhniv', scores, v_c)
         + jnp.einsum('bhnik,bhnkv->bhniv', q_c, s_prev))
    return o.reshape(B, H, L, DV), s_fin


def gla_prepare(q, k, v, af, ab, wa_f, ba_f, wa_b, ba_b):
    B, L, _ = k.shape
    heads = lambda t: t.reshape(B, L, GLA_HEADS, -1).transpose(0, 2, 1, 3)
    decay = lambda z, w, bb: heads(jax.nn.log_sigmoid((z @ w + bb).astype(jnp.float32)) / GLA_TAU)
    qh = None if q is None else heads(q * GLA_DK ** -0.5)
    return qh, heads(k), heads(v), decay(af, wa_f, ba_f), decay(ab, wa_b, ba_b)


def gla_bidirectional(lat, ctx_in):
    flip = lambda t: None if t is None else jnp.flip(t, axis=2)
    ql, kl, vl, alf, alb = lat
    qc, kc, vc, acf, acb = ctx_in
    B, H, _, DK = kl.shape
    zeros = jnp.zeros((B, H, DK, vl.shape[-1]), jnp.float32)
    oc_f, sc_f = gla_scan(qc, kc, vc, acf, zeros)
    oc_b, sc_b = gla_scan(flip(qc), flip(kc), flip(vc), flip(acb), zeros)
    ol_f, _ = gla_scan(ql, kl, vl, alf, sc_f)
    ol_b, _ = gla_scan(flip(ql), flip(kl), flip(vl), flip(alb), sc_b)
    o_lat = ol_f + flip(ol_b)
    o_ctx = None if qc is None else oc_f + flip(oc_b)
    return o_lat, o_ctx


def gla_output(o, r, gn_w):
    B, H, L, DV = o.shape
    o = o.transpose(0, 2, 1, 3)
    o = o * lax.rsqrt(jnp.mean(o * o, axis=-1, keepdims=True) + EPS)
    o = o.reshape(B, L, H * DV) * gn_w.astype(jnp.float32)
    return (o * jax.nn.silu(r.astype(jnp.float32))).astype(r.dtype)


def token_mixer(h, hc, w_in, w_out, conf_dw_w, conf_dw_b, conf_ln_w, conf_ln_b,
                gla_wa_f, gla_ba_f, gla_wa_b, gla_ba_b, gla_gn_w, sc_w, sc_b, ctx_out):
    dec = (gla_wa_f, gla_ba_f, gla_wa_b, gla_ba_b)
    conf_u, q, k, v, af, ab, r, sc_u = jnp.split(h @ w_in, SPLITS, axis=-1)
    if ctx_out:
        c_conf_u, c_q, c_k, c_v, c_af, c_ab, c_r, c_sc_u = jnp.split(hc @ w_in, SPLITS, axis=-1)
    else:
        c_q = None
        c_k, c_v, c_af, c_ab = jnp.split(hc @ w_in[:, OFF_K:OFF_R],
                                         [OFF_V - OFF_K, OFF_AF - OFF_K, OFF_AB - OFF_K], axis=-1)
    o_lat, o_ctx = gla_bidirectional(gla_prepare(q, k, v, af, ab, *dec),
                                     gla_prepare(c_q, c_k, c_v, c_af, c_ab, *dec))
    y = jnp.concatenate([conformer_conv(conf_u, conf_dw_w, conf_dw_b, conf_ln_w, conf_ln_b, True),
                         gla_output(o_lat, r, gla_gn_w),
                         short_conv(sc_u, sc_w, sc_b)], axis=-1) @ w_out
    if not ctx_out:
        return y, None
    yc = jnp.concatenate([conformer_conv(c_conf_u, conf_dw_w, conf_dw_b, conf_ln_w, conf_ln_b, False),
                          gla_output(o_ctx, c_r, gla_gn_w),
                          short_conv(c_sc_u, sc_w, sc_b)], axis=-1) @ w_out
    return y, yc


def expert_choice_ffn(h, router_w, w1, w3, w2):
    B, L, D = h.shape
    cap = EC_FACTOR * L // N_EXPERTS
    aff = jax.nn.softmax(jnp.einsum('bld,de->ble', h, router_w).astype(jnp.float32), axis=-1)
    gate, idx = lax.top_k(jnp.swapaxes(aff, 1, 2), cap)
    bidx = jnp.arange(B)[:, None, None]
    xs = h[bidx, idx]
    hid = jax.nn.silu(jnp.einsum('becd,edf->becf', xs, w1)) * jnp.einsum('becd,edf->becf', xs, w3)
    y = jnp.einsum('becf,efd->becd', hid, w2) * gate[..., None].astype(h.dtype)
    return jnp.zeros_like(h).at[bidx, idx].add(y)


def setup_inputs(seed: int = 0) -> dict:
    key = jax.random.key(seed)
    ks = jax.random.split(key, 32)
    f32 = jnp.float32
    nrm = lambda k, shape, scale: jax.random.normal(k, shape, f32) * scale
    L = DEPTH
    return {
        'x': nrm(ks[0], (BATCH, SEQ, D_MODEL), 1.0),
        'c': nrm(ks[1], (BATCH, D_MODEL), 1.0),
        'ctx': nrm(ks[2], (BATCH, CTX_LEN, D_MODEL), 1.0),
        'c_ctx': nrm(ks[3], (D_MODEL,), 1.0),
        'ada_w': nrm(ks[4], (L, D_MODEL, 6 * D_MODEL), 0.5 * D_MODEL ** -0.5),
        'ada_b': nrm(ks[5], (L, 6 * D_MODEL), 0.02),
        'norm1_w': 1.0 + nrm(ks[6], (L, D_MODEL), 0.02),
        'norm2_w': 1.0 + nrm(ks[7], (L, D_MODEL), 0.02),
        'w_in': nrm(ks[8], (L, D_MODEL, D_IN), D_MODEL ** -0.5),
        'conf_dw_w': nrm(ks[9], (L, CONF_K, D_CONF), CONF_K ** -0.5),
        'conf_dw_b': nrm(ks[10], (L, D_CONF), 0.02),
        'conf_ln_w': 1.0 + nrm(ks[11], (L, D_CONF), 0.02),
        'conf_ln_b': nrm(ks[12], (L, D_CONF), 0.02),
        'gla_wa_f': nrm(ks[13], (L, GLA_RANK, GLA_QK), GLA_RANK ** -0.5),
        'gla_ba_f': nrm(ks[14], (L, GLA_QK), 0.02),
        'gla_wa_b': nrm(ks[15], (L, GLA_RANK, GLA_QK), GLA_RANK ** -0.5),
        'gla_ba_b': nrm(ks[16], (L, GLA_QK), 0.02),
        'gla_gn_w': 1.0 + nrm(ks[17], (L, GLA_V), 0.02),
        'sc_w': nrm(ks[18], (L, SC_K, D_SC), SC_K ** -0.5),
        'sc_b': nrm(ks[19], (L, D_SC), 0.02),
        'w_out': nrm(ks[20], (L, D_MIX, D_MODEL), D_MIX ** -0.5),
        'router_w': nrm(ks[21], (L, D_MODEL, N_EXPERTS), D_MODEL ** -0.5),
        'exp_w1': nrm(ks[22], (L, N_EXPERTS, D_MODEL, D_EXPERT), D_MODEL ** -0.5),
        'exp_w3': nrm(ks[23], (L, N_EXPERTS, D_MODEL, D_EXPERT), D_MODEL ** -0.5),
        'exp_w2': nrm(ks[24], (L, N_EXPERTS, D_EXPERT, D_MODEL), D_EXPERT ** -0.5),
        'final_norm_w': 1.0 + nrm(ks[25], (D_MODEL,), 0.02),
    }


def reference(x, c, ctx, c_ctx, ada_w, ada_b, norm1_w, norm2_w, w_in, conf_dw_w, conf_dw_b,
              conf_ln_w, conf_ln_b, gla_wa_f, gla_ba_f, gla_wa_b, gla_ba_b, gla_gn_w, sc_w, sc_b,
              w_out, router_w, exp_w1, exp_w3, exp_w2, final_norm_w):
    s_lat = jax.nn.silu(c)[:, None, :]
    s_ctx = jax.nn.silu(c_ctx)
    xc = ctx
    for i in range(DEPTH):
        last = i == DEPTH - 1
        m_lat = jnp.split(s_lat @ ada_w[i] + ada_b[i], 6, axis=-1)
        m_ctx = jnp.split(s_ctx @ ada_w[i] + ada_b[i], 6, axis=-1)
        h = modulate(rmsnorm(x, norm1_w[i]), m_lat[0], m_lat[1])
        hc = modulate(rmsnorm(xc, norm1_w[i]), m_ctx[0], m_ctx[1])
        y, yc = token_mixer(h, hc, w_in[i], w_out[i], conf_dw_w[i], conf_dw_b[i], conf_ln_w[i],
                            conf_ln_b[i], gla_wa_f[i], gla_ba_f[i], gla_wa_b[i], gla_ba_b[i],
                            gla_gn_w[i], sc_w[i], sc_b[i], not last)
        x = x + m_lat[2] * y
        h = modulate(rmsnorm(x, norm2_w[i]), m_lat[3], m_lat[4])
        x = x + m_lat[5] * expert_choice_ffn(h, router_w[i], exp_w1[i], exp_w3[i], exp_w2[i])
        if not last:
            xc = xc + m_ctx[2] * yc
            hc = modulate(rmsnorm(xc, norm2_w[i]), m_ctx[3], m_ctx[4])
            xc = xc + m_ctx[5] * expert_choice_ffn(hc, router_w[i], exp_w1[i], exp_w3[i], exp_w2[i])
    return rmsnorm(x, final_norm_w)
```

```python
import functools

import jax
import jax.numpy as jnp
from jax import lax
from jax.experimental import pallas as pl
from jax.experimental.pallas import tpu as pltpu

F32 = jnp.float32
BF16 = jnp.bfloat16
HI = lax.Precision.HIGHEST

D_MODEL = 1024
DEPTH = 2
GRID_W = 64
D_CONF = 256
D_SC = 256
GLA_HEADS = 4
GLA_DK = 64
GLA_DV = 128
GLA_QK = GLA_HEADS * GLA_DK
GLA_V = GLA_HEADS * GLA_DV
GLA_RANK = 16
GLA_TAU = 16.0
CONF_K = 31
SC_K = 3
N_EXPERTS = 16
EC_FACTOR = 2
EPS = 1e-6

OFF_Q = 2 * D_CONF
OFF_K = OFF_Q + GLA_QK
OFF_V = OFF_K + GLA_QK
OFF_AF = OFF_V + GLA_V
OFF_AB = OFF_AF + GLA_RANK
OFF_R = OFF_AB + GLA_RANK
OFF_SC = OFF_R + GLA_V
D_IN = OFF_SC + 3 * D_SC

P_CONF = 0
P_QKV = 512
P_R = 1536
P_SC = 2048
P_AFAB = 2816
P_TOTAL = 2944
AFAB_W = 128

CHUNK = 64
CONV_SUB = 64
CONV_PAD = 16
SC_PAD = 8
TOK_TILE = 128
SLOT_ALIGN = 16
VMEM_LIMIT = 56 * 1024 * 1024


def _sigmoid(x):
    return 1.0 / (1.0 + jnp.exp(-x))


def _silu(x):
    return x * _sigmoid(x)


def _log_sigmoid(z):
    return jnp.minimum(z, 0.0) - jnp.log1p(jnp.exp(-jnp.abs(z)))


def _params(sem):
    return pltpu.CompilerParams(dimension_semantics=sem, vmem_limit_bytes=VMEM_LIMIT)


def _ada_kernel(s_ref, w_ref, b_ref, o_ref):
    s = _silu(s_ref[...])
    o_ref[0] = jnp.dot(s, w_ref[0], precision=HI, preferred_element_type=F32) + b_ref[0]


def _ada(s_in, ada_w, ada_b):
    rows = s_in.shape[0]
    tn = 1024
    return pl.pallas_call(
        _ada_kernel,
        grid=(DEPTH, 6 * D_MODEL // tn),
        in_specs=[pl.BlockSpec((rows, D_MODEL), lambda l, n: (0, 0)),
                  pl.BlockSpec((1, D_MODEL, tn), lambda l, n: (l, 0, n)),
                  pl.BlockSpec((1, 1, tn), lambda l, n: (l, 0, n))],
        out_specs=pl.BlockSpec((1, rows, tn), lambda l, n: (l, 0, n)),
        out_shape=jax.ShapeDtypeStruct((DEPTH, rows, 6 * D_MODEL), F32),
        compiler_params=_params(("arbitrary", "arbitrary")),
        name="ada",
    )(s_in, ada_w, ada_b.reshape(DEPTH, 1, 6 * D_MODEL))


def _rms_mod(x, nw, shift, scale):
    ms = jnp.mean(x * x, axis=-1, keepdims=True)
    return (x * lax.rsqrt(ms + EPS) * nw) * (1.0 + scale) + shift


def _proj_kernel(*refs, add_moe):
    if add_moe:
        (x_ref, moe_ref, g5_ref, nw_ref, sh_ref, sc_ref, w_ref,
         xo_ref, conf_ref, qkv_ref, r_ref, scu_ref, afab_ref) = refs
        x = x_ref[0] + g5_ref[0] * moe_ref[0]
        xo_ref[0] = x
    else:
        (x_ref, nw_ref, sh_ref, sc_ref, w_ref,
         conf_ref, qkv_ref, r_ref, scu_ref, afab_ref) = refs
        x = x_ref[0]
    hb = _rms_mod(x, nw_ref[...], sh_ref[0], sc_ref[0]).astype(BF16)

    def proj(lo, hi):
        return jnp.dot(hb, w_ref[:, lo:hi], preferred_element_type=F32)

    conf_ref[0] = proj(P_CONF, P_QKV).astype(BF16)
    qkv_ref[0] = proj(P_QKV, P_R).astype(BF16)
    r_ref[0] = proj(P_R, P_SC).astype(BF16)
    scu_ref[0] = proj(P_SC, P_AFAB).astype(BF16)
    afab_ref[0] = proj(P_AFAB, P_TOTAL)


def _proj(x, moe, g5, nw, shift, scale, w_perm):
    B, L, D = x.shape
    tm = min(L, 512)
    add_moe = moe is not None
    tok = lambda w: pl.BlockSpec((1, tm, w), lambda b, t: (b, t, 0))
    per_b = pl.BlockSpec((1, 1, D), lambda b, t: (b, 0, 0))
    ins, in_specs = [x], [tok(D)]
    if add_moe:
        ins += [moe, g5]
        in_specs += [tok(D), per_b]
    ins += [nw.reshape(1, D), shift, scale, w_perm]
    in_specs += [pl.BlockSpec((1, D), lambda b, t: (0, 0)), per_b, per_b,
                 pl.BlockSpec((D, P_TOTAL), lambda b, t: (0, 0))]
    widths = [(2 * D_CONF, BF16), (P_R - P_QKV, BF16), (GLA_V, BF16), (3 * D_SC, BF16), (AFAB_W, F32)]
    out_shape = [jax.ShapeDtypeStruct((B, L, w), dt) for w, dt in widths]
    out_specs = [tok(w) for w, _ in widths]
    if add_moe:
        out_shape = [jax.ShapeDtypeStruct((B, L, D), F32)] + out_shape
        out_specs = [tok(D)] + out_specs
    outs = pl.pallas_call(
        functools.partial(_proj_kernel, add_moe=add_moe),
        grid=(B, L // tm),
        in_specs=in_specs, out_specs=out_specs, out_shape=out_shape,
        compiler_params=_params(("parallel", "parallel")),
        name="proj",
    )(*ins)
    return tuple(outs) if add_moe else (None,) + tuple(outs)


def _conv_kernel(conf_ref, scu_ref, cw_ref, cb_ref, lnw_ref, lnb_ref, sw_ref, sb_ref,
                 yc_ref, ys_ref, zp_ref, pz_ref, *, row_len):
    L = conf_ref.shape[1]
    n_rows = L // row_len
    n_sub = row_len // CONV_SUB
    C = D_CONF

    zeros_pad = jnp.zeros((CONV_PAD, C), F32)

    def fill(r, carry):
        zp_ref[r, 0:CONV_PAD, :] = zeros_pad
        zp_ref[r, CONV_PAD + row_len:, :] = zeros_pad
        for s in range(n_sub):
            start = pl.multiple_of(r * row_len + s * CONV_SUB, CONV_SUB)
            u = conf_ref[0, pl.ds(start, CONV_SUB), :].astype(F32)
            lo = CONV_PAD + s * CONV_SUB
            zp_ref[r, lo:lo + CONV_SUB, :] = u[:, :C] * _sigmoid(u[:, C:])
        return carry

    lax.fori_loop(0, n_rows, fill, 0)

    def conv_row(r, carry):
        for s in range(n_sub):
            base = CONV_PAD - CONF_K // 2 + s * CONV_SUB
            acc = jnp.zeros((CONV_SUB, C), F32)
            for j in range(CONF_K):
                acc = acc + cw_ref[j:j + 1, :] * zp_ref[r, base + j:base + j + CONV_SUB, :]
            acc = acc + cb_ref[...]
            mu = jnp.mean(acc, axis=-1, keepdims=True)
            cen = acc - mu
            var = jnp.mean(cen * cen, axis=-1, keepdims=True)
            y = cen * lax.rsqrt(var + EPS) * lnw_ref[...] + lnb_ref[...]
            start = pl.multiple_of(r * row_len + s * CONV_SUB, CONV_SUB)
            yc_ref[0, pl.ds(start, CONV_SUB), :] = _silu(y).astype(BF16)
        return carry

    lax.fori_loop(0, n_rows, conv_row, 0)

    n_blk = L // CONV_SUB
    pz_ref[0:SC_PAD, :] = jnp.zeros((SC_PAD, D_SC), F32)
    pz_ref[SC_PAD + L:, :] = jnp.zeros((SC_PAD, D_SC), F32)

    def fill_p(i, carry):
        start = pl.multiple_of(i * CONV_SUB, CONV_SUB)
        u = scu_ref[0, pl.ds(start, CONV_SUB), :].astype(F32)
        pz_ref[pl.ds(SC_PAD + start, CONV_SUB), :] = u[:, D_SC:2 * D_SC] * u[:, 2 * D_SC:]
        return carry

    lax.fori_loop(0, n_blk, fill_p, 0)

    def sc_blk(i, carry):
        start = pl.multiple_of(i * CONV_SUB, CONV_SUB)
        win = pz_ref[pl.ds(start, CONV_SUB + 2 * SC_PAD), :]
        acc = sb_ref[...] + sw_ref[0:1, :] * win[SC_PAD - 1:SC_PAD - 1 + CONV_SUB]
        acc = acc + sw_ref[1:2, :] * win[SC_PAD:SC_PAD + CONV_SUB]
        acc = acc + sw_ref[2:3, :] * win[SC_PAD + 1:SC_PAD + 1 + CONV_SUB]
        bg = scu_ref[0, pl.ds(start, CONV_SUB), 0:D_SC].astype(F32)
        ys_ref[0, pl.ds(start, CONV_SUB), :] = (bg * acc).astype(BF16)
        return carry

    lax.fori_loop(0, n_blk, sc_blk, 0)


def _convs(conf, scu, cw, cb, lnw, lnb, sw, sb, row_len):
    B, L, _ = conf.shape
    n_rows = L // row_len
    full = lambda a: pl.BlockSpec(a.shape, lambda b: (0,) * a.ndim)
    cw_p = jnp.zeros((32, D_CONF), F32).at[:CONF_K].set(cw)
    sw_p = jnp.zeros((8, D_SC), F32).at[:SC_K].set(sw)
    small = [cw_p, cb.reshape(1, D_CONF), lnw.reshape(1, D_CONF), lnb.reshape(1, D_CONF),
             sw_p, sb.reshape(1, D_SC)]
    return pl.pallas_call(
        functools.partial(_conv_kernel, row_len=row_len),
        grid=(B,),
        in_specs=[pl.BlockSpec((1, L, 2 * D_CONF), lambda b: (b, 0, 0)),
                  pl.BlockSpec((1, L, 3 * D_SC), lambda b: (b, 0, 0))] + [full(a) for a in small],
        out_specs=[pl.BlockSpec((1, L, D_CONF), lambda b: (b, 0, 0)),
                   pl.BlockSpec((1, L, D_SC), lambda b: (b, 0, 0))],
        out_shape=[jax.ShapeDtypeStruct((B, L, D_CONF), BF16),
                   jax.ShapeDtypeStruct((B, L, D_SC), BF16)],
        scratch_shapes=[pltpu.VMEM((n_rows, row_len + 2 * CONV_PAD, D_CONF), F32),
                        pltpu.VMEM((L + 2 * SC_PAD, D_SC), F32)],
        compiler_params=_params(("parallel",)),
        name="convs",
    )(conf, scu, *small)


def _gla_kernel(qkv_ref, afab_ref, r_ref, s0f_ref, s0b_ref, wa_ref, ba_ref, gnw_ref,
                og_ref, sff_ref, sfb_ref, bf_ref, bb_ref, spf_ref, snb_ref):
    L = qkv_ref.shape[1]
    n = L // CHUNK
    C = CHUNK
    ii = lax.broadcasted_iota(jnp.int32, (C, C), 0)
    jj = lax.broadcasted_iota(jnp.int32, (C, C), 1)
    lower = ii >= jj
    tri_f = jnp.where(lower, 1.0, 0.0).astype(F32)
    tri_b = jnp.where(jj >= ii, 1.0, 0.0).astype(F32)

    def rows_of(c):
        return pl.ds(pl.multiple_of(c * C, C), C)

    def log_decay(rows, d):
        a = afab_ref[0, rows, d * GLA_RANK:(d + 1) * GLA_RANK]
        z = jnp.dot(a, wa_ref[d], precision=HI, preferred_element_type=F32) + ba_ref[d]
        return _log_sigmoid(z) * (1.0 / GLA_TAU)

    def col_bcast(row_vec):
        return jnp.broadcast_to(row_vec, (GLA_DV, GLA_QK)).T

    def state_update(c, s, d, tri, b_ref, sp_ref, total_row):
        rows = rows_of(c)
        la = log_decay(rows, d)
        b = jnp.dot(tri, la, precision=HI, preferred_element_type=F32)
        b_ref[rows, :] = b
        g = b[total_row:total_row + 1, :]
        k = qkv_ref[0, rows, GLA_QK:2 * GLA_QK].astype(F32)
        kt = (k * jnp.exp(g - b)).T.astype(BF16)
        u = []
        for h in range(GLA_HEADS):
            v_h = qkv_ref[0, rows, 2 * GLA_QK + h * GLA_DV:2 * GLA_QK + (h + 1) * GLA_DV]
            u.append(jnp.dot(kt[h * GLA_DK:(h + 1) * GLA_DK, :], v_h, preferred_element_type=F32))
        sp_ref[c] = s.astype(BF16)
        return col_bcast(jnp.exp(g)) * s + jnp.concatenate(u, axis=0)

    sf = lax.fori_loop(
        0, n, lambda c, s: state_update(c, s, 0, tri_f, bf_ref, spf_ref, C - 1), s0f_ref[0])
    sff_ref[0] = sf
    sb = lax.fori_loop(
        0, n, lambda t, s: state_update(n - 1 - t, s, 1, tri_b, bb_ref, snb_ref, 0), s0b_ref[0])
    sfb_ref[0] = sb

    scale = GLA_DK ** -0.5
    mid = C // 2

    def out_chunk(c, carry):
        rows = rows_of(c)
        q = qkv_ref[0, rows, 0:GLA_QK].astype(F32) * scale
        k = qkv_ref[0, rows, GLA_QK:2 * GLA_QK].astype(F32)
        bf = bf_ref[rows, :]
        bb = bb_ref[rows, :]
        rf = bf[mid:mid + 1, :]
        rb = bb[mid:mid + 1, :]
        qf = (q * jnp.exp(bf - rf)).astype(BF16)
        kf = (k * jnp.exp(rf - bf)).astype(BF16)
        qb = (q * jnp.exp(bb - rb)).astype(BF16)
        kb = (k * jnp.exp(rb - bb)).astype(BF16)
        qif = (q * jnp.exp(bf)).astype(BF16)
        qib = (q * jnp.exp(bb)).astype(BF16)
        spf = spf_ref[c]
        snb = snb_ref[c]
        nt = (((1,), (1,)), ((), ()))
        outs = []
        for h in range(GLA_HEADS):
            ks = slice(h * GLA_DK, (h + 1) * GLA_DK)
            s_f = lax.dot_general(qf[:, ks], kf[:, ks], nt, preferred_element_type=F32)
            s_b = lax.dot_general(qb[:, ks], kb[:, ks], nt, preferred_element_type=F32)
            p = (jnp.where(lower, s_f, 0.0) + jnp.where(jj >= ii, s_b, 0.0)).astype(BF16)
            v_h = qkv_ref[0, rows, 2 * GLA_QK + h * GLA_DV:2 * GLA_QK + (h + 1) * GLA_DV]
            o = jnp.dot(p, v_h, preferred_element_type=F32)
            o = o + jnp.dot(qif[:, ks], spf[ks, :], preferred_element_type=F32)
            o = o + jnp.dot(qib[:, ks], snb[ks, :], preferred_element_type=F32)
            ms = jnp.mean(o * o, axis=-1, keepdims=True)
            outs.append(o * lax.rsqrt(ms + EPS))
        o_all = jnp.concatenate(outs, axis=-1) * gnw_ref[...]
        og_ref[0, rows, :] = (o_all * _silu(r_ref[0, rows, :].astype(F32))).astype(BF16)
        return carry

    lax.fori_loop(0, n, out_chunk, 0)


def _gla(qkv, afab, r, s0f, s0b, wa, ba, gnw):
    B, L, _ = qkv.shape
    n = L // CHUNK
    tok = lambda w: pl.BlockSpec((1, L, w), lambda b: (b, 0, 0))
    st = pl.BlockSpec((1, GLA_QK, GLA_DV), lambda b: (b, 0, 0))
    full = lambda a: pl.BlockSpec(a.shape, lambda b: (0,) * a.ndim)
    small = [wa, ba, gnw.reshape(1, GLA_V)]
    return pl.pallas_call(
        _gla_kernel,
        grid=(B,),
        in_specs=[tok(P_R - P_QKV), tok(AFAB_W), tok(GLA_V), st, st] + [full(a) for a in small],
        out_specs=[tok(GLA_V), st, st],
        out_shape=[jax.ShapeDtypeStruct((B, L, GLA_V), BF16),
                   jax.ShapeDtypeStruct((B, GLA_QK, GLA_DV), F32),
                   jax.ShapeDtypeStruct((B, GLA_QK, GLA_DV), F32)],
        scratch_shapes=[pltpu.VMEM((L, GLA_QK), F32), pltpu.VMEM((L, GLA_QK), F32),
                        pltpu.VMEM((n, GLA_QK, GLA_DV), BF16), pltpu.VMEM((n, GLA_QK, GLA_DV), BF16)],
        compiler_params=_params(("parallel",)),
        name="gla",
    )(qkv, afab, r, s0f, s0b, *small)


def _outproj_kernel(yc_ref, og_ref, ys_ref, x_ref, wo_ref, g2_ref, nw_ref, sh_ref, sc_ref, rwt_ref,
                    xo_ref, h2_ref, lt_ref):
    y = jnp.dot(yc_ref[0], wo_ref[0:D_CONF, :], preferred_element_type=F32)
    y = y + jnp.dot(og_ref[0], wo_ref[D_CONF:D_CONF + GLA_V, :], preferred_element_type=F32)
    y = y + jnp.dot(ys_ref[0], wo_ref[D_CONF + GLA_V:, :], preferred_element_type=F32)
    x = x_ref[0] + g2_ref[0] * y
    xo_ref[0] = x
    hb = _rms_mod(x, nw_ref[...], sh_ref[0], sc_ref[0]).astype(BF16)
    h2_ref[0] = hb
    lt_ref[0] = lax.dot_general(rwt_ref[...], hb, (((1,), (1,)), ((), ())),
                                preferred_element_type=F32)


def _outproj(yc, og, ys, x, wo, g2, nw, shift, scale, rwt):
    B, L, D = x.shape
    tm = min(L, 512)
    tok = lambda w: pl.BlockSpec((1, tm, w), lambda b, t: (b, t, 0))
    per_b = pl.BlockSpec((1, 1, D), lambda b, t: (b, 0, 0))
    return pl.pallas_call(
        _outproj_kernel,
        grid=(B, L // tm),
        in_specs=[tok(D_CONF), tok(GLA_V), tok(D_SC), tok(D),
                  pl.BlockSpec((D, D), lambda b, t: (0, 0)), per_b,
                  pl.BlockSpec((1, D), lambda b, t: (0, 0)), per_b, per_b,
                  pl.BlockSpec((N_EXPERTS, D), lambda b, t: (0, 0))],
        out_specs=[tok(D), tok(D), pl.BlockSpec((1, N_EXPERTS, tm), lambda b, t: (b, 0, t))],
        out_shape=[jax.ShapeDtypeStruct((B, L, D), F32), jax.ShapeDtypeStruct((B, L, D), BF16),
                   jax.ShapeDtypeStruct((B, N_EXPERTS, L), F32)],
        compiler_params=_params(("parallel", "parallel")),
        name="outproj",
    )(yc, og, ys, x, wo, g2, nw.reshape(1, D), shift, scale, rwt)


def _lane_cumsum(x):
    n = x.shape[-1]
    lane = lax.broadcasted_iota(jnp.int32, x.shape, x.ndim - 1)
    s = 1
    while s < n:
        x = x + jnp.where(lane >= s, pltpu.roll(x, s, axis=x.ndim - 1), 0)
        s *= 2
    return x


def _route_kernel(lt_ref, pos_ref, gate_ref, bnd_ref, *, cap):
    lt = lt_ref[0]
    L = lt.shape[1]
    e = jnp.exp(lt - jnp.max(lt, axis=0, keepdims=True))
    aff = e / jnp.sum(e, axis=0, keepdims=True)

    def search(i, t):
        cand = t | lax.shift_left(jnp.int32(1), 30 - i)
        cnt = jnp.sum((aff >= pltpu.bitcast(cand, F32)).astype(jnp.int32), axis=1, keepdims=True)
        return jnp.where(cnt >= cap, cand, t)

    thr_bits = lax.fori_loop(0, 31, search, jnp.zeros((lt.shape[0], 1), jnp.int32))
    thr = pltpu.bitcast(thr_bits, F32)
    gt = aff > thr
    eq = aff == thr
    need = cap - jnp.sum(gt.astype(jnp.int32), axis=1, keepdims=True)
    eq_i = eq.astype(jnp.int32)
    eq_rank = _lane_cumsum(eq_i) - eq_i
    sel = gt | (eq & (eq_rank < need))
    sel_i = sel.astype(jnp.int32)
    slot = _lane_cumsum(sel_i) - sel_i
    pos_ref[0] = jnp.where(sel, slot, -1)
    gate_ref[0] = aff
    bnd_ref[0] = jnp.concatenate([slot[:, k * TOK_TILE:k * TOK_TILE + 1] for k in range(L // TOK_TILE)],
                                 axis=1)


def _route(lt, cap):
    B, E, L = lt.shape
    spec = pl.BlockSpec((1, E, L), lambda b: (b, 0, 0))
    nt = L // TOK_TILE
    return pl.pallas_call(
        functools.partial(_route_kernel, cap=cap),
        grid=(B,),
        in_specs=[spec], out_specs=[spec, spec, pl.BlockSpec((1, E, nt), lambda b: (b, 0, 0))],
        out_shape=[jax.ShapeDtypeStruct((B, E, L), jnp.int32), jax.ShapeDtypeStruct((B, E, L), F32),
                   jax.ShapeDtypeStruct((B, E, nt), jnp.int32)],
        compiler_params=_params(("parallel",)),
        name="route",
    )(lt)


def _window(lo, hi, width, cap):
    a0 = jnp.minimum((lo // SLOT_ALIGN) * SLOT_ALIGN, cap - width)
    n = jnp.where(hi > lo, (hi - a0 + width - 1) // width, 0)
    return a0, n


def _one_hot_rows(pos_row, start, width, first_row=None):
    slot = lax.broadcasted_iota(jnp.int32, (width, pos_row.shape[1]), 0) + start
    on = 1.0 if first_row is None else jnp.where(slot >= first_row, 1.0, 0.0)
    return jnp.where(slot == pos_row, on, 0.0).astype(BF16)


def _gather_kernel(bnd_ref, pos_ref, h_ref, xs_ref, acc_ref, *, cap, tile, width):
    b = pl.program_id(0)
    L = h_ref.shape[1]
    nt = L // TOK_TILE
    step = tile // TOK_TILE

    def expert(e, carry):
        acc_ref[...] = jnp.zeros_like(acc_ref)
        base = (b * N_EXPERTS + e) * nt
        for k in range(L // tile):
            lo = bnd_ref[base + k * step]
            hi = bnd_ref[base + (k + 1) * step] if (k + 1) * step < nt else jnp.int32(cap)
            a0, n = _window(lo, hi, width, cap)
            pos_row = pos_ref[0, pl.ds(e, 1), k * tile:(k + 1) * tile]
            h_tile = h_ref[0, k * tile:(k + 1) * tile, :]

            def win(w, c, a0=a0, pos_row=pos_row, h_tile=h_tile):
                first = a0 + w * width
                start = pl.multiple_of(jnp.minimum(first, cap - width), SLOT_ALIGN)
                m = _one_hot_rows(pos_row, start, width, first)
                acc_ref[pl.ds(start, width), :] += jnp.dot(m, h_tile, preferred_element_type=F32)
                return c

            lax.fori_loop(0, n, win, 0)
        xs_ref[0, e] = acc_ref[...].astype(BF16)
        return carry

    lax.fori_loop(0, N_EXPERTS, expert, 0)


def _gather(bnd, pos, h2, cap):
    B, L, D = h2.shape
    E = N_EXPERTS
    tile = min(L, 256)
    width = min(cap, 64)
    return pl.pallas_call(
        functools.partial(_gather_kernel, cap=cap, tile=tile, width=width),
        grid_spec=pltpu.PrefetchScalarGridSpec(
            num_scalar_prefetch=1, grid=(B,),
            in_specs=[pl.BlockSpec((1, E, L), lambda b, s: (b, 0, 0)),
                      pl.BlockSpec((1, L, D), lambda b, s: (b, 0, 0))],
            out_specs=pl.BlockSpec((1, E, cap, D), lambda b, s: (b, 0, 0, 0)),
            scratch_shapes=[pltpu.VMEM((cap, D), F32)]),
        out_shape=jax.ShapeDtypeStruct((B, E, cap, D), BF16),
        compiler_params=_params(("arbitrary",)),
        name="gather",
    )(bnd.reshape(-1), pos, h2)


def _ffn_kernel(pos_ref, gate_ref, xs_ref, w1_ref, w3_ref, w2_ref, y_ref, *, cap):
    bb = xs_ref.shape[0]
    L = pos_ref.shape[3]
    slot = lax.broadcasted_iota(jnp.int32, (cap, L), 0)
    gates = [jnp.sum(jnp.where(slot == pos_ref[i, 0], gate_ref[i, 0], 0.0), axis=1, keepdims=True)
             for i in range(bb)]
    gate = jnp.concatenate(gates, axis=0)
    xs = xs_ref[:, 0].reshape(bb * cap, xs_ref.shape[3])
    hid = _silu(jnp.dot(xs, w1_ref[0], preferred_element_type=F32))
    hid = (hid * jnp.dot(xs, w3_ref[0], preferred_element_type=F32)).astype(BF16)
    y = (jnp.dot(hid, w2_ref[0], preferred_element_type=F32) * gate).astype(BF16)
    y_ref[:, 0] = y.reshape(bb, cap, y.shape[1])


def _ffn(pos, gate, xs, w1, w3, w2, bb):
    B, E, cap, D = xs.shape
    L = pos.shape[2]
    row = pl.BlockSpec((bb, 1, 1, L), lambda e, b: (b, e, 0, 0))
    slab = pl.BlockSpec((bb, 1, cap, D), lambda e, b: (b, e, 0, 0))
    wspec = pl.BlockSpec((1, D, D), lambda e, b: (e, 0, 0))
    return pl.pallas_call(
        functools.partial(_ffn_kernel, cap=cap),
        grid=(E, B // bb),
        in_specs=[row, row, slab, wspec, wspec, wspec],
        out_specs=slab,
        out_shape=jax.ShapeDtypeStruct((B, E, cap, D), BF16),
        compiler_params=_params(("parallel", "parallel")),
        name="ffn",
    )(pos.reshape(B, E, 1, L), gate.reshape(B, E, 1, L), xs, w1, w3, w2)


def _combine_kernel(bnd_ref, pos_ref, y_ref, out_ref, *, cap, width):
    b = pl.program_id(0)
    L = out_ref.shape[1]
    nt = L // TOK_TILE
    tn = (((0,), (0,)), ((), ()))

    def tile_body(t, carry):
        toks = pl.ds(pl.multiple_of(t * TOK_TILE, TOK_TILE), TOK_TILE)
        ms, ys, wins = [], [], []
        for e in range(N_EXPERTS):
            base = (b * N_EXPERTS + e) * nt
            lo = bnd_ref[base + t]
            hi = jnp.where(t + 1 < nt, bnd_ref[base + jnp.minimum(t + 1, nt - 1)], cap)
            a0, n = _window(lo, hi, width, cap)
            a0 = pl.multiple_of(a0, SLOT_ALIGN)
            pos_row = pos_ref[0, e:e + 1, toks]
            ms.append(_one_hot_rows(pos_row, a0, width))
            ys.append(y_ref[0, e, pl.ds(a0, width), :])
            wins.append((a0, n, pos_row))
        out_ref[0, toks, :] = lax.dot_general(jnp.concatenate(ms, axis=0), jnp.concatenate(ys, axis=0), tn,
                                              preferred_element_type=F32)
        for e, (a0, n, pos_row) in enumerate(wins):
            def extra(w, c, e=e, a0=a0, pos_row=pos_row):
                first = a0 + w * width
                start = pl.multiple_of(jnp.minimum(first, cap - width), SLOT_ALIGN)
                m = _one_hot_rows(pos_row, start, width, first)
                out_ref[0, toks, :] += lax.dot_general(m, y_ref[0, e, pl.ds(start, width), :], tn,
                                                       preferred_element_type=F32)
                return c

            lax.fori_loop(1, n, extra, 0)
        return carry

    lax.fori_loop(0, nt, tile_body, 0)


def _combine(bnd, pos, y, L):
    B, E, cap, D = y.shape
    width = min(cap, 64)
    return pl.pallas_call(
        functools.partial(_combine_kernel, cap=cap, width=width),
        grid_spec=pltpu.PrefetchScalarGridSpec(
            num_scalar_prefetch=1, grid=(B,),
            in_specs=[pl.BlockSpec((1, E, L), lambda b, s: (b, 0, 0)),
                      pl.BlockSpec((1, E, cap, D), lambda b, s: (b, 0, 0, 0))],
            out_specs=pl.BlockSpec((1, L, D), lambda b, s: (b, 0, 0))),
        out_shape=jax.ShapeDtypeStruct((B, L, D), F32),
        compiler_params=_params(("arbitrary",)),
        name="combine",
    )(bnd.reshape(-1), pos, y)


def _moe(lt, h2, w1, w3, w2, cap, bb):
    pos, gate, bnd = _route(lt, cap)
    xs = _gather(bnd, pos, h2, cap)
    y = _ffn(pos, gate, xs, w1, w3, w2, bb)
    return _combine(bnd, pos, y, h2.shape[1])


def _final_kernel(x_ref, moe_ref, g5_ref, nw_ref, o_ref):
    x = x_ref[0] + g5_ref[0] * moe_ref[0]
    ms = jnp.mean(x * x, axis=-1, keepdims=True)
    o_ref[0] = x * lax.rsqrt(ms + EPS) * nw_ref[...]


def _final(x, moe, g5, nw):
    B, L, D = x.shape
    tm = 512
    tok = pl.BlockSpec((1, tm, D), lambda b, t: (b, t, 0))
    return pl.pallas_call(
        _final_kernel,
        grid=(B, L // tm),
        in_specs=[tok, tok, pl.BlockSpec((1, 1, D), lambda b, t: (b, 0, 0)),
                  pl.BlockSpec((1, D), lambda b, t: (0, 0))],
        out_specs=tok,
        out_shape=jax.ShapeDtypeStruct((B, L, D), F32),
        compiler_params=_params(("parallel", "parallel")),
        name="final",
    )(x, moe, g5, nw.reshape(1, D))


def _permute_w_in(w):
    cols = jnp.concatenate([w[:, 0:OFF_AF], w[:, OFF_R:OFF_SC], w[:, OFF_SC:D_IN], w[:, OFF_AF:OFF_R]],
                           axis=1)
    return jnp.pad(cols, ((0, 0), (0, P_TOTAL - D_IN))).astype(BF16)


def kernel(x, c, ctx, c_ctx, ada_w, ada_b, norm1_w, norm2_w, w_in, conf_dw_w, conf_dw_b, conf_ln_w,
           conf_ln_b, gla_wa_f, gla_ba_f, gla_wa_b, gla_ba_b, gla_gn_w, sc_w, sc_b, w_out, router_w,
           exp_w1, exp_w3, exp_w2, final_norm_w):
    B, L, D = x.shape
    Lc = ctx.shape[1]
    cap = EC_FACTOR * L // N_EXPERTS
    cap_c = EC_FACTOR * Lc // N_EXPERTS

    cond = jnp.concatenate([c, c_ctx[None, :], jnp.zeros((7, D), F32)], axis=0)
    mods = _ada(cond, ada_w, ada_b)

    xc = ctx
    moe = moe_c = g5 = g5c = None
    zero_state = jnp.zeros((B, GLA_QK, GLA_DV), F32)
    for i in range(DEPTH):
        last = i == DEPTH - 1
        m_lat = [mods[i, :B, j * D:(j + 1) * D].reshape(B, 1, D) for j in range(6)]
        m_ctx = [jnp.broadcast_to(mods[i, B, j * D:(j + 1) * D].reshape(1, 1, D), (B, 1, D))
                 for j in range(6)]
        w_perm = _permute_w_in(w_in[i])
        wo = w_out[i].astype(BF16)
        rwt = router_w[i].T.astype(BF16)
        wa = jnp.stack([gla_wa_f[i], gla_wa_b[i]])
        ba = jnp.stack([gla_ba_f[i], gla_ba_b[i]]).reshape(2, 1, GLA_QK)
        w1, w3, w2 = exp_w1[i].astype(BF16), exp_w3[i].astype(BF16), exp_w2[i].astype(BF16)
        conv_w = (conf_dw_w[i], conf_dw_b[i], conf_ln_w[i], conf_ln_b[i], sc_w[i], sc_b[i])

        xc_new, c_conf, c_qkv, c_r, c_sc, c_afab = _proj(xc, moe_c, g5c, norm1_w[i], m_ctx[0], m_ctx[1],
                                                         w_perm)
        xc = xc if xc_new is None else xc_new
        c_og, c_sf, c_sb = _gla(c_qkv, c_afab, c_r, zero_state, zero_state, wa, ba, gla_gn_w[i])

        x_new, conf, qkv, r, scu, afab = _proj(x, moe, g5, norm1_w[i], m_lat[0], m_lat[1], w_perm)
        x = x if x_new is None else x_new
        yc, ys = _convs(conf, scu, *conv_w, row_len=GRID_W)
        og, _, _ = _gla(qkv, afab, r, c_sf, c_sb, wa, ba, gla_gn_w[i])
        x, h2, lt = _outproj(yc, og, ys, x, wo, m_lat[2], norm2_w[i], m_lat[3], m_lat[4], rwt)
        moe = _moe(lt, h2, w1, w3, w2, cap, 2)
        g5 = m_lat[5]

        if not last:
            c_yc, c_ys = _convs(c_conf, c_sc, *conv_w, row_len=Lc)
            xc, c_h2, c_lt = _outproj(c_yc, c_og, c_ys, xc, wo, m_ctx[2], norm2_w[i], m_ctx[3], m_ctx[4],
                                      rwt)
            moe_c = _moe(c_lt, c_h2, w1, w3, w2, cap_c, B)
            g5c = m_ctx[5]
    return _final(x, moe, g5, final_norm_w)
```

```python
import functools

import jax
import jax.numpy as jnp
from jax import lax
from jax.experimental import pallas as pl
from jax.experimental.pallas import tpu as pltpu

F32 = jnp.float32
BF16 = jnp.bfloat16
HI = lax.Precision.HIGHEST

D_MODEL = 1024
DEPTH = 2
GRID_W = 64
D_CONF = 256
D_SC = 256
GLA_HEADS = 4
GLA_DK = 64
GLA_DV = 128
GLA_QK = GLA_HEADS * GLA_DK
GLA_V = GLA_HEADS * GLA_DV
GLA_RANK = 16
GLA_TAU = 16.0
CONF_K = 31
SC_K = 3
N_EXPERTS = 16
EC_FACTOR = 2
EPS = 1e-6

OFF_Q = 2 * D_CONF
OFF_K = OFF_Q + GLA_QK
OFF_V = OFF_K + GLA_QK
OFF_AF = OFF_V + GLA_V
OFF_AB = OFF_AF + GLA_RANK
OFF_R = OFF_AB + GLA_RANK
OFF_SC = OFF_R + GLA_V
D_IN = OFF_SC + 3 * D_SC

P_CONF = 0
P_QKV = 512
P_R = 1536
P_SC = 2048
P_AFAB = 2816
P_TOTAL = 2944
AFAB_W = 128

CHUNK = 64
GLA_BLOCK = 4
CONV_SUB = 64
CONV_PAD = 16
SC_PAD = 8
TOK_TILE = 128
SLOT_ALIGN = 16
VMEM_LIMIT = 56 * 1024 * 1024


def _sigmoid(x):
    return 1.0 / (1.0 + jnp.exp(-x))


def _silu(x):
    return x * _sigmoid(x)


def _log_sigmoid(z):
    return jnp.minimum(z, 0.0) - jnp.log1p(jnp.exp(-jnp.abs(z)))


def _params(sem):
    return pltpu.CompilerParams(dimension_semantics=sem, vmem_limit_bytes=VMEM_LIMIT)


def _ada_kernel(s_ref, w_ref, b_ref, o_ref):
    s = _silu(s_ref[...])
    o_ref[0] = jnp.dot(s, w_ref[0], precision=HI, preferred_element_type=F32) + b_ref[0]


def _ada(s_in, ada_w, ada_b):
    rows = s_in.shape[0]
    tn = 1024
    return pl.pallas_call(
        _ada_kernel,
        grid=(DEPTH, 6 * D_MODEL // tn),
        in_specs=[pl.BlockSpec((rows, D_MODEL), lambda l, n: (0, 0)),
                  pl.BlockSpec((1, D_MODEL, tn), lambda l, n: (l, 0, n)),
                  pl.BlockSpec((1, 1, tn), lambda l, n: (l, 0, n))],
        out_specs=pl.BlockSpec((1, rows, tn), lambda l, n: (l, 0, n)),
        out_shape=jax.ShapeDtypeStruct((DEPTH, rows, 6 * D_MODEL), F32),
        compiler_params=_params(("arbitrary", "arbitrary")),
        name="ada",
    )(s_in, ada_w, ada_b.reshape(DEPTH, 1, 6 * D_MODEL))


def _rms_mod(x, nw, shift, scale):
    ms = jnp.mean(x * x, axis=-1, keepdims=True)
    return (x * lax.rsqrt(ms + EPS) * nw) * (1.0 + scale) + shift


def _proj_kernel(*refs, add_moe):
    if add_moe:
        (x_ref, moe_ref, g5_ref, nw_ref, sh_ref, sc_ref, w_ref,
         xo_ref, conf_ref, qkv_ref, r_ref, scu_ref, afab_ref) = refs
        x = x_ref[0] + g5_ref[0] * moe_ref[0]
        xo_ref[0] = x
    else:
        (x_ref, nw_ref, sh_ref, sc_ref, w_ref,
         conf_ref, qkv_ref, r_ref, scu_ref, afab_ref) = refs
        x = x_ref[0]
    hb = _rms_mod(x, nw_ref[...], sh_ref[0], sc_ref[0]).astype(BF16)

    def proj(lo, hi):
        return jnp.dot(hb, w_ref[:, lo:hi], preferred_element_type=F32)

    conf_ref[0] = proj(P_CONF, P_QKV).astype(BF16)
    qkv_ref[0] = proj(P_QKV, P_R).astype(BF16)
    r_ref[0] = proj(P_R, P_SC).astype(BF16)
    scu_ref[0] = proj(P_SC, P_AFAB).astype(BF16)
    afab_ref[0] = proj(P_AFAB, P_TOTAL)


def _proj(x, moe, g5, nw, shift, scale, w_perm):
    B, L, D = x.shape
    tm = min(L, 512)
    add_moe = moe is not None
    tok = lambda w: pl.BlockSpec((1, tm, w), lambda b, t: (b, t, 0))
    per_b = pl.BlockSpec((1, 1, D), lambda b, t: (b, 0, 0))
    ins, in_specs = [x], [tok(D)]
    if add_moe:
        ins += [moe, g5]
        in_specs += [tok(D), per_b]
    ins += [nw.reshape(1, D), shift, scale, w_perm]
    in_specs += [pl.BlockSpec((1, D), lambda b, t: (0, 0)), per_b, per_b,
                 pl.BlockSpec((D, P_TOTAL), lambda b, t: (0, 0))]
    widths = [(2 * D_CONF, BF16), (P_R - P_QKV, BF16), (GLA_V, BF16), (3 * D_SC, BF16), (AFAB_W, F32)]
    out_shape = [jax.ShapeDtypeStruct((B, L, w), dt) for w, dt in widths]
    out_specs = [tok(w) for w, _ in widths]
    if add_moe:
        out_shape = [jax.ShapeDtypeStruct((B, L, D), F32)] + out_shape
        out_specs = [tok(D)] + out_specs
    outs = pl.pallas_call(
        functools.partial(_proj_kernel, add_moe=add_moe),
        grid=(B, L // tm),
        in_specs=in_specs, out_specs=out_specs, out_shape=out_shape,
        compiler_params=_params(("parallel", "parallel")),
        name="proj",
    )(*ins)
    return tuple(outs) if add_moe else (None,) + tuple(outs)


def _conv_kernel(conf_ref, scu_ref, cw_ref, cb_ref, lnw_ref, lnb_ref, sw_ref, sb_ref,
                 yc_ref, ys_ref, zp_ref, pz_ref, *, row_len):
    L = conf_ref.shape[1]
    n_rows = L // row_len
    n_sub = row_len // CONV_SUB
    C = D_CONF

    zeros_pad = jnp.zeros((CONV_PAD, C), F32)

    def fill(r, carry):
        zp_ref[r, 0:CONV_PAD, :] = zeros_pad
        zp_ref[r, CONV_PAD + row_len:, :] = zeros_pad
        for s in range(n_sub):
            start = pl.multiple_of(r * row_len + s * CONV_SUB, CONV_SUB)
            u = conf_ref[0, pl.ds(start, CONV_SUB), :].astype(F32)
            lo = CONV_PAD + s * CONV_SUB
            zp_ref[r, lo:lo + CONV_SUB, :] = u[:, :C] * _sigmoid(u[:, C:])
        return carry

    lax.fori_loop(0, n_rows, fill, 0)

    def conv_row(r, carry):
        for s in range(n_sub):
            base = CONV_PAD - CONF_K // 2 + s * CONV_SUB
            acc = jnp.zeros((CONV_SUB, C), F32)
            for j in range(CONF_K):
                acc = acc + cw_ref[j:j + 1, :] * zp_ref[r, base + j:base + j + CONV_SUB, :]
            acc = acc + cb_ref[...]
            mu = jnp.mean(acc, axis=-1, keepdims=True)
            cen = acc - mu
            var = jnp.mean(cen * cen, axis=-1, keepdims=True)
            y = cen * lax.rsqrt(var + EPS) * lnw_ref[...] + lnb_ref[...]
            start = pl.multiple_of(r * row_len + s * CONV_SUB, CONV_SUB)
            yc_ref[0, pl.ds(start, CONV_SUB), :] = _silu(y).astype(BF16)
        return carry

    lax.fori_loop(0, n_rows, conv_row, 0)

    n_blk = L // CONV_SUB
    pz_ref[0:SC_PAD, :] = jnp.zeros((SC_PAD, D_SC), F32)
    pz_ref[SC_PAD + L:, :] = jnp.zeros((SC_PAD, D_SC), F32)

    def fill_p(i, carry):
        start = pl.multiple_of(i * CONV_SUB, CONV_SUB)
        u = scu_ref[0, pl.ds(start, CONV_SUB), :].astype(F32)
        pz_ref[pl.ds(SC_PAD + start, CONV_SUB), :] = u[:, D_SC:2 * D_SC] * u[:, 2 * D_SC:]
        return carry

    lax.fori_loop(0, n_blk, fill_p, 0)

    def sc_blk(i, carry):
        start = pl.multiple_of(i * CONV_SUB, CONV_SUB)
        win = pz_ref[pl.ds(start, CONV_SUB + 2 * SC_PAD), :]
        acc = sb_ref[...] + sw_ref[0:1, :] * win[SC_PAD - 1:SC_PAD - 1 + CONV_SUB]
        acc = acc + sw_ref[1:2, :] * win[SC_PAD:SC_PAD + CONV_SUB]
        acc = acc + sw_ref[2:3, :] * win[SC_PAD + 1:SC_PAD + 1 + CONV_SUB]
        bg = scu_ref[0, pl.ds(start, CONV_SUB), 0:D_SC].astype(F32)
        ys_ref[0, pl.ds(start, CONV_SUB), :] = (bg * acc).astype(BF16)
        return carry

    lax.fori_loop(0, n_blk, sc_blk, 0)


def _convs(conf, scu, cw, cb, lnw, lnb, sw, sb, row_len):
    B, L, _ = conf.shape
    n_rows = L // row_len
    full = lambda a: pl.BlockSpec(a.shape, lambda b: (0,) * a.ndim)
    cw_p = jnp.zeros((32, D_CONF), F32).at[:CONF_K].set(cw)
    sw_p = jnp.zeros((8, D_SC), F32).at[:SC_K].set(sw)
    small = [cw_p, cb.reshape(1, D_CONF), lnw.reshape(1, D_CONF), lnb.reshape(1, D_CONF),
             sw_p, sb.reshape(1, D_SC)]
    return pl.pallas_call(
        functools.partial(_conv_kernel, row_len=row_len),
        grid=(B,),
        in_specs=[pl.BlockSpec((1, L, 2 * D_CONF), lambda b: (b, 0, 0)),
                  pl.BlockSpec((1, L, 3 * D_SC), lambda b: (b, 0, 0))] + [full(a) for a in small],
        out_specs=[pl.BlockSpec((1, L, D_CONF), lambda b: (b, 0, 0)),
                   pl.BlockSpec((1, L, D_SC), lambda b: (b, 0, 0))],
        out_shape=[jax.ShapeDtypeStruct((B, L, D_CONF), BF16),
                   jax.ShapeDtypeStruct((B, L, D_SC), BF16)],
        scratch_shapes=[pltpu.VMEM((n_rows, row_len + 2 * CONV_PAD, D_CONF), F32),
                        pltpu.VMEM((L + 2 * SC_PAD, D_SC), F32)],
        compiler_params=_params(("parallel",)),
        name="convs",
    )(conf, scu, *small)


def _gla_kernel(qkv_ref, afab_ref, r_ref, s0f_ref, s0b_ref, wab_ref, bab_ref, gnw_ref,
                og_ref, sff_ref, sfb_ref, p_ref, qif_ref, qib_ref, spf_ref, ub_ref, gb_ref):
    L = qkv_ref.shape[1]
    C = CHUNK
    G = min(GLA_BLOCK, L // C)
    R = G * C
    n_blk = L // R
    mid = C // 2
    scale = GLA_DK ** -0.5
    nt = (((1,), (1,)), ((), ()))
    ii = lax.broadcasted_iota(jnp.int32, (R, R), 0)
    jj = lax.broadcasted_iota(jnp.int32, (R, R), 1)
    same = (ii // C) == (jj // C)
    tri_f = jnp.where(same, jnp.where(ii >= jj, 1.0, 0.0), 0.0).astype(BF16)
    tri_b = jnp.where(same, jnp.where(jj >= ii, 1.0, 0.0), 0.0).astype(BF16)
    PK = 2 * GLA_DK
    PV = 2 * GLA_DV
    ci = lax.broadcasted_iota(jnp.int32, (C, 2 * C), 0)
    cj = lax.broadcasted_iota(jnp.int32, (C, 2 * C), 1) % C
    lower = ci >= cj
    upper = cj >= ci
    kr = lax.broadcasted_iota(jnp.int32, (2 * C, PK), 0) // C
    kc = lax.broadcasted_iota(jnp.int32, (2 * C, PK), 1) // GLA_DK
    key_diag = kr == kc
    vr = lax.broadcasted_iota(jnp.int32, (2 * C, PV), 0) // C
    vc = lax.broadcasted_iota(jnp.int32, (2 * C, PV), 1) // GLA_DV
    val_diag = vr == vc

    def col_bcast(row_vec):
        return jnp.broadcast_to(row_vec, (GLA_DV, GLA_QK)).T

    def chunk_cumsum(tri, la):
        hi = la.astype(BF16)
        lo = (la - hi.astype(F32)).astype(BF16)
        both = jnp.dot(tri, jnp.concatenate([hi, lo], axis=1), preferred_element_type=F32)
        return both[:, :GLA_QK] + both[:, GLA_QK:]

    def v_pair(rows, p):
        return qkv_ref[0, rows, 2 * GLA_QK + p * PV:2 * GLA_QK + (p + 1) * PV]

    def pass1(i, s_f):
        r0 = pl.multiple_of(i * R, R)
        ab = afab_ref[0, pl.ds(r0, R), 0:2 * GLA_RANK].astype(BF16)
        z = jnp.dot(ab, wab_ref[...], preferred_element_type=F32) + bab_ref[...]
        la = _log_sigmoid(z) * (1.0 / GLA_TAU)
        b_f = chunk_cumsum(tri_f, la[:, :GLA_QK])
        b_b = chunk_cumsum(tri_b, la[:, GLA_QK:])
        q_all = qkv_ref[0, pl.ds(r0, R), 0:GLA_QK].astype(F32) * scale
        k_all = qkv_ref[0, pl.ds(r0, R), GLA_QK:2 * GLA_QK].astype(F32)
        for g in range(G):
            c = i * G + g
            rows = pl.ds(pl.multiple_of(r0 + g * C, C), C)
            sl = slice(g * C, (g + 1) * C)
            q, k = q_all[sl], k_all[sl]
            scaled = []
            for b, tot_row in ((b_f[sl], C - 1), (b_b[sl], 0)):
                ref_row = b[mid:mid + 1]
                tot = b[tot_row:tot_row + 1]
                q_rel = q * jnp.exp(b - ref_row)
                k_rel = k * jnp.exp(ref_row - b)
                scaled.append((q_rel.astype(BF16), k_rel.astype(BF16),
                               (q_rel * jnp.exp(ref_row)).astype(BF16),
                               (k_rel * jnp.exp(tot - ref_row)).T.astype(BF16),
                               tot))
            (qf, kf, qif, kuf_t, g_f), (qb, kb, qib, kub_t, g_b) = scaled
            qif_ref[rows, :] = qif
            qib_ref[rows, :] = qib
            p_parts, uf, ub = [], [], []
            for p in range(GLA_HEADS // 2):
                ks = slice(p * PK, (p + 1) * PK)
                kbd_f = jnp.where(key_diag, jnp.concatenate([kf[:, ks]] * 2, axis=0), 0)
                kbd_b = jnp.where(key_diag, jnp.concatenate([kb[:, ks]] * 2, axis=0), 0)
                s_fwd = lax.dot_general(qf[:, ks], kbd_f, nt, preferred_element_type=F32)
                s_bwd = lax.dot_general(qb[:, ks], kbd_b, nt, preferred_element_type=F32)
                p_parts.append((jnp.where(lower, s_fwd, 0.0) + jnp.where(upper, s_bwd, 0.0)).astype(BF16))
                u = jnp.dot(jnp.concatenate([kuf_t[ks, :], kub_t[ks, :]], axis=0), v_pair(rows, p),
                            preferred_element_type=F32)
                for blk, dst in ((u[:PK], uf), (u[PK:], ub)):
                    dst += [blk[:GLA_DK, :GLA_DV], blk[GLA_DK:, GLA_DV:]]
            p_ref[rows, :] = jnp.concatenate(p_parts, axis=1)
            ub_ref[c] = jnp.concatenate(ub, axis=0)
            gb_ref[c] = jnp.broadcast_to(g_b, (8, GLA_QK))
            spf_ref[c] = s_f.astype(BF16)
            s_f = col_bcast(jnp.exp(g_f)) * s_f + jnp.concatenate(uf, axis=0)
        return s_f

    sff_ref[0] = lax.fori_loop(0, n_blk, pass1, s0f_ref[0])

    def pass2(t, s_b):
        i = n_blk - 1 - t
        for g in reversed(range(G)):
            c = i * G + g
            rows = pl.ds(pl.multiple_of(i * R + g * C, C), C)
            spf = spf_ref[c]
            snb = s_b.astype(BF16)
            zero_blk = jnp.zeros((GLA_DK, GLA_DV), BF16)

            def pair_state(s, p):
                top = jnp.concatenate([s[2 * p * GLA_DK:(2 * p + 1) * GLA_DK], zero_blk], axis=1)
                bot = jnp.concatenate([zero_blk, s[(2 * p + 1) * GLA_DK:(2 * p + 2) * GLA_DK]], axis=1)
                return jnp.concatenate([top, bot], axis=0)

            outs = []
            for p in range(GLA_HEADS // 2):
                ks = slice(p * PK, (p + 1) * PK)
                vbd = jnp.where(val_diag, jnp.concatenate([v_pair(rows, p)] * 2, axis=0), 0)
                lhs = jnp.concatenate([p_ref[rows, ks], qif_ref[rows, ks], qib_ref[rows, ks]], axis=1)
                rhs = jnp.concatenate([vbd, pair_state(spf, p), pair_state(snb, p)], axis=0)
                o_pair = jnp.dot(lhs, rhs, preferred_element_type=F32)
                for o in (o_pair[:, :GLA_DV], o_pair[:, GLA_DV:]):
                    ms = jnp.mean(o * o, axis=-1, keepdims=True)
                    outs.append(o * lax.rsqrt(ms + EPS))
            o_all = jnp.concatenate(outs, axis=-1) * gnw_ref[...]
            og_ref[0, rows, :] = (o_all * _silu(r_ref[0, rows, :].astype(F32))).astype(BF16)
            s_b = col_bcast(jnp.exp(gb_ref[c][0:1])) * s_b + ub_ref[c]
        return s_b

    sfb_ref[0] = lax.fori_loop(0, n_blk, pass2, s0b_ref[0])


def _gla(qkv, afab, r, s0f, s0b, wab, bab, gnw):
    B, L, _ = qkv.shape
    n = L // CHUNK
    tok = lambda w: pl.BlockSpec((1, L, w), lambda b: (b, 0, 0))
    st = pl.BlockSpec((1, GLA_QK, GLA_DV), lambda b: (b, 0, 0))
    full = lambda a: pl.BlockSpec(a.shape, lambda b: (0,) * a.ndim)
    small = [wab, bab, gnw.reshape(1, GLA_V)]
    return pl.pallas_call(
        _gla_kernel,
        grid=(B,),
        in_specs=[tok(P_R - P_QKV), tok(AFAB_W), tok(GLA_V), st, st] + [full(a) for a in small],
        out_specs=[tok(GLA_V), st, st],
        out_shape=[jax.ShapeDtypeStruct((B, L, GLA_V), BF16),
                   jax.ShapeDtypeStruct((B, GLA_QK, GLA_DV), F32),
                   jax.ShapeDtypeStruct((B, GLA_QK, GLA_DV), F32)],
        scratch_shapes=[pltpu.VMEM((L, GLA_QK), BF16),
                        pltpu.VMEM((L, GLA_QK), BF16), pltpu.VMEM((L, GLA_QK), BF16),
                        pltpu.VMEM((n, GLA_QK, GLA_DV), BF16),
                        pltpu.VMEM((n, GLA_QK, GLA_DV), F32),
                        pltpu.VMEM((n, 8, GLA_QK), F32)],
        compiler_params=_params(("parallel",)),
        name="gla",
    )(qkv, afab, r, s0f, s0b, *small)


def _outproj_kernel(yc_ref, og_ref, ys_ref, x_ref, wo_ref, g2_ref, nw_ref, sh_ref, sc_ref, rwt_ref,
                    xo_ref, h2_ref, lt_ref):
    y = jnp.dot(yc_ref[0], wo_ref[0:D_CONF, :], preferred_element_type=F32)
    y = y + jnp.dot(og_ref[0], wo_ref[D_CONF:D_CONF + GLA_V, :], preferred_element_type=F32)
    y = y + jnp.dot(ys_ref[0], wo_ref[D_CONF + GLA_V:, :], preferred_element_type=F32)
    x = x_ref[0] + g2_ref[0] * y
    xo_ref[0] = x
    hb = _rms_mod(x, nw_ref[...], sh_ref[0], sc_ref[0]).astype(BF16)
    h2_ref[0] = hb
    lt_ref[0] = lax.dot_general(rwt_ref[...], hb, (((1,), (1,)), ((), ())),
                                preferred_element_type=F32)


def _outproj(yc, og, ys, x, wo, g2, nw, shift, scale, rwt):
    B, L, D = x.shape
    tm = min(L, 512)
    tok = lambda w: pl.BlockSpec((1, tm, w), lambda b, t: (b, t, 0))
    per_b = pl.BlockSpec((1, 1, D), lambda b, t: (b, 0, 0))
    return pl.pallas_call(
        _outproj_kernel,
        grid=(B, L // tm),
        in_specs=[tok(D_CONF), tok(GLA_V), tok(D_SC), tok(D),
                  pl.BlockSpec((D, D), lambda b, t: (0, 0)), per_b,
                  pl.BlockSpec((1, D), lambda b, t: (0, 0)), per_b, per_b,
                  pl.BlockSpec((N_EXPERTS, D), lambda b, t: (0, 0))],
        out_specs=[tok(D), tok(D), pl.BlockSpec((1, N_EXPERTS, tm), lambda b, t: (b, 0, t))],
        out_shape=[jax.ShapeDtypeStruct((B, L, D), F32), jax.ShapeDtypeStruct((B, L, D), BF16),
                   jax.ShapeDtypeStruct((B, N_EXPERTS, L), F32)],
        compiler_params=_params(("parallel", "parallel")),
        name="outproj",
    )(yc, og, ys, x, wo, g2, nw.reshape(1, D), shift, scale, rwt)


def _lane_cumsum(x):
    n = x.shape[-1]
    lane = lax.broadcasted_iota(jnp.int32, x.shape, x.ndim - 1)
    s = 1
    while s < n:
        x = x + jnp.where(lane >= s, pltpu.roll(x, s, axis=x.ndim - 1), 0)
        s *= 2
    return x


def _route_kernel(lt_ref, pos_ref, gate_ref, bnd_ref, *, cap):
    lt = lt_ref[0]
    L = lt.shape[1]
    e = jnp.exp(lt - jnp.max(lt, axis=0, keepdims=True))
    aff = e / jnp.sum(e, axis=0, keepdims=True)

    def search(i, t):
        cand = t | lax.shift_left(jnp.int32(1), 30 - i)
        cnt = jnp.sum((aff >= pltpu.bitcast(cand, F32)).astype(jnp.int32), axis=1, keepdims=True)
        return jnp.where(cnt >= cap, cand, t)

    thr_bits = lax.fori_loop(0, 31, search, jnp.zeros((lt.shape[0], 1), jnp.int32))
    thr = pltpu.bitcast(thr_bits, F32)
    gt = aff > thr
    eq = aff == thr
    need = cap - jnp.sum(gt.astype(jnp.int32), axis=1, keepdims=True)
    eq_i = eq.astype(jnp.int32)
    eq_rank = _lane_cumsum(eq_i) - eq_i
    sel = gt | (eq & (eq_rank < need))
    sel_i = sel.astype(jnp.int32)
    slot = _lane_cumsum(sel_i) - sel_i
    pos_ref[0] = jnp.where(sel, slot, -1)
    gate_ref[0] = aff
    bnd_ref[0] = jnp.concatenate([slot[:, k * TOK_TILE:k * TOK_TILE + 1] for k in range(L // TOK_TILE)],
                                 axis=1)


def _route(lt, cap):
    B, E, L = lt.shape
    spec = pl.BlockSpec((1, E, L), lambda b: (b, 0, 0))
    nt = L // TOK_TILE
    return pl.pallas_call(
        functools.partial(_route_kernel, cap=cap),
        grid=(B,),
        in_specs=[spec], out_specs=[spec, spec, pl.BlockSpec((1, E, nt), lambda b: (b, 0, 0))],
        out_shape=[jax.ShapeDtypeStruct((B, E, L), jnp.int32), jax.ShapeDtypeStruct((B, E, L), F32),
                   jax.ShapeDtypeStruct((B, E, nt), jnp.int32)],
        compiler_params=_params(("parallel",)),
        name="route",
    )(lt)


def _window(lo, hi, width, cap):
    a0 = jnp.minimum((lo // SLOT_ALIGN) * SLOT_ALIGN, cap - width)
    n = jnp.where(hi > lo, (hi - a0 + width - 1) // width, 0)
    return a0, n


def _one_hot_rows(pos_row, start, width, first_row=None):
    slot = lax.broadcasted_iota(jnp.int32, (width, pos_row.shape[1]), 0) + start
    on = 1.0 if first_row is None else jnp.where(slot >= first_row, 1.0, 0.0)
    return jnp.where(slot == pos_row, on, 0.0).astype(BF16)


def _gather_kernel(bnd_ref, pos_ref, h_ref, xs_ref, *, cap, tile, width):
    b = pl.program_id(0)
    L = h_ref.shape[1]
    nt = L // TOK_TILE
    step = tile // TOK_TILE
    n_tiles = L // tile
    xs_ref[...] = jnp.zeros_like(xs_ref)

    def tile_body(k, carry):
        toks = pl.ds(pl.multiple_of(k * tile, tile), tile)
        h_tile = h_ref[0, toks, :]
        ms, wins = [], []
        for e in range(N_EXPERTS):
            base = (b * N_EXPERTS + e) * nt
            lo = bnd_ref[base + k * step]
            hi = jnp.where(k + 1 < n_tiles, bnd_ref[base + jnp.minimum(k + 1, n_tiles - 1) * step], cap)
            a0, n = _window(lo, hi, width, cap)
            a0 = pl.multiple_of(a0, SLOT_ALIGN)
            pos_row = pos_ref[0, e:e + 1, toks]
            ms.append(_one_hot_rows(pos_row, a0, width))
            wins.append((a0, n, pos_row))
        part = jnp.dot(jnp.concatenate(ms, axis=0), h_tile, preferred_element_type=F32)
        for e, (a0, n, pos_row) in enumerate(wins):
            xs_ref[0, e, pl.ds(a0, width), :] += part[e * width:(e + 1) * width].astype(BF16)

            def extra(w, c, e=e, a0=a0, pos_row=pos_row):
                first = a0 + w * width
                start = pl.multiple_of(jnp.minimum(first, cap - width), SLOT_ALIGN)
                m = _one_hot_rows(pos_row, start, width, first)
                xs_ref[0, e, pl.ds(start, width), :] += jnp.dot(
                    m, h_ref[0, toks, :], preferred_element_type=F32).astype(BF16)
                return c

            lax.fori_loop(1, n, extra, 0)
        return carry

    lax.fori_loop(0, n_tiles, tile_body, 0)


def _gather(bnd, pos, h2, cap):
    B, L, D = h2.shape
    E = N_EXPERTS
    tile = min(L, 256)
    width = min(cap, 64)
    return pl.pallas_call(
        functools.partial(_gather_kernel, cap=cap, tile=tile, width=width),
        grid_spec=pltpu.PrefetchScalarGridSpec(
            num_scalar_prefetch=1, grid=(B,),
            in_specs=[pl.BlockSpec((1, E, L), lambda b, s: (b, 0, 0)),
                      pl.BlockSpec((1, L, D), lambda b, s: (b, 0, 0))],
            out_specs=pl.BlockSpec((1, E, cap, D), lambda b, s: (b, 0, 0, 0))),
        out_shape=jax.ShapeDtypeStruct((B, E, cap, D), BF16),
        compiler_params=_params(("arbitrary",)),
        name="gather",
    )(bnd.reshape(-1), pos, h2)


def _ffn_kernel(pos_ref, gate_ref, xs_ref, w1_ref, w3_ref, w2_ref, y_ref, *, cap):
    bb = xs_ref.shape[0]
    L = pos_ref.shape[3]
    slot = lax.broadcasted_iota(jnp.int32, (cap, L), 0)
    gates = [jnp.sum(jnp.where(slot == pos_ref[i, 0], gate_ref[i, 0], 0.0), axis=1, keepdims=True)
             for i in range(bb)]
    gate = jnp.concatenate(gates, axis=0)
    xs = xs_ref[:, 0].reshape(bb * cap, xs_ref.shape[3])
    hid = _silu(jnp.dot(xs, w1_ref[0], preferred_element_type=F32))
    hid = (hid * jnp.dot(xs, w3_ref[0], preferred_element_type=F32)).astype(BF16)
    y = (jnp.dot(hid, w2_ref[0], preferred_element_type=F32) * gate).astype(BF16)
    y_ref[:, 0] = y.reshape(bb, cap, y.shape[1])


def _ffn(pos, gate, xs, w1, w3, w2, bb):
    B, E, cap, D = xs.shape
    L = pos.shape[2]
    row = pl.BlockSpec((bb, 1, 1, L), lambda e, b: (b, e, 0, 0))
    slab = pl.BlockSpec((bb, 1, cap, D), lambda e, b: (b, e, 0, 0))
    wspec = pl.BlockSpec((1, D, D), lambda e, b: (e, 0, 0))
    return pl.pallas_call(
        functools.partial(_ffn_kernel, cap=cap),
        grid=(E, B // bb),
        in_specs=[row, row, slab, wspec, wspec, wspec],
        out_specs=slab,
        out_shape=jax.ShapeDtypeStruct((B, E, cap, D), BF16),
        compiler_params=_params(("parallel", "parallel")),
        name="ffn",
    )(pos.reshape(B, E, 1, L), gate.reshape(B, E, 1, L), xs, w1, w3, w2)


def _combine_kernel(bnd_ref, pos_ref, y_ref, out_ref, *, cap, width):
    b = pl.program_id(0)
    L = out_ref.shape[1]
    nt = L // TOK_TILE
    tn = (((0,), (0,)), ((), ()))

    def tile_body(t, carry):
        toks = pl.ds(pl.multiple_of(t * TOK_TILE, TOK_TILE), TOK_TILE)
        ms, ys, wins = [], [], []
        for e in range(N_EXPERTS):
            base = (b * N_EXPERTS + e) * nt
            lo = bnd_ref[base + t]
            hi = jnp.where(t + 1 < nt, bnd_ref[base + jnp.minimum(t + 1, nt - 1)], cap)
            a0, n = _window(lo, hi, width, cap)
            a0 = pl.multiple_of(a0, SLOT_ALIGN)
            pos_row = pos_ref[0, e:e + 1, toks]
            ms.append(_one_hot_rows(pos_row, a0, width))
            ys.append(y_ref[0, e, pl.ds(a0, width), :])
            wins.append((a0, n, pos_row))
        out_ref[0, toks, :] = lax.dot_general(jnp.concatenate(ms, axis=0), jnp.concatenate(ys, axis=0), tn,
                                              preferred_element_type=F32)
        for e, (a0, n, pos_row) in enumerate(wins):
            def extra(w, c, e=e, a0=a0, pos_row=pos_row):
                first = a0 + w * width
                start = pl.multiple_of(jnp.minimum(first, cap - width), SLOT_ALIGN)
                m = _one_hot_rows(pos_row, start, width, first)
                out_ref[0, toks, :] += lax.dot_general(m, y_ref[0, e, pl.ds(start, width), :], tn,
                                                       preferred_element_type=F32)
                return c

            lax.fori_loop(1, n, extra, 0)
        return carry

    lax.fori_loop(0, nt, tile_body, 0)


def _combine(bnd, pos, y, L):
    B, E, cap, D = y.shape
    width = min(cap, 64)
    return pl.pallas_call(
        functools.partial(_combine_kernel, cap=cap, width=width),
        grid_spec=pltpu.PrefetchScalarGridSpec(
            num_scalar_prefetch=1, grid=(B,),
            in_specs=[pl.BlockSpec((1, E, L), lambda b, s: (b, 0, 0)),
                      pl.BlockSpec((1, E, cap, D), lambda b, s: (b, 0, 0, 0))],
            out_specs=pl.BlockSpec((1, L, D), lambda b, s: (b, 0, 0))),
        out_shape=jax.ShapeDtypeStruct((B, L, D), F32),
        compiler_params=_params(("arbitrary",)),
        name="combine",
    )(bnd.reshape(-1), pos, y)


def _moe(lt, h2, w1, w3, w2, cap, bb):
    pos, gate, bnd = _route(lt, cap)
    xs = _gather(bnd, pos, h2, cap)
    y = _ffn(pos, gate, xs, w1, w3, w2, bb)
    return _combine(bnd, pos, y, h2.shape[1])


def _final_kernel(x_ref, moe_ref, g5_ref, nw_ref, o_ref):
    x = x_ref[0] + g5_ref[0] * moe_ref[0]
    ms = jnp.mean(x * x, axis=-1, keepdims=True)
    o_ref[0] = x * lax.rsqrt(ms + EPS) * nw_ref[...]


def _final(x, moe, g5, nw):
    B, L, D = x.shape
    tm = 512
    tok = pl.BlockSpec((1, tm, D), lambda b, t: (b, t, 0))
    return pl.pallas_call(
        _final_kernel,
        grid=(B, L // tm),
        in_specs=[tok, tok, pl.BlockSpec((1, 1, D), lambda b, t: (b, 0, 0)),
                  pl.BlockSpec((1, D), lambda b, t: (0, 0))],
        out_specs=tok,
        out_shape=jax.ShapeDtypeStruct((B, L, D), F32),
        compiler_params=_params(("parallel", "parallel")),
        name="final",
    )(x, moe, g5, nw.reshape(1, D))


def _permute_w_in(w):
    cols = jnp.concatenate([w[:, 0:OFF_AF], w[:, OFF_R:OFF_SC], w[:, OFF_SC:D_IN], w[:, OFF_AF:OFF_R]],
                           axis=1)
    return jnp.pad(cols, ((0, 0), (0, P_TOTAL - D_IN))).astype(BF16)


def kernel(x, c, ctx, c_ctx, ada_w, ada_b, norm1_w, norm2_w, w_in, conf_dw_w, conf_dw_b, conf_ln_w,
           conf_ln_b, gla_wa_f, gla_ba_f, gla_wa_b, gla_ba_b, gla_gn_w, sc_w, sc_b, w_out, router_w,
           exp_w1, exp_w3, exp_w2, final_norm_w):
    B, L, D = x.shape
    Lc = ctx.shape[1]
    cap = EC_FACTOR * L // N_EXPERTS
    cap_c = EC_FACTOR * Lc // N_EXPERTS

    cond = jnp.concatenate([c, c_ctx[None, :], jnp.zeros((7, D), F32)], axis=0)
    mods = _ada(cond, ada_w, ada_b)

    xc = ctx
    moe = moe_c = g5 = g5c = None
    zero_state = jnp.zeros((B, GLA_QK, GLA_DV), F32)
    for i in range(DEPTH):
        last = i == DEPTH - 1
        m_lat = [mods[i, :B, j * D:(j + 1) * D].reshape(B, 1, D) for j in range(6)]
        m_ctx = [jnp.broadcast_to(mods[i, B, j * D:(j + 1) * D].reshape(1, 1, D), (B, 1, D))
                 for j in range(6)]
        w_perm = _permute_w_in(w_in[i])
        wo = w_out[i].astype(BF16)
        rwt = router_w[i].T.astype(BF16)
        zero_wa = jnp.zeros((GLA_RANK, GLA_QK), F32)
        wa = jnp.block([[gla_wa_f[i], zero_wa], [zero_wa, gla_wa_b[i]]]).astype(BF16)
        ba = jnp.concatenate([gla_ba_f[i], gla_ba_b[i]]).reshape(1, 2 * GLA_QK)
        w1, w3, w2 = exp_w1[i].astype(BF16), exp_w3[i].astype(BF16), exp_w2[i].astype(BF16)
        conv_w = (conf_dw_w[i], conf_dw_b[i], conf_ln_w[i], conf_ln_b[i], sc_w[i], sc_b[i])

        xc_new, c_conf, c_qkv, c_r, c_sc, c_afab = _proj(xc, moe_c, g5c, norm1_w[i], m_ctx[0], m_ctx[1],
                                                         w_perm)
        xc = xc if xc_new is None else xc_new
        c_og, c_sf, c_sb = _gla(c_qkv, c_afab, c_r, zero_state, zero_state, wa, ba, gla_gn_w[i])

        x_new, conf, qkv, r, scu, afab = _proj(x, moe, g5, norm1_w[i], m_lat[0], m_lat[1], w_perm)
        x = x if x_new is None else x_new
        yc, ys = _convs(conf, scu, *conv_w, row_len=GRID_W)
        og, _, _ = _gla(qkv, afab, r, c_sf, c_sb, wa, ba, gla_gn_w[i])
        x, h2, lt = _outproj(yc, og, ys, x, wo, m_lat[2], norm2_w[i], m_lat[3], m_lat[4], rwt)
        moe = _moe(lt, h2, w1, w3, w2, cap, 2)
        g5 = m_lat[5]

        if not last:
            c_yc, c_ys = _convs(c_conf, c_sc, *conv_w, row_len=Lc)
            xc, c_h2, c_lt = _outproj(c_yc, c_og, c_ys, xc, wo, m_ctx[2], norm2_w[i], m_ctx[3], m_ctx[4],
                                      rwt)
            moe_c = _moe(c_lt, c_h2, w1, w3, w2, cap_c, B)
            g5c = m_ctx[5]
    return _final(x, moe, g5, final_norm_w)
```

```python
import functools

import jax
import jax.numpy as jnp
from jax import lax
from jax.experimental import pallas as pl
from jax.experimental.pallas import tpu as pltpu

F32 = jnp.float32
BF16 = jnp.bfloat16
HI = lax.Precision.HIGHEST

D_MODEL = 1024
DEPTH = 2
GRID_W = 64
D_CONF = 256
D_SC = 256
GLA_HEADS = 4
GLA_DK = 64
GLA_DV = 128
GLA_QK = GLA_HEADS * GLA_DK
GLA_V = GLA_HEADS * GLA_DV
GLA_RANK = 16
GLA_TAU = 16.0
CONF_K = 31
SC_K = 3
N_EXPERTS = 16
EC_FACTOR = 2
EPS = 1e-6

OFF_Q = 2 * D_CONF
OFF_K = OFF_Q + GLA_QK
OFF_V = OFF_K + GLA_QK
OFF_AF = OFF_V + GLA_V
OFF_AB = OFF_AF + GLA_RANK
OFF_R = OFF_AB + GLA_RANK
OFF_SC = OFF_R + GLA_V
D_IN = OFF_SC + 3 * D_SC

P_CONF = 0
P_QKV = 512
P_R = 1536
P_SC = 2048
P_AFAB = 2816
P_TOTAL = 2944
AFAB_W = 128

CHUNK = 64
GLA_BLOCK = 4
CONV_SUB = 64
SUBLANES = 8
CONV_PAD = 16
SC_PAD = 8
TOK_TILE = 128
SLOT_ALIGN = 16
VMEM_LIMIT = 56 * 1024 * 1024


def _sigmoid(x):
    return 1.0 / (1.0 + jnp.exp(-x))


def _silu(x):
    return x * _sigmoid(x)


def _log_sigmoid(z):
    return jnp.minimum(z, 0.0) - jnp.log1p(jnp.exp(-jnp.abs(z)))


def _params(sem):
    return pltpu.CompilerParams(dimension_semantics=sem, vmem_limit_bytes=VMEM_LIMIT)


def _ada_kernel(s_ref, w_ref, b_ref, o_ref):
    s = _silu(s_ref[...])
    o_ref[0] = jnp.dot(s, w_ref[0], precision=HI, preferred_element_type=F32) + b_ref[0]


def _ada(s_in, ada_w, ada_b):
    rows = s_in.shape[0]
    tn = 1024
    return pl.pallas_call(
        _ada_kernel,
        grid=(DEPTH, 6 * D_MODEL // tn),
        in_specs=[pl.BlockSpec((rows, D_MODEL), lambda l, n: (0, 0)),
                  pl.BlockSpec((1, D_MODEL, tn), lambda l, n: (l, 0, n)),
                  pl.BlockSpec((1, 1, tn), lambda l, n: (l, 0, n))],
        out_specs=pl.BlockSpec((1, rows, tn), lambda l, n: (l, 0, n)),
        out_shape=jax.ShapeDtypeStruct((DEPTH, rows, 6 * D_MODEL), F32),
        compiler_params=_params(("arbitrary", "arbitrary")),
        name="ada",
    )(s_in, ada_w, ada_b.reshape(DEPTH, 1, 6 * D_MODEL))


def _rms_mod(x, nw, shift, scale):
    ms = jnp.mean(x * x, axis=-1, keepdims=True)
    return (x * lax.rsqrt(ms + EPS) * nw) * (1.0 + scale) + shift


def _proj_kernel(x_ref, nw_ref, sh_ref, sc_ref, w_ref, conf_ref, qkv_ref, r_ref, scu_ref, afab_ref):
    hb = _rms_mod(x_ref[0], nw_ref[...], sh_ref[0], sc_ref[0]).astype(BF16)

    def proj(lo, hi):
        return jnp.dot(hb, w_ref[:, lo:hi], preferred_element_type=F32)

    conf_ref[0] = proj(P_CONF, P_QKV).astype(BF16)
    qkv_ref[0] = proj(P_QKV, P_R).astype(BF16)
    r_ref[0] = proj(P_R, P_SC).astype(BF16)
    scu_ref[0] = proj(P_SC, P_AFAB).astype(BF16)
    afab_ref[0] = proj(P_AFAB, P_TOTAL)


def _proj(x, nw, shift, scale, w_perm):
    B, L, D = x.shape
    tm = min(L, 512)
    tok = lambda w: pl.BlockSpec((1, tm, w), lambda b, t: (b, t, 0))
    per_b = pl.BlockSpec((1, 1, D), lambda b, t: (b, 0, 0))
    widths = [(2 * D_CONF, BF16), (P_R - P_QKV, BF16), (GLA_V, BF16), (3 * D_SC, BF16), (AFAB_W, F32)]
    return pl.pallas_call(
        _proj_kernel,
        grid=(B, L // tm),
        in_specs=[tok(D), pl.BlockSpec((1, D), lambda b, t: (0, 0)), per_b, per_b,
                  pl.BlockSpec((D, P_TOTAL), lambda b, t: (0, 0))],
        out_specs=[tok(w) for w, _ in widths],
        out_shape=[jax.ShapeDtypeStruct((B, L, w), dt) for w, dt in widths],
        compiler_params=_params(("parallel", "parallel")),
        name="proj",
    )(x, nw.reshape(1, D), shift, scale, w_perm)


def _conv_kernel(conf_ref, scu_ref, cw_ref, cb_ref, lnw_ref, lnb_ref, sw_ref, sb_ref,
                 yc_ref, ys_ref, zp_ref, zs_ref, pz_ref, *, row_len):
    L = conf_ref.shape[1]
    n_rows = L // row_len
    n_sub = row_len // CONV_SUB
    C = D_CONF

    zeros_pad = jnp.zeros((CONV_PAD, C), F32)

    def fill(r, carry):
        zp_ref[r, 0:CONV_PAD, :] = zeros_pad
        zp_ref[r, CONV_PAD + row_len:, :] = zeros_pad
        for s in range(n_sub):
            start = pl.multiple_of(r * row_len + s * CONV_SUB, CONV_SUB)
            u = conf_ref[0, pl.ds(start, CONV_SUB), :].astype(F32)
            lo = CONV_PAD + s * CONV_SUB
            zp_ref[r, lo:lo + CONV_SUB, :] = u[:, :C] * _sigmoid(u[:, C:])
        return carry

    lax.fori_loop(0, n_rows, fill, 0)

    shift_len = row_len + 2 * CONV_PAD - SUBLANES

    def conv_row(r, carry):
        for sft in range(1, SUBLANES):
            zs_ref[sft, 0:shift_len, :] = zp_ref[r, sft:sft + shift_len, :]
        for s in range(n_sub):
            base = CONV_PAD - CONF_K // 2 + s * CONV_SUB
            acc = jnp.zeros((CONV_SUB, C), F32)
            for j in range(CONF_K):
                sft = (base + j) % SUBLANES
                lo = base + j - sft
                tap = zp_ref[r, lo:lo + CONV_SUB, :] if sft == 0 else zs_ref[sft, lo:lo + CONV_SUB, :]
                acc = acc + cw_ref[j:j + 1, :] * tap
            acc = acc + cb_ref[...]
            mu = jnp.mean(acc, axis=-1, keepdims=True)
            cen = acc - mu
            var = jnp.mean(cen * cen, axis=-1, keepdims=True)
            y = cen * lax.rsqrt(var + EPS) * lnw_ref[...] + lnb_ref[...]
            start = pl.multiple_of(r * row_len + s * CONV_SUB, CONV_SUB)
            yc_ref[0, pl.ds(start, CONV_SUB), :] = _silu(y).astype(BF16)
        return carry

    lax.fori_loop(0, n_rows, conv_row, 0)

    n_blk = L // CONV_SUB
    pz_ref[0:SC_PAD, :] = jnp.zeros((SC_PAD, D_SC), F32)
    pz_ref[SC_PAD + L:, :] = jnp.zeros((SC_PAD, D_SC), F32)

    def fill_p(i, carry):
        start = pl.multiple_of(i * CONV_SUB, CONV_SUB)
        u = scu_ref[0, pl.ds(start, CONV_SUB), :].astype(F32)
        pz_ref[pl.ds(SC_PAD + start, CONV_SUB), :] = u[:, D_SC:2 * D_SC] * u[:, 2 * D_SC:]
        return carry

    lax.fori_loop(0, n_blk, fill_p, 0)

    def sc_blk(i, carry):
        start = pl.multiple_of(i * CONV_SUB, CONV_SUB)
        win = pz_ref[pl.ds(start, CONV_SUB + 2 * SC_PAD), :]
        acc = sb_ref[...] + sw_ref[0:1, :] * win[SC_PAD - 1:SC_PAD - 1 + CONV_SUB]
        acc = acc + sw_ref[1:2, :] * win[SC_PAD:SC_PAD + CONV_SUB]
        acc = acc + sw_ref[2:3, :] * win[SC_PAD + 1:SC_PAD + 1 + CONV_SUB]
        bg = scu_ref[0, pl.ds(start, CONV_SUB), 0:D_SC].astype(F32)
        ys_ref[0, pl.ds(start, CONV_SUB), :] = (bg * acc).astype(BF16)
        return carry

    lax.fori_loop(0, n_blk, sc_blk, 0)


def _convs(conf, scu, cw, cb, lnw, lnb, sw, sb, row_len):
    B, L, _ = conf.shape
    n_rows = L // row_len
    full = lambda a: pl.BlockSpec(a.shape, lambda b: (0,) * a.ndim)
    cw_p = jnp.zeros((32, D_CONF), F32).at[:CONF_K].set(cw)
    sw_p = jnp.zeros((8, D_SC), F32).at[:SC_K].set(sw)
    small = [cw_p, cb.reshape(1, D_CONF), lnw.reshape(1, D_CONF), lnb.reshape(1, D_CONF),
             sw_p, sb.reshape(1, D_SC)]
    return pl.pallas_call(
        functools.partial(_conv_kernel, row_len=row_len),
        grid=(B,),
        in_specs=[pl.BlockSpec((1, L, 2 * D_CONF), lambda b: (b, 0, 0)),
                  pl.BlockSpec((1, L, 3 * D_SC), lambda b: (b, 0, 0))] + [full(a) for a in small],
        out_specs=[pl.BlockSpec((1, L, D_CONF), lambda b: (b, 0, 0)),
                   pl.BlockSpec((1, L, D_SC), lambda b: (b, 0, 0))],
        out_shape=[jax.ShapeDtypeStruct((B, L, D_CONF), BF16),
                   jax.ShapeDtypeStruct((B, L, D_SC), BF16)],
        scratch_shapes=[pltpu.VMEM((n_rows, row_len + 2 * CONV_PAD, D_CONF), F32),
                        pltpu.VMEM((SUBLANES, row_len + 2 * CONV_PAD, D_CONF), F32),
                        pltpu.VMEM((L + 2 * SC_PAD, D_SC), F32)],
        compiler_params=_params(("parallel",)),
        name="convs",
    )(conf, scu, *small)


def _gla_kernel(qkv_ref, afab_ref, r_ref, s0f_ref, s0b_ref, wab_ref, bab_ref, gnw_ref,
                og_ref, sff_ref, sfb_ref, p_ref, qif_ref, qib_ref, spf_ref, ub_ref, gb_ref):
    L = qkv_ref.shape[1]
    C = CHUNK
    G = min(GLA_BLOCK, L // C)
    R = G * C
    n_blk = L // R
    mid = C // 2
    scale = GLA_DK ** -0.5
    nt = (((1,), (1,)), ((), ()))
    ii = lax.broadcasted_iota(jnp.int32, (R, R), 0)
    jj = lax.broadcasted_iota(jnp.int32, (R, R), 1)
    same = (ii // C) == (jj // C)
    tri_f = jnp.where(same, jnp.where(ii >= jj, 1.0, 0.0), 0.0).astype(BF16)
    tri_b = jnp.where(same, jnp.where(jj >= ii, 1.0, 0.0), 0.0).astype(BF16)
    PK = 2 * GLA_DK
    PV = 2 * GLA_DV
    ci = lax.broadcasted_iota(jnp.int32, (C, 2 * C), 0)
    cj = lax.broadcasted_iota(jnp.int32, (C, 2 * C), 1) % C
    lower = ci >= cj
    upper = cj >= ci
    kr = lax.broadcasted_iota(jnp.int32, (2 * C, PK), 0) // C
    kc = lax.broadcasted_iota(jnp.int32, (2 * C, PK), 1) // GLA_DK
    key_diag = kr == kc
    vr = lax.broadcasted_iota(jnp.int32, (2 * C, PV), 0) // C
    vc = lax.broadcasted_iota(jnp.int32, (2 * C, PV), 1) // GLA_DV
    val_diag = vr == vc

    def col_bcast(row_vec):
        return jnp.broadcast_to(row_vec, (GLA_DV, GLA_QK)).T

    def chunk_cumsum(tri, la):
        hi = la.astype(BF16)
        lo = (la - hi.astype(F32)).astype(BF16)
        both = jnp.dot(tri, jnp.concatenate([hi, lo], axis=1), preferred_element_type=F32)
        return both[:, :GLA_QK] + both[:, GLA_QK:]

    def v_pair(rows, p):
        return qkv_ref[0, rows, 2 * GLA_QK + p * PV:2 * GLA_QK + (p + 1) * PV]

    def pass1(i, s_f):
        r0 = pl.multiple_of(i * R, R)
        ab = afab_ref[0, pl.ds(r0, R), 0:2 * GLA_RANK].astype(BF16)
        z = jnp.dot(ab, wab_ref[...], preferred_element_type=F32) + bab_ref[...]
        la = _log_sigmoid(z) * (1.0 / GLA_TAU)
        b_f = chunk_cumsum(tri_f, la[:, :GLA_QK])
        b_b = chunk_cumsum(tri_b, la[:, GLA_QK:])
        q_all = qkv_ref[0, pl.ds(r0, R), 0:GLA_QK].astype(F32) * scale
        k_all = qkv_ref[0, pl.ds(r0, R), GLA_QK:2 * GLA_QK].astype(F32)
        for g in range(G):
            c = i * G + g
            rows = pl.ds(pl.multiple_of(r0 + g * C, C), C)
            sl = slice(g * C, (g + 1) * C)
            q, k = q_all[sl], k_all[sl]
            scaled = []
            for b, tot_row in ((b_f[sl], C - 1), (b_b[sl], 0)):
                ref_row = b[mid:mid + 1]
                tot = b[tot_row:tot_row + 1]
                q_rel = q * jnp.exp(b - ref_row)
                k_rel = k * jnp.exp(ref_row - b)
                scaled.append((q_rel.astype(BF16), k_rel.astype(BF16),
                               (q_rel * jnp.exp(ref_row)).astype(BF16),
                               (k_rel * jnp.exp(tot - ref_row)).T.astype(BF16),
                               tot))
            (qf, kf, qif, kuf_t, g_f), (qb, kb, qib, kub_t, g_b) = scaled
            qif_ref[rows, :] = qif
            qib_ref[rows, :] = qib
            p_parts, uf, ub = [], [], []
            for p in range(GLA_HEADS // 2):
                ks = slice(p * PK, (p + 1) * PK)
                kbd_f = jnp.where(key_diag, jnp.concatenate([kf[:, ks]] * 2, axis=0), 0)
                kbd_b = jnp.where(key_diag, jnp.concatenate([kb[:, ks]] * 2, axis=0), 0)
                s_fwd = lax.dot_general(qf[:, ks], kbd_f, nt, preferred_element_type=F32)
                s_bwd = lax.dot_general(qb[:, ks], kbd_b, nt, preferred_element_type=F32)
                p_parts.append((jnp.where(lower, s_fwd, 0.0) + jnp.where(upper, s_bwd, 0.0)).astype(BF16))
                u = jnp.dot(jnp.concatenate([kuf_t[ks, :], kub_t[ks, :]], axis=0), v_pair(rows, p),
                            preferred_element_type=F32)
                for blk, dst in ((u[:PK], uf), (u[PK:], ub)):
                    dst += [blk[:GLA_DK, :GLA_DV], blk[GLA_DK:, GLA_DV:]]
            p_ref[rows, :] = jnp.concatenate(p_parts, axis=1)
            ub_ref[c] = jnp.concatenate(ub, axis=0)
            gb_ref[c] = jnp.broadcast_to(g_b, (8, GLA_QK))
            spf_ref[c] = s_f.astype(BF16)
            s_f = col_bcast(jnp.exp(g_f)) * s_f + jnp.concatenate(uf, axis=0)
        return s_f

    sff_ref[0] = lax.fori_loop(0, n_blk, pass1, s0f_ref[0])

    def pass2(t, s_b):
        i = n_blk - 1 - t
        for g in reversed(range(G)):
            c = i * G + g
            rows = pl.ds(pl.multiple_of(i * R + g * C, C), C)
            spf = spf_ref[c]
            snb = s_b.astype(BF16)
            zero_blk = jnp.zeros((GLA_DK, GLA_DV), BF16)

            def pair_state(s, p):
                top = jnp.concatenate([s[2 * p * GLA_DK:(2 * p + 1) * GLA_DK], zero_blk], axis=1)
                bot = jnp.concatenate([zero_blk, s[(2 * p + 1) * GLA_DK:(2 * p + 2) * GLA_DK]], axis=1)
                return jnp.concatenate([top, bot], axis=0)

            outs = []
            for p in range(GLA_HEADS // 2):
                ks = slice(p * PK, (p + 1) * PK)
                vbd = jnp.where(val_diag, jnp.concatenate([v_pair(rows, p)] * 2, axis=0), 0)
                lhs = jnp.concatenate([p_ref[rows, ks], qif_ref[rows, ks], qib_ref[rows, ks]], axis=1)
                rhs = jnp.concatenate([vbd, pair_state(spf, p), pair_state(snb, p)], axis=0)
                o_pair = jnp.dot(lhs, rhs, preferred_element_type=F32)
                for o in (o_pair[:, :GLA_DV], o_pair[:, GLA_DV:]):
                    ms = jnp.mean(o * o, axis=-1, keepdims=True)
                    outs.append(o * lax.rsqrt(ms + EPS))
            o_all = jnp.concatenate(outs, axis=-1) * gnw_ref[...]
            og_ref[0, rows, :] = (o_all * _silu(r_ref[0, rows, :].astype(F32))).astype(BF16)
            s_b = col_bcast(jnp.exp(gb_ref[c][0:1])) * s_b + ub_ref[c]
        return s_b

    sfb_ref[0] = lax.fori_loop(0, n_blk, pass2, s0b_ref[0])


def _gla(qkv, afab, r, s0f, s0b, wab, bab, gnw):
    B, L, _ = qkv.shape
    n = L // CHUNK
    tok = lambda w: pl.BlockSpec((1, L, w), lambda b: (b, 0, 0))
    st = pl.BlockSpec((1, GLA_QK, GLA_DV), lambda b: (b, 0, 0))
    full = lambda a: pl.BlockSpec(a.shape, lambda b: (0,) * a.ndim)
    small = [wab, bab, gnw.reshape(1, GLA_V)]
    return pl.pallas_call(
        _gla_kernel,
        grid=(B,),
        in_specs=[tok(P_R - P_QKV), tok(AFAB_W), tok(GLA_V), st, st] + [full(a) for a in small],
        out_specs=[tok(GLA_V), st, st],
        out_shape=[jax.ShapeDtypeStruct((B, L, GLA_V), BF16),
                   jax.ShapeDtypeStruct((B, GLA_QK, GLA_DV), F32),
                   jax.ShapeDtypeStruct((B, GLA_QK, GLA_DV), F32)],
        scratch_shapes=[pltpu.VMEM((L, GLA_QK), BF16),
                        pltpu.VMEM((L, GLA_QK), BF16), pltpu.VMEM((L, GLA_QK), BF16),
                        pltpu.VMEM((n, GLA_QK, GLA_DV), BF16),
                        pltpu.VMEM((n, GLA_QK, GLA_DV), F32),
                        pltpu.VMEM((n, 8, GLA_QK), F32)],
        compiler_params=_params(("parallel",)),
        name="gla",
    )(qkv, afab, r, s0f, s0b, *small)


def _outproj_kernel(yc_ref, og_ref, ys_ref, x_ref, wo_ref, g2_ref, nw_ref, sh_ref, sc_ref, rwt_ref,
                    xo_ref, h2_ref, lt_ref):
    y = jnp.dot(yc_ref[0], wo_ref[0:D_CONF, :], preferred_element_type=F32)
    y = y + jnp.dot(og_ref[0], wo_ref[D_CONF:D_CONF + GLA_V, :], preferred_element_type=F32)
    y = y + jnp.dot(ys_ref[0], wo_ref[D_CONF + GLA_V:, :], preferred_element_type=F32)
    x = x_ref[0] + g2_ref[0] * y
    xo_ref[0] = x
    hb = _rms_mod(x, nw_ref[...], sh_ref[0], sc_ref[0]).astype(BF16)
    h2_ref[0] = hb
    lt_ref[0] = lax.dot_general(rwt_ref[...], hb, (((1,), (1,)), ((), ())),
                                preferred_element_type=F32)


def _outproj(yc, og, ys, x, wo, g2, nw, shift, scale, rwt):
    B, L, D = x.shape
    tm = min(L, 512)
    tok = lambda w: pl.BlockSpec((1, tm, w), lambda b, t: (b, t, 0))
    per_b = pl.BlockSpec((1, 1, D), lambda b, t: (b, 0, 0))
    return pl.pallas_call(
        _outproj_kernel,
        grid=(B, L // tm),
        in_specs=[tok(D_CONF), tok(GLA_V), tok(D_SC), tok(D),
                  pl.BlockSpec((D, D), lambda b, t: (0, 0)), per_b,
                  pl.BlockSpec((1, D), lambda b, t: (0, 0)), per_b, per_b,
                  pl.BlockSpec((N_EXPERTS, D), lambda b, t: (0, 0))],
        out_specs=[tok(D), tok(D), pl.BlockSpec((1, N_EXPERTS, tm), lambda b, t: (b, 0, t))],
        out_shape=[jax.ShapeDtypeStruct((B, L, D), F32), jax.ShapeDtypeStruct((B, L, D), BF16),
                   jax.ShapeDtypeStruct((B, N_EXPERTS, L), F32)],
        compiler_params=_params(("parallel", "parallel")),
        name="outproj",
    )(yc, og, ys, x, wo, g2, nw.reshape(1, D), shift, scale, rwt)


def _lane_cumsum(x):
    n = x.shape[-1]
    lane = lax.broadcasted_iota(jnp.int32, x.shape, x.ndim - 1)
    s = 1
    while s < n:
        x = x + jnp.where(lane >= s, pltpu.roll(x, s, axis=x.ndim - 1), 0)
        s *= 2
    return x


def _route_kernel(lt_ref, pos_ref, gate_ref, bnd_ref, *, cap):
    lt = lt_ref[0]
    L = lt.shape[1]
    e = jnp.exp(lt - jnp.max(lt, axis=0, keepdims=True))
    aff = e / jnp.sum(e, axis=0, keepdims=True)

    def search(i, t):
        cand = t | lax.shift_left(jnp.int32(1), 30 - i)
        cnt = jnp.sum((aff >= pltpu.bitcast(cand, F32)).astype(jnp.int32), axis=1, keepdims=True)
        return jnp.where(cnt >= cap, cand, t)

    thr_bits = lax.fori_loop(0, 31, search, jnp.zeros((lt.shape[0], 1), jnp.int32))
    thr = pltpu.bitcast(thr_bits, F32)
    gt = aff > thr
    eq = aff == thr
    need = cap - jnp.sum(gt.astype(jnp.int32), axis=1, keepdims=True)
    eq_i = eq.astype(jnp.int32)
    eq_rank = _lane_cumsum(eq_i) - eq_i
    sel = gt | (eq & (eq_rank < need))
    sel_i = sel.astype(jnp.int32)
    slot = _lane_cumsum(sel_i) - sel_i
    pos_ref[0] = jnp.where(sel, slot, -1)
    gate_ref[0] = aff
    bnd_ref[0] = jnp.concatenate([slot[:, k * TOK_TILE:k * TOK_TILE + 1] for k in range(L // TOK_TILE)],
                                 axis=1)


def _route(lt, cap):
    B, E, L = lt.shape
    spec = pl.BlockSpec((1, E, L), lambda b: (b, 0, 0))
    nt = L // TOK_TILE
    return pl.pallas_call(
        functools.partial(_route_kernel, cap=cap),
        grid=(B,),
        in_specs=[spec], out_specs=[spec, spec, pl.BlockSpec((1, E, nt), lambda b: (b, 0, 0))],
        out_shape=[jax.ShapeDtypeStruct((B, E, L), jnp.int32), jax.ShapeDtypeStruct((B, E, L), F32),
                   jax.ShapeDtypeStruct((B, E, nt), jnp.int32)],
        compiler_params=_params(("parallel",)),
        name="route",
    )(lt)


def _window(lo, hi, width, cap):
    a0 = jnp.minimum((lo // SLOT_ALIGN) * SLOT_ALIGN, cap - width)
    n = jnp.where(hi > lo, (hi - a0 + width - 1) // width, 0)
    return a0, n


def _one_hot_rows(pos_row, start, width, first_row=None):
    slot = lax.broadcasted_iota(jnp.int32, (width, pos_row.shape[1]), 0) + start
    on = 1.0 if first_row is None else jnp.where(slot >= first_row, 1.0, 0.0)
    return jnp.where(slot == pos_row, on, 0.0).astype(BF16)


def _gather_kernel(bnd_ref, pos_ref, h_ref, xs_ref, *, cap, tile, width):
    b = pl.program_id(0)
    L = h_ref.shape[1]
    nt = L // TOK_TILE
    step = tile // TOK_TILE
    n_tiles = L // tile
    xs_ref[...] = jnp.zeros_like(xs_ref)

    def tile_body(k, carry):
        toks = pl.ds(pl.multiple_of(k * tile, tile), tile)
        h_tile = h_ref[0, toks, :]
        ms, wins = [], []
        for e in range(N_EXPERTS):
            base = (b * N_EXPERTS + e) * nt
            lo = bnd_ref[base + k * step]
            hi = jnp.where(k + 1 < n_tiles, bnd_ref[base + jnp.minimum(k + 1, n_tiles - 1) * step], cap)
            a0, n = _window(lo, hi, width, cap)
            a0 = pl.multiple_of(a0, SLOT_ALIGN)
            pos_row = pos_ref[0, e:e + 1, toks]
            ms.append(_one_hot_rows(pos_row, a0, width))
            wins.append((a0, n, pos_row))
        part = jnp.dot(jnp.concatenate(ms, axis=0), h_tile, preferred_element_type=F32)
        for e, (a0, n, pos_row) in enumerate(wins):
            xs_ref[0, e, pl.ds(a0, width), :] += part[e * width:(e + 1) * width].astype(BF16)

            def extra(w, c, e=e, a0=a0, pos_row=pos_row):
                first = a0 + w * width
                start = pl.multiple_of(jnp.minimum(first, cap - width), SLOT_ALIGN)
                m = _one_hot_rows(pos_row, start, width, first)
                xs_ref[0, e, pl.ds(start, width), :] += jnp.dot(
                    m, h_ref[0, toks, :], preferred_element_type=F32).astype(BF16)
                return c

            lax.fori_loop(1, n, extra, 0)
        return carry

    lax.fori_loop(0, n_tiles, tile_body, 0)


def _gather(bnd, pos, h2, cap):
    B, L, D = h2.shape
    E = N_EXPERTS
    tile = min(L, 256)
    width = min(cap, 64)
    return pl.pallas_call(
        functools.partial(_gather_kernel, cap=cap, tile=tile, width=width),
        grid_spec=pltpu.PrefetchScalarGridSpec(
            num_scalar_prefetch=1, grid=(B,),
            in_specs=[pl.BlockSpec((1, E, L), lambda b, s: (b, 0, 0)),
                      pl.BlockSpec((1, L, D), lambda b, s: (b, 0, 0))],
            out_specs=pl.BlockSpec((1, E, cap, D), lambda b, s: (b, 0, 0, 0))),
        out_shape=jax.ShapeDtypeStruct((B, E, cap, D), BF16),
        compiler_params=_params(("arbitrary",)),
        name="gather",
    )(bnd.reshape(-1), pos, h2)


def _ffn_kernel(pos_ref, gate_ref, xs_ref, w1_ref, w3_ref, w2_ref, y_ref, wb_ref, *, cap):
    bb = xs_ref.shape[0]
    L = pos_ref.shape[3]

    @pl.when(pl.program_id(1) == 0)
    def _():
        for i, w_ref in enumerate((w1_ref, w3_ref, w2_ref)):
            wb_ref[i] = w_ref[0].astype(BF16)

    slot = lax.broadcasted_iota(jnp.int32, (cap, L), 0)
    gates = [jnp.sum(jnp.where(slot == pos_ref[i, 0], gate_ref[i, 0], 0.0), axis=1, keepdims=True)
             for i in range(bb)]
    gate = jnp.concatenate(gates, axis=0)
    xs = xs_ref[:, 0].reshape(bb * cap, xs_ref.shape[3])
    hid = _silu(jnp.dot(xs, wb_ref[0], preferred_element_type=F32))
    hid = (hid * jnp.dot(xs, wb_ref[1], preferred_element_type=F32)).astype(BF16)
    y = (jnp.dot(hid, wb_ref[2], preferred_element_type=F32) * gate).astype(BF16)
    y_ref[:, 0] = y.reshape(bb, cap, y.shape[1])


def _ffn(pos, gate, xs, w1, w3, w2, bb):
    B, E, cap, D = xs.shape
    L = pos.shape[2]
    row = pl.BlockSpec((bb, 1, 1, L), lambda e, b: (b, e, 0, 0))
    slab = pl.BlockSpec((bb, 1, cap, D), lambda e, b: (b, e, 0, 0))
    wspec = pl.BlockSpec((1, D, D), lambda e, b: (e, 0, 0))
    return pl.pallas_call(
        functools.partial(_ffn_kernel, cap=cap),
        grid=(E, B // bb),
        in_specs=[row, row, slab, wspec, wspec, wspec],
        out_specs=slab,
        out_shape=jax.ShapeDtypeStruct((B, E, cap, D), BF16),
        scratch_shapes=[pltpu.VMEM((3, D, D), BF16)],
        compiler_params=_params(("arbitrary", "arbitrary")),
        name="ffn",
    )(pos.reshape(B, E, 1, L), gate.reshape(B, E, 1, L), xs, w1, w3, w2)


def _combine_kernel(bnd_ref, pos_ref, y_ref, x_ref, g_ref, *rest, cap, width, final_norm):
    nw_ref, out_ref = rest if final_norm else (None, rest[0])
    b = pl.program_id(0)
    L = pos_ref.shape[2]
    nt = L // TOK_TILE
    tiles_per_step = out_ref.shape[1] // TOK_TILE
    tn = (((0,), (0,)), ((), ()))

    def tile_body(i, carry):
        t = pl.program_id(1) * tiles_per_step + i
        toks = pl.ds(pl.multiple_of(t * TOK_TILE, TOK_TILE), TOK_TILE)
        rows = pl.ds(pl.multiple_of(i * TOK_TILE, TOK_TILE), TOK_TILE)
        ms, ys, wins = [], [], []
        for e in range(N_EXPERTS):
            base = (b * N_EXPERTS + e) * nt
            lo = bnd_ref[base + t]
            hi = jnp.where(t + 1 < nt, bnd_ref[base + jnp.minimum(t + 1, nt - 1)], cap)
            a0, n = _window(lo, hi, width, cap)
            a0 = pl.multiple_of(a0, SLOT_ALIGN)
            pos_row = pos_ref[0, e:e + 1, toks]
            ms.append(_one_hot_rows(pos_row, a0, width))
            ys.append(y_ref[0, e, pl.ds(a0, width), :])
            wins.append((a0, n, pos_row))
        out_ref[0, rows, :] = lax.dot_general(jnp.concatenate(ms, axis=0), jnp.concatenate(ys, axis=0), tn,
                                              preferred_element_type=F32)
        for e, (a0, n, pos_row) in enumerate(wins):
            def extra(w, c, e=e, a0=a0, pos_row=pos_row):
                first = a0 + w * width
                start = pl.multiple_of(jnp.minimum(first, cap - width), SLOT_ALIGN)
                m = _one_hot_rows(pos_row, start, width, first)
                out_ref[0, rows, :] += lax.dot_general(m, y_ref[0, e, pl.ds(start, width), :], tn,
                                                       preferred_element_type=F32)
                return c

            lax.fori_loop(1, n, extra, 0)
        x = x_ref[0, rows, :] + g_ref[0] * out_ref[0, rows, :]
        if final_norm:
            x = x * lax.rsqrt(jnp.mean(x * x, axis=-1, keepdims=True) + EPS) * nw_ref[...]
        out_ref[0, rows, :] = x
        return carry

    lax.fori_loop(0, tiles_per_step, tile_body, 0)


def _combine(bnd, pos, y, x, g, final_nw):
    B, E, cap, D = y.shape
    L = x.shape[1]
    width = min(cap, 64)
    tm = min(L, 512)
    final_norm = final_nw is not None
    tok = pl.BlockSpec((1, tm, D), lambda b, t, s: (b, t, 0))
    ins = [bnd.reshape(-1), pos, y, x, g]
    in_specs = [pl.BlockSpec((1, E, L), lambda b, t, s: (b, 0, 0)),
                pl.BlockSpec((1, E, cap, D), lambda b, t, s: (b, 0, 0, 0)),
                tok, pl.BlockSpec((1, 1, D), lambda b, t, s: (b, 0, 0))]
    if final_norm:
        ins.append(final_nw.reshape(1, D))
        in_specs.append(pl.BlockSpec((1, D), lambda b, t, s: (0, 0)))
    return pl.pallas_call(
        functools.partial(_combine_kernel, cap=cap, width=width, final_norm=final_norm),
        grid_spec=pltpu.PrefetchScalarGridSpec(
            num_scalar_prefetch=1, grid=(B, L // tm), in_specs=in_specs, out_specs=tok),
        out_shape=jax.ShapeDtypeStruct((B, L, D), F32),
        compiler_params=_params(("arbitrary", "arbitrary")),
        name="combine",
    )(*ins)


def _moe(lt, h2, x, g, w1, w3, w2, cap, bb, final_nw=None):
    pos, gate, bnd = _route(lt, cap)
    xs = _gather(bnd, pos, h2, cap)
    y = _ffn(pos, gate, xs, w1, w3, w2, bb)
    return _combine(bnd, pos, y, x, g, final_nw)


def _permute_w_in(w):
    cols = jnp.concatenate([w[:, 0:OFF_AF], w[:, OFF_R:OFF_SC], w[:, OFF_SC:D_IN], w[:, OFF_AF:OFF_R]],
                           axis=1)
    return jnp.pad(cols, ((0, 0), (0, P_TOTAL - D_IN))).astype(BF16)


def kernel(x, c, ctx, c_ctx, ada_w, ada_b, norm1_w, norm2_w, w_in, conf_dw_w, conf_dw_b, conf_ln_w,
           conf_ln_b, gla_wa_f, gla_ba_f, gla_wa_b, gla_ba_b, gla_gn_w, sc_w, sc_b, w_out, router_w,
           exp_w1, exp_w3, exp_w2, final_norm_w):
    B, L, D = x.shape
    Lc = ctx.shape[1]
    cap = EC_FACTOR * L // N_EXPERTS
    cap_c = EC_FACTOR * Lc // N_EXPERTS

    cond = jnp.concatenate([c, c_ctx[None, :], jnp.zeros((7, D), F32)], axis=0)
    mods = _ada(cond, ada_w, ada_b)

    xc = ctx
    zero_state = jnp.zeros((B, GLA_QK, GLA_DV), F32)
    for i in range(DEPTH):
        last = i == DEPTH - 1
        m_lat = [mods[i, :B, j * D:(j + 1) * D].reshape(B, 1, D) for j in range(6)]
        m_ctx = [jnp.broadcast_to(mods[i, B, j * D:(j + 1) * D].reshape(1, 1, D), (B, 1, D))
                 for j in range(6)]
        w_perm = _permute_w_in(w_in[i])
        wo = w_out[i].astype(BF16)
        rwt = router_w[i].T.astype(BF16)
        zero_wa = jnp.zeros((GLA_RANK, GLA_QK), F32)
        wa = jnp.block([[gla_wa_f[i], zero_wa], [zero_wa, gla_wa_b[i]]]).astype(BF16)
        ba = jnp.concatenate([gla_ba_f[i], gla_ba_b[i]]).reshape(1, 2 * GLA_QK)
        experts = (exp_w1[i], exp_w3[i], exp_w2[i])
        conv_w = (conf_dw_w[i], conf_dw_b[i], conf_ln_w[i], conf_ln_b[i], sc_w[i], sc_b[i])

        c_conf, c_qkv, c_r, c_sc, c_afab = _proj(xc, norm1_w[i], m_ctx[0], m_ctx[1], w_perm)
        c_og, c_sf, c_sb = _gla(c_qkv, c_afab, c_r, zero_state, zero_state, wa, ba, gla_gn_w[i])

        conf, qkv, r, scu, afab = _proj(x, norm1_w[i], m_lat[0], m_lat[1], w_perm)
        yc, ys = _convs(conf, scu, *conv_w, row_len=GRID_W)
        og, _, _ = _gla(qkv, afab, r, c_sf, c_sb, wa, ba, gla_gn_w[i])
        x, h2, lt = _outproj(yc, og, ys, x, wo, m_lat[2], norm2_w[i], m_lat[3], m_lat[4], rwt)
        x = _moe(lt, h2, x, m_lat[5], *experts, cap, 2, final_norm_w if last else None)

        if not last:
            c_yc, c_ys = _convs(c_conf, c_sc, *conv_w, row_len=Lc)
            xc, c_h2, c_lt = _outproj(c_yc, c_og, c_ys, xc, wo, m_ctx[2], norm2_w[i], m_ctx[3], m_ctx[4],
                                      rwt)
            xc = _moe(c_lt, c_h2, xc, m_ctx[5], *experts, cap_c, B)
    return x
```

```python
import functools

import jax
import jax.numpy as jnp
from jax import lax
from jax.experimental import pallas as pl
from jax.experimental.pallas import tpu as pltpu

F32 = jnp.float32
BF16 = jnp.bfloat16
HI = lax.Precision.HIGHEST

D_MODEL = 1024
DEPTH = 2
GRID_W = 64
D_CONF = 256
D_SC = 256
GLA_HEADS = 4
GLA_DK = 64
GLA_DV = 128
GLA_QK = GLA_HEADS * GLA_DK
GLA_V = GLA_HEADS * GLA_DV
GLA_RANK = 16
GLA_TAU = 16.0
CONF_K = 31
SC_K = 3
N_EXPERTS = 16
EC_FACTOR = 2
EPS = 1e-6

OFF_Q = 2 * D_CONF
OFF_K = OFF_Q + GLA_QK
OFF_V = OFF_K + GLA_QK
OFF_AF = OFF_V + GLA_V
OFF_AB = OFF_AF + GLA_RANK
OFF_R = OFF_AB + GLA_RANK
OFF_SC = OFF_R + GLA_V
D_IN = OFF_SC + 3 * D_SC

P_CONF = 0
P_QKV = 512
P_R = 1536
P_SC = 2048
P_AFAB = 2816
P_TOTAL = 2944
AFAB_W = 128

CHUNK = 64
GLA_BLOCK = 4
CONV_SUB = 64
CONV_TOKENS = 8
SUBLANES = 8
LANES = 128
CONV_PAD = 16
SC_PAD = 8
TOK_TILE = 128
COMB_TILE = 256
ROUTE_SAMPLES = 4
SLOT_ALIGN = 16
VMEM_LIMIT = 56 * 1024 * 1024


def _sigmoid(x):
    return 1.0 / (1.0 + jnp.exp(-x))


def _silu(x):
    return x * _sigmoid(x)


def _log_sigmoid(z):
    return jnp.minimum(z, 0.0) - jnp.log1p(jnp.exp(-jnp.abs(z)))


def _params(sem):
    return pltpu.CompilerParams(dimension_semantics=sem, vmem_limit_bytes=VMEM_LIMIT)


def _ada_kernel(s_ref, w_ref, b_ref, o_ref):
    s = _silu(s_ref[...])
    o_ref[0] = jnp.dot(s, w_ref[0], precision=HI, preferred_element_type=F32) + b_ref[0]


def _ada(s_in, ada_w, ada_b):
    rows = s_in.shape[0]
    tn = 1024
    return pl.pallas_call(
        _ada_kernel,
        grid=(DEPTH, 6 * D_MODEL // tn),
        in_specs=[pl.BlockSpec((rows, D_MODEL), lambda l, n: (0, 0)),
                  pl.BlockSpec((1, D_MODEL, tn), lambda l, n: (l, 0, n)),
                  pl.BlockSpec((1, 1, tn), lambda l, n: (l, 0, n))],
        out_specs=pl.BlockSpec((1, rows, tn), lambda l, n: (l, 0, n)),
        out_shape=jax.ShapeDtypeStruct((DEPTH, rows, 6 * D_MODEL), F32),
        compiler_params=_params(("arbitrary", "arbitrary")),
        name="ada",
    )(s_in, ada_w, ada_b.reshape(DEPTH, 1, 6 * D_MODEL))


def _rms_mod(x, nw, shift, scale):
    ms = jnp.mean(x * x, axis=-1, keepdims=True)
    return (x * lax.rsqrt(ms + EPS) * nw) * (1.0 + scale) + shift


def _proj_kernel(x_ref, nw_ref, sh_ref, sc_ref, w_ref, conf_ref, qkv_ref, r_ref, scu_ref, afab_ref):
    hb = _rms_mod(x_ref[0], nw_ref[...], sh_ref[0], sc_ref[0]).astype(BF16)

    def proj(lo, hi):
        return jnp.dot(hb, w_ref[:, lo:hi], preferred_element_type=F32)

    conf_ref[0] = proj(P_CONF, P_QKV).astype(BF16)
    qkv_ref[0] = proj(P_QKV, P_R).astype(BF16)
    r_ref[0] = proj(P_R, P_SC).astype(BF16)
    scu_ref[0] = proj(P_SC, P_AFAB).astype(BF16)
    afab_ref[0] = proj(P_AFAB, P_TOTAL)


def _proj(x, nw, shift, scale, w_perm):
    B, L, D = x.shape
    tm = min(L, 512)
    tok = lambda w: pl.BlockSpec((1, tm, w), lambda b, t: (b, t, 0))
    per_b = pl.BlockSpec((1, 1, D), lambda b, t: (b, 0, 0))
    widths = [(2 * D_CONF, BF16), (P_R - P_QKV, BF16), (GLA_V, BF16), (3 * D_SC, BF16), (AFAB_W, F32)]
    return pl.pallas_call(
        _proj_kernel,
        grid=(B, L // tm),
        in_specs=[tok(D), pl.BlockSpec((1, D), lambda b, t: (0, 0)), per_b, per_b,
                  pl.BlockSpec((D, P_TOTAL), lambda b, t: (0, 0))],
        out_specs=[tok(w) for w, _ in widths],
        out_shape=[jax.ShapeDtypeStruct((B, L, w), dt) for w, dt in widths],
        compiler_params=_params(("parallel", "parallel")),
        name="proj",
    )(x, nw.reshape(1, D), shift, scale, w_perm)


def _conv_kernel(conf_ref, scu_ref, cw_ref, cb_ref, lnw_ref, lnb_ref, sw_ref, sb_ref,
                 yc_ref, ys_ref, yt_ref, ot_ref, pz_ref, *, row_len):
    L = conf_ref.shape[1]
    n_rows = L // row_len
    C = D_CONF
    S = SUBLANES

    split_row = n_rows == 1
    assert split_row or n_rows % S == 0
    t_out = row_len // S if split_row else row_len
    span = t_out + 2 * CONV_PAD

    halves = [slice(h * LANES, (h + 1) * LANES) for h in range(C // LANES)]

    def group(g, carry):
        for h in range(len(halves)):
            if not split_row:
                yt_ref[h, 0:CONV_PAD * S, :] = jnp.zeros((CONV_PAD * S, LANES), F32)
                yt_ref[h, (CONV_PAD + t_out) * S:, :] = jnp.zeros((CONV_PAD * S, LANES), F32)
        for k in range(S):
            if split_row:
                tok0 = k * t_out - CONV_PAD
                lo, hi = max(tok0, 0), min(tok0 + span, L)
                src = conf_ref[0, lo:hi, :]
                for a, b in ((0, lo - tok0), (hi - tok0, span)):
                    for h in range(len(halves)):
                        if b > a:
                            yt_ref[h, pl.ds(k + S * a, b - a, stride=S), :] = jnp.zeros((b - a, LANES), F32)
                first = lo - tok0
            else:
                src = conf_ref[0, pl.ds(pl.multiple_of((g * S + k) * row_len, row_len), row_len), :]
                first = CONV_PAD
            u = src.astype(F32)
            z = u[:, :C] * _sigmoid(u[:, C:])
            for h, ls in enumerate(halves):
                yt_ref[h, pl.ds(k + S * first, z.shape[0], stride=S), :] = z[:, ls]

        def out_rows(t):
            return pl.ds(pl.multiple_of(t * S, S), S)

        for h, ls in enumerate(halves):
            def taps(i, c, h=h, ls=ls):
                t0 = i * CONV_TOKENS
                accs = [None] * CONV_TOKENS
                for m in range(CONV_TOKENS + CONF_K - 1):
                    v = yt_ref[h, out_rows(t0 + CONV_PAD - CONF_K // 2 + m), :]
                    for tt in range(max(0, m - CONF_K + 1), min(CONV_TOKENS, m + 1)):
                        term = cw_ref[m - tt:m - tt + 1, ls] * v
                        accs[tt] = term if accs[tt] is None else accs[tt] + term
                for tt in range(CONV_TOKENS):
                    ot_ref[h, out_rows(t0 + tt), :] = accs[tt] + cb_ref[:, ls]
                return c

            lax.fori_loop(0, t_out // CONV_TOKENS, taps, 0)

        for k in range(S):
            dst = k * t_out if split_row else pl.multiple_of((g * S + k) * row_len, row_len)
            o = jnp.concatenate([ot_ref[h, pl.ds(k, t_out, stride=S), :] for h in range(len(halves))], axis=1)
            mu = jnp.mean(o, axis=-1, keepdims=True)
            cen = o - mu
            var = jnp.mean(cen * cen, axis=-1, keepdims=True)
            y = cen * lax.rsqrt(var + EPS) * lnw_ref[...] + lnb_ref[...]
            yc_ref[0, pl.ds(dst, t_out), :] = _silu(y).astype(BF16)
        return carry

    lax.fori_loop(0, 1 if split_row else n_rows // S, group, 0)

    n_blk = L // CONV_SUB
    pz_ref[0:SC_PAD, :] = jnp.zeros((SC_PAD, D_SC), F32)
    pz_ref[SC_PAD + L:, :] = jnp.zeros((SC_PAD, D_SC), F32)

    def fill_p(i, carry):
        start = pl.multiple_of(i * CONV_SUB, CONV_SUB)
        u = scu_ref[0, pl.ds(start, CONV_SUB), :].astype(F32)
        pz_ref[pl.ds(SC_PAD + start, CONV_SUB), :] = u[:, D_SC:2 * D_SC] * u[:, 2 * D_SC:]
        return carry

    lax.fori_loop(0, n_blk, fill_p, 0)

    def sc_blk(i, carry):
        start = pl.multiple_of(i * CONV_SUB, CONV_SUB)
        win = pz_ref[pl.ds(start, CONV_SUB + 2 * SC_PAD), :]
        acc = sb_ref[...] + sw_ref[0:1, :] * win[SC_PAD - 1:SC_PAD - 1 + CONV_SUB]
        acc = acc + sw_ref[1:2, :] * win[SC_PAD:SC_PAD + CONV_SUB]
        acc = acc + sw_ref[2:3, :] * win[SC_PAD + 1:SC_PAD + 1 + CONV_SUB]
        bg = scu_ref[0, pl.ds(start, CONV_SUB), 0:D_SC].astype(F32)
        ys_ref[0, pl.ds(start, CONV_SUB), :] = (bg * acc).astype(BF16)
        return carry

    lax.fori_loop(0, n_blk, sc_blk, 0)


def _convs(conf, scu, cw, cb, lnw, lnb, sw, sb, row_len):
    B, L, _ = conf.shape
    t_out = row_len // SUBLANES if L == row_len else row_len
    full = lambda a: pl.BlockSpec(a.shape, lambda b: (0,) * a.ndim)
    cw_p = jnp.zeros((32, D_CONF), F32).at[:CONF_K].set(cw)
    sw_p = jnp.zeros((8, D_SC), F32).at[:SC_K].set(sw)
    small = [cw_p, cb.reshape(1, D_CONF), lnw.reshape(1, D_CONF), lnb.reshape(1, D_CONF),
             sw_p, sb.reshape(1, D_SC)]
    return pl.pallas_call(
        functools.partial(_conv_kernel, row_len=row_len),
        grid=(B,),
        in_specs=[pl.BlockSpec((1, L, 2 * D_CONF), lambda b: (b, 0, 0)),
                  pl.BlockSpec((1, L, 3 * D_SC), lambda b: (b, 0, 0))] + [full(a) for a in small],
        out_specs=[pl.BlockSpec((1, L, D_CONF), lambda b: (b, 0, 0)),
                   pl.BlockSpec((1, L, D_SC), lambda b: (b, 0, 0))],
        out_shape=[jax.ShapeDtypeStruct((B, L, D_CONF), BF16),
                   jax.ShapeDtypeStruct((B, L, D_SC), BF16)],
        scratch_shapes=[pltpu.VMEM((D_CONF // LANES, (t_out + 2 * CONV_PAD) * SUBLANES, LANES), F32),
                        pltpu.VMEM((D_CONF // LANES, t_out * SUBLANES, LANES), F32),
                        pltpu.VMEM((L + 2 * SC_PAD, D_SC), F32)],
        compiler_params=_params(("parallel",)),
        name="convs",
    )(conf, scu, *small)


def _gla_kernel(qkv_ref, afab_ref, r_ref, s0f_ref, s0b_ref, wab_ref, bab_ref, gnw_ref,
                og_ref, sff_ref, sfb_ref, p_ref, qif_ref, qib_ref, spf_ref, ub_ref, gb_ref):
    L = qkv_ref.shape[1]
    C = CHUNK
    G = min(GLA_BLOCK, L // C)
    R = G * C
    n_blk = L // R
    mid = C // 2
    scale = GLA_DK ** -0.5
    nt = (((1,), (1,)), ((), ()))
    ii = lax.broadcasted_iota(jnp.int32, (R, R), 0)
    jj = lax.broadcasted_iota(jnp.int32, (R, R), 1)
    same = (ii // C) == (jj // C)
    tri_f = jnp.where(same, jnp.where(ii >= jj, 1.0, 0.0), 0.0).astype(BF16)
    tri_b = jnp.where(same, jnp.where(jj >= ii, 1.0, 0.0), 0.0).astype(BF16)
    PK = 2 * GLA_DK
    PV = 2 * GLA_DV
    ci = lax.broadcasted_iota(jnp.int32, (C, 2 * C), 0)
    cj = lax.broadcasted_iota(jnp.int32, (C, 2 * C), 1) % C
    lower = ci >= cj
    upper = cj >= ci
    kr = lax.broadcasted_iota(jnp.int32, (2 * C, PK), 0) // C
    kc = lax.broadcasted_iota(jnp.int32, (2 * C, PK), 1) // GLA_DK
    key_diag = kr == kc
    vr = lax.broadcasted_iota(jnp.int32, (2 * C, PV), 0) // C
    vc = lax.broadcasted_iota(jnp.int32, (2 * C, PV), 1) // GLA_DV
    val_diag = vr == vc

    def col_bcast(row_vec):
        return jnp.broadcast_to(row_vec, (GLA_DV, GLA_QK)).T

    def chunk_cumsum(tri, la):
        hi = la.astype(BF16)
        lo = (la - hi.astype(F32)).astype(BF16)
        both = jnp.dot(tri, jnp.concatenate([hi, lo], axis=1), preferred_element_type=F32)
        return both[:, :GLA_QK] + both[:, GLA_QK:]

    def v_pair(rows, p):
        return qkv_ref[0, rows, 2 * GLA_QK + p * PV:2 * GLA_QK + (p + 1) * PV]

    def pass1(i, s_f):
        r0 = pl.multiple_of(i * R, R)
        ab = afab_ref[0, pl.ds(r0, R), 0:2 * GLA_RANK].astype(BF16)
        z = jnp.dot(ab, wab_ref[...], preferred_element_type=F32) + bab_ref[...]
        la = _log_sigmoid(z) * (1.0 / GLA_TAU)
        b_f = chunk_cumsum(tri_f, la[:, :GLA_QK])
        b_b = chunk_cumsum(tri_b, la[:, GLA_QK:])
        q_all = qkv_ref[0, pl.ds(r0, R), 0:GLA_QK].astype(F32) * scale
        k_all = qkv_ref[0, pl.ds(r0, R), GLA_QK:2 * GLA_QK].astype(F32)
        for g in range(G):
            c = i * G + g
            rows = pl.ds(pl.multiple_of(r0 + g * C, C), C)
            sl = slice(g * C, (g + 1) * C)
            q, k = q_all[sl], k_all[sl]
            scaled = []
            for b, tot_row in ((b_f[sl], C - 1), (b_b[sl], 0)):
                ref_row = b[mid:mid + 1]
                tot = b[tot_row:tot_row + 1]
                q_rel = q * jnp.exp(b - ref_row)
                k_rel = k * jnp.exp(ref_row - b)
                scaled.append((q_rel.astype(BF16), k_rel.astype(BF16),
                               (q_rel * jnp.exp(ref_row)).astype(BF16),
                               (k_rel * jnp.exp(tot - ref_row)).T.astype(BF16),
                               tot))
            (qf, kf, qif, kuf_t, g_f), (qb, kb, qib, kub_t, g_b) = scaled
            qif_ref[rows, :] = qif
            qib_ref[rows, :] = qib
            p_parts, uf, ub = [], [], []
            for p in range(GLA_HEADS // 2):
                ks = slice(p * PK, (p + 1) * PK)
                kbd_f = jnp.where(key_diag, jnp.concatenate([kf[:, ks]] * 2, axis=0), 0)
                kbd_b = jnp.where(key_diag, jnp.concatenate([kb[:, ks]] * 2, axis=0), 0)
                s_fwd = lax.dot_general(qf[:, ks], kbd_f, nt, preferred_element_type=F32)
                s_bwd = lax.dot_general(qb[:, ks], kbd_b, nt, preferred_element_type=F32)
                p_parts.append((jnp.where(lower, s_fwd, 0.0) + jnp.where(upper, s_bwd, 0.0)).astype(BF16))
                u = jnp.dot(jnp.concatenate([kuf_t[ks, :], kub_t[ks, :]], axis=0), v_pair(rows, p),
                            preferred_element_type=F32)
                for blk, dst in ((u[:PK], uf), (u[PK:], ub)):
                    dst += [blk[:GLA_DK, :GLA_DV], blk[GLA_DK:, GLA_DV:]]
            p_ref[rows, :] = jnp.concatenate(p_parts, axis=1)
            ub_ref[c] = jnp.concatenate(ub, axis=0)
            gb_ref[c] = jnp.broadcast_to(g_b, (8, GLA_QK))
            spf_ref[c] = s_f.astype(BF16)
            s_f = col_bcast(jnp.exp(g_f)) * s_f + jnp.concatenate(uf, axis=0)
        return s_f

    sff_ref[0] = lax.fori_loop(0, n_blk, pass1, s0f_ref[0])

    def pass2(t, s_b):
        i = n_blk - 1 - t
        for g in reversed(range(G)):
            c = i * G + g
            rows = pl.ds(pl.multiple_of(i * R + g * C, C), C)
            spf = spf_ref[c]
            snb = s_b.astype(BF16)
            zero_blk = jnp.zeros((GLA_DK, GLA_DV), BF16)

            def pair_state(s, p):
                top = jnp.concatenate([s[2 * p * GLA_DK:(2 * p + 1) * GLA_DK], zero_blk], axis=1)
                bot = jnp.concatenate([zero_blk, s[(2 * p + 1) * GLA_DK:(2 * p + 2) * GLA_DK]], axis=1)
                return jnp.concatenate([top, bot], axis=0)

            outs = []
            for p in range(GLA_HEADS // 2):
                ks = slice(p * PK, (p + 1) * PK)
                vbd = jnp.where(val_diag, jnp.concatenate([v_pair(rows, p)] * 2, axis=0), 0)
                lhs = jnp.concatenate([p_ref[rows, ks], qif_ref[rows, ks], qib_ref[rows, ks]], axis=1)
                rhs = jnp.concatenate([vbd, pair_state(spf, p), pair_state(snb, p)], axis=0)
                o_pair = jnp.dot(lhs, rhs, preferred_element_type=F32)
                for o in (o_pair[:, :GLA_DV], o_pair[:, GLA_DV:]):
                    ms = jnp.mean(o * o, axis=-1, keepdims=True)
                    outs.append(o * lax.rsqrt(ms + EPS))
            o_all = jnp.concatenate(outs, axis=-1) * gnw_ref[...]
            og_ref[0, rows, :] = (o_all * _silu(r_ref[0, rows, :].astype(F32))).astype(BF16)
            s_b = col_bcast(jnp.exp(gb_ref[c][0:1])) * s_b + ub_ref[c]
        return s_b

    sfb_ref[0] = lax.fori_loop(0, n_blk, pass2, s0b_ref[0])


def _gla(qkv, afab, r, s0f, s0b, wab, bab, gnw):
    B, L, _ = qkv.shape
    n = L // CHUNK
    tok = lambda w: pl.BlockSpec((1, L, w), lambda b: (b, 0, 0))
    st = pl.BlockSpec((1, GLA_QK, GLA_DV), lambda b: (b, 0, 0))
    full = lambda a: pl.BlockSpec(a.shape, lambda b: (0,) * a.ndim)
    small = [wab, bab, gnw.reshape(1, GLA_V)]
    return pl.pallas_call(
        _gla_kernel,
        grid=(B,),
        in_specs=[tok(P_R - P_QKV), tok(AFAB_W), tok(GLA_V), st, st] + [full(a) for a in small],
        out_specs=[tok(GLA_V), st, st],
        out_shape=[jax.ShapeDtypeStruct((B, L, GLA_V), BF16),
                   jax.ShapeDtypeStruct((B, GLA_QK, GLA_DV), F32),
                   jax.ShapeDtypeStruct((B, GLA_QK, GLA_DV), F32)],
        scratch_shapes=[pltpu.VMEM((L, GLA_QK), BF16),
                        pltpu.VMEM((L, GLA_QK), BF16), pltpu.VMEM((L, GLA_QK), BF16),
                        pltpu.VMEM((n, GLA_QK, GLA_DV), BF16),
                        pltpu.VMEM((n, GLA_QK, GLA_DV), F32),
                        pltpu.VMEM((n, 8, GLA_QK), F32)],
        compiler_params=_params(("parallel",)),
        name="gla",
    )(qkv, afab, r, s0f, s0b, *small)


def _outproj_kernel(yc_ref, og_ref, ys_ref, x_ref, wo_ref, g2_ref, nw_ref, sh_ref, sc_ref, rwt_ref,
                    xo_ref, h2_ref, lt_ref):
    y = jnp.dot(yc_ref[0], wo_ref[0:D_CONF, :], preferred_element_type=F32)
    y = y + jnp.dot(og_ref[0], wo_ref[D_CONF:D_CONF + GLA_V, :], preferred_element_type=F32)
    y = y + jnp.dot(ys_ref[0], wo_ref[D_CONF + GLA_V:, :], preferred_element_type=F32)
    x = x_ref[0] + g2_ref[0] * y
    xo_ref[0] = x
    hb = _rms_mod(x, nw_ref[...], sh_ref[0], sc_ref[0]).astype(BF16)
    h2_ref[0] = hb
    lt_ref[0] = lax.dot_general(rwt_ref[...], hb, (((1,), (1,)), ((), ())),
                                preferred_element_type=F32)


def _outproj(yc, og, ys, x, wo, g2, nw, shift, scale, rwt):
    B, L, D = x.shape
    tm = min(L, 512)
    tok = lambda w: pl.BlockSpec((1, tm, w), lambda b, t: (b, t, 0))
    per_b = pl.BlockSpec((1, 1, D), lambda b, t: (b, 0, 0))
    return pl.pallas_call(
        _outproj_kernel,
        grid=(B, L // tm),
        in_specs=[tok(D_CONF), tok(GLA_V), tok(D_SC), tok(D),
                  pl.BlockSpec((D, D), lambda b, t: (0, 0)), per_b,
                  pl.BlockSpec((1, D), lambda b, t: (0, 0)), per_b, per_b,
                  pl.BlockSpec((N_EXPERTS, D), lambda b, t: (0, 0))],
        out_specs=[tok(D), tok(D), pl.BlockSpec((1, N_EXPERTS, tm), lambda b, t: (b, 0, t))],
        out_shape=[jax.ShapeDtypeStruct((B, L, D), F32), jax.ShapeDtypeStruct((B, L, D), BF16),
                   jax.ShapeDtypeStruct((B, N_EXPERTS, L), F32)],
        compiler_params=_params(("parallel", "parallel")),
        name="outproj",
    )(yc, og, ys, x, wo, g2, nw.reshape(1, D), shift, scale, rwt)


def _lane_cumsum(x):
    n = x.shape[-1]
    lane = lax.broadcasted_iota(jnp.int32, x.shape, x.ndim - 1)
    s = 1
    while s < n:
        x = x + jnp.where(lane >= s, pltpu.roll(x, s, axis=x.ndim - 1), 0)
        s *= 2
    return x


def _route_kernel(lt_ref, pos_ref, gate_ref, bnd_ref, *, cap):
    bs, E, L = lt_ref.shape
    lt = lt_ref[...]
    e = jnp.exp(lt - jnp.max(lt, axis=1, keepdims=True))
    aff = (e / jnp.sum(e, axis=1, keepdims=True)).reshape(bs * E, L)

    def search(i, t):
        cand = t | lax.shift_left(jnp.int32(1), 30 - i)
        cnt = jnp.sum((aff >= pltpu.bitcast(cand, F32)).astype(jnp.int32), axis=1, keepdims=True)
        return jnp.where(cnt >= cap, cand, t)

    thr_bits = lax.fori_loop(0, 31, search, jnp.zeros((bs * E, 1), jnp.int32))
    thr = pltpu.bitcast(thr_bits, F32)
    gt = aff > thr
    eq = aff == thr
    need = cap - jnp.sum(gt.astype(jnp.int32), axis=1, keepdims=True)
    eq_i = eq.astype(jnp.int32)
    eq_rank = _lane_cumsum(eq_i) - eq_i
    sel = gt | (eq & (eq_rank < need))
    sel_i = sel.astype(jnp.int32)
    slot = _lane_cumsum(sel_i) - sel_i
    pos_ref[...] = jnp.where(sel, slot, -1).reshape(bs, E, L)
    gate_ref[...] = aff.reshape(bs, E, L)
    bnd = jnp.concatenate([slot[:, k * TOK_TILE:k * TOK_TILE + 1] for k in range(L // TOK_TILE)], axis=1)
    bnd_ref[...] = bnd.reshape(bs, E, L // TOK_TILE)


def _route(lt, cap):
    B, E, L = lt.shape
    bs = min(B, ROUTE_SAMPLES)
    spec = pl.BlockSpec((bs, E, L), lambda b: (b, 0, 0))
    nt = L // TOK_TILE
    return pl.pallas_call(
        functools.partial(_route_kernel, cap=cap),
        grid=(B // bs,),
        in_specs=[spec], out_specs=[spec, spec, pl.BlockSpec((bs, E, nt), lambda b: (b, 0, 0))],
        out_shape=[jax.ShapeDtypeStruct((B, E, L), jnp.int32), jax.ShapeDtypeStruct((B, E, L), F32),
                   jax.ShapeDtypeStruct((B, E, nt), jnp.int32)],
        compiler_params=_params(("parallel",)),
        name="route",
    )(lt)


def _window(lo, hi, width, cap):
    a0 = jnp.minimum((lo // SLOT_ALIGN) * SLOT_ALIGN, cap - width)
    n = jnp.where(hi > lo, (hi - a0 + width - 1) // width, 0)
    return a0, n


def _one_hot_rows(pos_row, start, width, first_row=None):
    slot = lax.broadcasted_iota(jnp.int32, (width, pos_row.shape[1]), 0) + start
    on = 1.0 if first_row is None else jnp.where(slot >= first_row, 1.0, 0.0)
    return jnp.where(slot == pos_row, on, 0.0).astype(BF16)


def _gather_kernel(bnd_ref, pos_ref, h_ref, xs_ref, *, cap, tile, width):
    b = pl.program_id(0)
    L = h_ref.shape[1]
    nt = L // TOK_TILE
    step = tile // TOK_TILE
    n_tiles = L // tile
    xs_ref[...] = jnp.zeros_like(xs_ref)

    def tile_body(k, carry):
        toks = pl.ds(pl.multiple_of(k * tile, tile), tile)
        h_tile = h_ref[0, toks, :]
        ms, wins = [], []
        for e in range(N_EXPERTS):
            base = (b * N_EXPERTS + e) * nt
            lo = bnd_ref[base + k * step]
            hi = jnp.where(k + 1 < n_tiles, bnd_ref[base + jnp.minimum(k + 1, n_tiles - 1) * step], cap)
            a0, n = _window(lo, hi, width, cap)
            a0 = pl.multiple_of(a0, SLOT_ALIGN)
            pos_row = pos_ref[0, e:e + 1, toks]
            ms.append(_one_hot_rows(pos_row, a0, width))
            wins.append((a0, n, pos_row))
        part = jnp.dot(jnp.concatenate(ms, axis=0), h_tile, preferred_element_type=F32)
        for e, (a0, n, pos_row) in enumerate(wins):
            xs_ref[0, e, pl.ds(a0, width), :] += part[e * width:(e + 1) * width].astype(BF16)

            def extra(w, c, e=e, a0=a0, pos_row=pos_row):
                first = a0 + w * width
                start = pl.multiple_of(jnp.minimum(first, cap - width), SLOT_ALIGN)
                m = _one_hot_rows(pos_row, start, width, first)
                xs_ref[0, e, pl.ds(start, width), :] += jnp.dot(
                    m, h_ref[0, toks, :], preferred_element_type=F32).astype(BF16)
                return c

            lax.fori_loop(1, n, extra, 0)
        return carry

    lax.fori_loop(0, n_tiles, tile_body, 0)


def _gather(bnd, pos, h2, cap):
    B, L, D = h2.shape
    E = N_EXPERTS
    tile = min(L, 256)
    width = min(cap, 64)
    return pl.pallas_call(
        functools.partial(_gather_kernel, cap=cap, tile=tile, width=width),
        grid_spec=pltpu.PrefetchScalarGridSpec(
            num_scalar_prefetch=1, grid=(B,),
            in_specs=[pl.BlockSpec((1, E, L), lambda b, s: (b, 0, 0)),
                      pl.BlockSpec((1, L, D), lambda b, s: (b, 0, 0))],
            out_specs=pl.BlockSpec((1, E, cap, D), lambda b, s: (b, 0, 0, 0))),
        out_shape=jax.ShapeDtypeStruct((B, E, cap, D), BF16),
        compiler_params=_params(("arbitrary",)),
        name="gather",
    )(bnd.reshape(-1), pos, h2)


def _ffn_kernel(pos_ref, gate_ref, xs_ref, w1_ref, w3_ref, w2_ref, y_ref, wb_ref, *, cap):
    bb = xs_ref.shape[0]
    L = pos_ref.shape[3]

    @pl.when(pl.program_id(1) == 0)
    def _():
        for i, w_ref in enumerate((w1_ref, w3_ref, w2_ref)):
            wb_ref[i] = w_ref[0, 0].astype(BF16)

    slot = lax.broadcasted_iota(jnp.int32, (cap, L), 0)
    gates = [jnp.sum(jnp.where(slot == pos_ref[i, 0], gate_ref[i, 0], 0.0), axis=1, keepdims=True)
             for i in range(bb)]
    gate = jnp.concatenate(gates, axis=0)
    xs = xs_ref[:, 0].reshape(bb * cap, xs_ref.shape[3])
    hid = _silu(jnp.dot(xs, wb_ref[0], preferred_element_type=F32))
    hid = (hid * jnp.dot(xs, wb_ref[1], preferred_element_type=F32)).astype(BF16)
    y = (jnp.dot(hid, wb_ref[2], preferred_element_type=F32) * gate).astype(BF16)
    y_ref[:, 0] = y.reshape(bb, cap, y.shape[1])


def _ffn(pos, gate, xs, w1, w3, w2, layer, bb):
    B, E, cap, D = xs.shape
    L = pos.shape[2]
    row = pl.BlockSpec((bb, 1, 1, L), lambda e, b: (b, e, 0, 0))
    slab = pl.BlockSpec((bb, 1, cap, D), lambda e, b: (b, e, 0, 0))
    wspec = pl.BlockSpec((1, 1, D, D), lambda e, b: (layer, e, 0, 0))
    return pl.pallas_call(
        functools.partial(_ffn_kernel, cap=cap),
        grid=(E, B // bb),
        in_specs=[row, row, slab, wspec, wspec, wspec],
        out_specs=slab,
        out_shape=jax.ShapeDtypeStruct((B, E, cap, D), BF16),
        scratch_shapes=[pltpu.VMEM((3, D, D), BF16)],
        compiler_params=_params(("arbitrary", "arbitrary")),
        name="ffn",
    )(pos.reshape(B, E, 1, L), gate.reshape(B, E, 1, L), xs, w1, w3, w2)


def _combine_kernel(bnd_ref, pos_ref, y_ref, x_ref, g_ref, *rest, cap, width, final_norm):
    nw_ref, out_ref = rest if final_norm else (None, rest[0])
    b = pl.program_id(0)
    L = pos_ref.shape[2]
    nt = L // TOK_TILE
    tile = min(COMB_TILE, out_ref.shape[1])
    step = tile // TOK_TILE
    n_tiles = L // tile
    tiles_per_step = out_ref.shape[1] // tile
    tn = (((0,), (0,)), ((), ()))

    def tile_body(i, carry):
        t = pl.program_id(1) * tiles_per_step + i
        toks = pl.ds(pl.multiple_of(t * tile, tile), tile)
        rows = pl.ds(pl.multiple_of(i * tile, tile), tile)
        ms, ys, wins = [], [], []
        for e in range(N_EXPERTS):
            base = (b * N_EXPERTS + e) * nt
            lo = bnd_ref[base + t * step]
            hi = jnp.where(t + 1 < n_tiles, bnd_ref[base + jnp.minimum(t + 1, n_tiles - 1) * step], cap)
            a0, n = _window(lo, hi, width, cap)
            a0 = pl.multiple_of(a0, SLOT_ALIGN)
            pos_row = pos_ref[0, e:e + 1, toks]
            ms.append(_one_hot_rows(pos_row, a0, width))
            ys.append(y_ref[0, e, pl.ds(a0, width), :])
            wins.append((a0, n, pos_row))
        out_ref[0, rows, :] = lax.dot_general(jnp.concatenate(ms, axis=0), jnp.concatenate(ys, axis=0), tn,
                                              preferred_element_type=F32)
        for e, (a0, n, pos_row) in enumerate(wins):
            def extra(w, c, e=e, a0=a0, pos_row=pos_row):
                first = a0 + w * width
                start = pl.multiple_of(jnp.minimum(first, cap - width), SLOT_ALIGN)
                m = _one_hot_rows(pos_row, start, width, first)
                out_ref[0, rows, :] += lax.dot_general(m, y_ref[0, e, pl.ds(start, width), :], tn,
                                                       preferred_element_type=F32)
                return c

            lax.fori_loop(1, n, extra, 0)
        x = x_ref[0, rows, :] + g_ref[0] * out_ref[0, rows, :]
        if final_norm:
            x = x * lax.rsqrt(jnp.mean(x * x, axis=-1, keepdims=True) + EPS) * nw_ref[...]
        out_ref[0, rows, :] = x
        return carry

    lax.fori_loop(0, tiles_per_step, tile_body, 0)


def _combine(bnd, pos, y, x, g, final_nw):
    B, E, cap, D = y.shape
    L = x.shape[1]
    width = min(cap, 64)
    tm = min(L, 512)
    final_norm = final_nw is not None
    tok = pl.BlockSpec((1, tm, D), lambda b, t, s: (b, t, 0))
    ins = [bnd.reshape(-1), pos, y, x, g]
    in_specs = [pl.BlockSpec((1, E, L), lambda b, t, s: (b, 0, 0)),
                pl.BlockSpec((1, E, cap, D), lambda b, t, s: (b, 0, 0, 0)),
                tok, pl.BlockSpec((1, 1, D), lambda b, t, s: (b, 0, 0))]
    if final_norm:
        ins.append(final_nw.reshape(1, D))
        in_specs.append(pl.BlockSpec((1, D), lambda b, t, s: (0, 0)))
    return pl.pallas_call(
        functools.partial(_combine_kernel, cap=cap, width=width, final_norm=final_norm),
        grid_spec=pltpu.PrefetchScalarGridSpec(
            num_scalar_prefetch=1, grid=(B, L // tm), in_specs=in_specs, out_specs=tok),
        out_shape=jax.ShapeDtypeStruct((B, L, D), F32),
        compiler_params=_params(("arbitrary", "arbitrary")),
        name="combine",
    )(*ins)


def _moe(lt, h2, x, g, w1, w3, w2, layer, cap, bb, final_nw=None):
    pos, gate, bnd = _route(lt, cap)
    xs = _gather(bnd, pos, h2, cap)
    y = _ffn(pos, gate, xs, w1, w3, w2, layer, bb)
    return _combine(bnd, pos, y, x, g, final_nw)


def _permute_w_in(w):
    cols = jnp.concatenate([w[:, 0:OFF_AF], w[:, OFF_R:OFF_SC], w[:, OFF_SC:D_IN], w[:, OFF_AF:OFF_R]],
                           axis=1)
    return jnp.pad(cols, ((0, 0), (0, P_TOTAL - D_IN))).astype(BF16)


def kernel(x, c, ctx, c_ctx, ada_w, ada_b, norm1_w, norm2_w, w_in, conf_dw_w, conf_dw_b, conf_ln_w,
           conf_ln_b, gla_wa_f, gla_ba_f, gla_wa_b, gla_ba_b, gla_gn_w, sc_w, sc_b, w_out, router_w,
           exp_w1, exp_w3, exp_w2, final_norm_w):
    B, L, D = x.shape
    Lc = ctx.shape[1]
    cap = EC_FACTOR * L // N_EXPERTS
    cap_c = EC_FACTOR * Lc // N_EXPERTS

    cond = jnp.concatenate([c, c_ctx[None, :], jnp.zeros((7, D), F32)], axis=0)
    mods = _ada(cond, ada_w, ada_b)

    xc = ctx
    zero_state = jnp.zeros((B, GLA_QK, GLA_DV), F32)
    for i in range(DEPTH):
        last = i == DEPTH - 1
        m_lat = [mods[i, :B, j * D:(j + 1) * D].reshape(B, 1, D) for j in range(6)]
        m_ctx = [jnp.broadcast_to(mods[i, B, j * D:(j + 1) * D].reshape(1, 1, D), (B, 1, D))
                 for j in range(6)]
        w_perm = _permute_w_in(w_in[i])
        wo = w_out[i].astype(BF16)
        rwt = router_w[i].T.astype(BF16)
        zero_wa = jnp.zeros((GLA_RANK, GLA_QK), F32)
        wa = jnp.block([[gla_wa_f[i], zero_wa], [zero_wa, gla_wa_b[i]]]).astype(BF16)
        ba = jnp.concatenate([gla_ba_f[i], gla_ba_b[i]]).reshape(1, 2 * GLA_QK)
        experts = (exp_w1, exp_w3, exp_w2, i)
        conv_w = (conf_dw_w[i], conf_dw_b[i], conf_ln_w[i], conf_ln_b[i], sc_w[i], sc_b[i])

        c_conf, c_qkv, c_r, c_sc, c_afab = _proj(xc, norm1_w[i], m_ctx[0], m_ctx[1], w_perm)
        c_og, c_sf, c_sb = _gla(c_qkv, c_afab, c_r, zero_state, zero_state, wa, ba, gla_gn_w[i])

        conf, qkv, r, scu, afab = _proj(x, norm1_w[i], m_lat[0], m_lat[1], w_perm)
        yc, ys = _convs(conf, scu, *conv_w, row_len=GRID_W)
        og, _, _ = _gla(qkv, afab, r, c_sf, c_sb, wa, ba, gla_gn_w[i])
        x, h2, lt = _outproj(yc, og, ys, x, wo, m_lat[2], norm2_w[i], m_lat[3], m_lat[4], rwt)
        x = _moe(lt, h2, x, m_lat[5], *experts, cap, 2, final_norm_w if last else None)

        if not last:
            c_yc, c_ys = _convs(c_conf, c_sc, *conv_w, row_len=Lc)
            xc, c_h2, c_lt = _outproj(c_yc, c_og, c_ys, xc, wo, m_ctx[2], norm2_w[i], m_ctx[3], m_ctx[4],
                                      rwt)
            xc = _moe(c_lt, c_h2, xc, m_ctx[5], *experts, cap_c, B)
    return x
```

```python
import functools

import jax
import jax.numpy as jnp
from jax import lax
from jax.experimental import pallas as pl
from jax.experimental.pallas import tpu as pltpu

F32 = jnp.float32
BF16 = jnp.bfloat16
HI = lax.Precision.HIGHEST

D_MODEL = 1024
DEPTH = 2
GRID_W = 64
D_CONF = 256
D_SC = 256
GLA_HEADS = 4
GLA_DK = 64
GLA_DV = 128
GLA_QK = GLA_HEADS * GLA_DK
GLA_V = GLA_HEADS * GLA_DV
GLA_RANK = 16
GLA_TAU = 16.0
CONF_K = 31
SC_K = 3
N_EXPERTS = 16
EC_FACTOR = 2
EPS = 1e-6

OFF_Q = 2 * D_CONF
OFF_K = OFF_Q + GLA_QK
OFF_V = OFF_K + GLA_QK
OFF_AF = OFF_V + GLA_V
OFF_AB = OFF_AF + GLA_RANK
OFF_R = OFF_AB + GLA_RANK
OFF_SC = OFF_R + GLA_V
D_IN = OFF_SC + 3 * D_SC

P_CONF = 0
P_QKV = 512
P_R = 1536
P_SC = 2048
P_AFAB = 2816
P_TOTAL = 2944
AFAB_W = 128

CHUNK = 64
GLA_BLOCK = 8
CUMSUM_ROWS = 256
CONV_SUB = 64
CONV_TOKENS = 8
SUBLANES = 8
LANES = 128
CONV_PAD = 16
SC_PAD = 8
TOK_TILE = 128
COMB_TILE = 256
ROUTE_SAMPLES = 4
FFN_SAMPLES = 4
SLOT_ALIGN = 16
VMEM_LIMIT = 56 * 1024 * 1024


def _sigmoid(x):
    return 1.0 / (1.0 + jnp.exp(-x))


def _silu(x):
    return x * _sigmoid(x)


def _log_sigmoid(z):
    return jnp.minimum(z, 0.0) - jnp.log1p(jnp.exp(-jnp.abs(z)))


def _params(sem):
    return pltpu.CompilerParams(dimension_semantics=sem, vmem_limit_bytes=VMEM_LIMIT)


def _ada_kernel(s_ref, w_ref, b_ref, o_ref):
    s = _silu(s_ref[...])
    o_ref[0] = jnp.dot(s, w_ref[0], precision=HI, preferred_element_type=F32) + b_ref[0]


def _ada(s_in, ada_w, ada_b):
    rows = s_in.shape[0]
    tn = 1024
    return pl.pallas_call(
        _ada_kernel,
        grid=(DEPTH, 6 * D_MODEL // tn),
        in_specs=[pl.BlockSpec((rows, D_MODEL), lambda l, n: (0, 0)),
                  pl.BlockSpec((1, D_MODEL, tn), lambda l, n: (l, 0, n)),
                  pl.BlockSpec((1, 1, tn), lambda l, n: (l, 0, n))],
        out_specs=pl.BlockSpec((1, rows, tn), lambda l, n: (l, 0, n)),
        out_shape=jax.ShapeDtypeStruct((DEPTH, rows, 6 * D_MODEL), F32),
        compiler_params=_params(("arbitrary", "arbitrary")),
        name="ada",
    )(s_in, ada_w, ada_b.reshape(DEPTH, 1, 6 * D_MODEL))


def _rms_mod(x, nw, shift, scale):
    ms = jnp.mean(x * x, axis=-1, keepdims=True)
    return (x * lax.rsqrt(ms + EPS) * nw) * (1.0 + scale) + shift


def _proj_kernel(x_ref, nw_ref, sh_ref, sc_ref, w_ref, conf_ref, qkv_ref, r_ref, scu_ref, afab_ref):
    hb = _rms_mod(x_ref[0], nw_ref[...], sh_ref[0], sc_ref[0]).astype(BF16)

    def proj(lo, hi):
        return jnp.dot(hb, w_ref[:, lo:hi], preferred_element_type=F32)

    conf_ref[0] = proj(P_CONF, P_QKV).astype(BF16)
    qkv_ref[0] = proj(P_QKV, P_R).astype(BF16)
    r_ref[0] = proj(P_R, P_SC).astype(BF16)
    scu_ref[0] = proj(P_SC, P_AFAB).astype(BF16)
    afab_ref[0] = proj(P_AFAB, P_TOTAL)


def _proj(x, nw, shift, scale, w_perm):
    B, L, D = x.shape
    tm = min(L, 512)
    tok = lambda w: pl.BlockSpec((1, tm, w), lambda b, t: (b, t, 0))
    per_b = pl.BlockSpec((1, 1, D), lambda b, t: (b, 0, 0))
    widths = [(2 * D_CONF, BF16), (P_R - P_QKV, BF16), (GLA_V, BF16), (3 * D_SC, BF16), (AFAB_W, F32)]
    return pl.pallas_call(
        _proj_kernel,
        grid=(B, L // tm),
        in_specs=[tok(D), pl.BlockSpec((1, D), lambda b, t: (0, 0)), per_b, per_b,
                  pl.BlockSpec((D, P_TOTAL), lambda b, t: (0, 0))],
        out_specs=[tok(w) for w, _ in widths],
        out_shape=[jax.ShapeDtypeStruct((B, L, w), dt) for w, dt in widths],
        compiler_params=_params(("parallel", "parallel")),
        name="proj",
    )(x, nw.reshape(1, D), shift, scale, w_perm)


def _conv_kernel(conf_ref, scu_ref, cw_ref, cb_ref, lnw_ref, lnb_ref, sw_ref, sb_ref,
                 yc_ref, ys_ref, yt_ref, ot_ref, pz_ref, *, row_len):
    L = conf_ref.shape[1]
    n_rows = L // row_len
    C = D_CONF
    S = SUBLANES

    split_row = n_rows == 1
    assert split_row or n_rows % S == 0
    t_out = row_len // S if split_row else row_len
    span = t_out + 2 * CONV_PAD

    halves = [slice(h * LANES, (h + 1) * LANES) for h in range(C // LANES)]

    def group(g, carry):
        for h in range(len(halves)):
            if not split_row:
                yt_ref[h, 0:CONV_PAD * S, :] = jnp.zeros((CONV_PAD * S, LANES), F32)
                yt_ref[h, (CONV_PAD + t_out) * S:, :] = jnp.zeros((CONV_PAD * S, LANES), F32)
        for k in range(S):
            if split_row:
                tok0 = k * t_out - CONV_PAD
                lo, hi = max(tok0, 0), min(tok0 + span, L)
                src = conf_ref[0, lo:hi, :]
                for a, b in ((0, lo - tok0), (hi - tok0, span)):
                    for h in range(len(halves)):
                        if b > a:
                            yt_ref[h, pl.ds(k + S * a, b - a, stride=S), :] = jnp.zeros((b - a, LANES), F32)
                first = lo - tok0
            else:
                src = conf_ref[0, pl.ds(pl.multiple_of((g * S + k) * row_len, row_len), row_len), :]
                first = CONV_PAD
            u = src.astype(F32)
            z = u[:, :C] * _sigmoid(u[:, C:])
            for h, ls in enumerate(halves):
                yt_ref[h, pl.ds(k + S * first, z.shape[0], stride=S), :] = z[:, ls]

        def out_rows(t):
            return pl.ds(pl.multiple_of(t * S, S), S)

        for h, ls in enumerate(halves):
            def taps(i, c, h=h, ls=ls):
                t0 = i * CONV_TOKENS
                accs = [None] * CONV_TOKENS
                for m in range(CONV_TOKENS + CONF_K - 1):
                    v = yt_ref[h, out_rows(t0 + CONV_PAD - CONF_K // 2 + m), :]
                    for tt in range(max(0, m - CONF_K + 1), min(CONV_TOKENS, m + 1)):
                        term = cw_ref[m - tt:m - tt + 1, ls] * v
                        accs[tt] = term if accs[tt] is None else accs[tt] + term
                for tt in range(CONV_TOKENS):
                    ot_ref[h, out_rows(t0 + tt), :] = accs[tt] + cb_ref[:, ls]
                return c

            lax.fori_loop(0, t_out // CONV_TOKENS, taps, 0)

        for k in range(S):
            dst = k * t_out if split_row else pl.multiple_of((g * S + k) * row_len, row_len)
            o = jnp.concatenate([ot_ref[h, pl.ds(k, t_out, stride=S), :] for h in range(len(halves))], axis=1)
            mu = jnp.mean(o, axis=-1, keepdims=True)
            cen = o - mu
            var = jnp.mean(cen * cen, axis=-1, keepdims=True)
            y = cen * lax.rsqrt(var + EPS) * lnw_ref[...] + lnb_ref[...]
            yc_ref[0, pl.ds(dst, t_out), :] = _silu(y).astype(BF16)
        return carry

    lax.fori_loop(0, 1 if split_row else n_rows // S, group, 0)

    n_blk = L // CONV_SUB
    pz_ref[0:SC_PAD, :] = jnp.zeros((SC_PAD, D_SC), F32)
    pz_ref[SC_PAD + L:, :] = jnp.zeros((SC_PAD, D_SC), F32)

    def fill_p(i, carry):
        start = pl.multiple_of(i * CONV_SUB, CONV_SUB)
        u = scu_ref[0, pl.ds(start, CONV_SUB), :].astype(F32)
        pz_ref[pl.ds(SC_PAD + start, CONV_SUB), :] = u[:, D_SC:2 * D_SC] * u[:, 2 * D_SC:]
        return carry

    lax.fori_loop(0, n_blk, fill_p, 0)

    def sc_blk(i, carry):
        start = pl.multiple_of(i * CONV_SUB, CONV_SUB)
        win = pz_ref[pl.ds(start, CONV_SUB + 2 * SC_PAD), :]
        acc = sb_ref[...] + sw_ref[0:1, :] * win[SC_PAD - 1:SC_PAD - 1 + CONV_SUB]
        acc = acc + sw_ref[1:2, :] * win[SC_PAD:SC_PAD + CONV_SUB]
        acc = acc + sw_ref[2:3, :] * win[SC_PAD + 1:SC_PAD + 1 + CONV_SUB]
        bg = scu_ref[0, pl.ds(start, CONV_SUB), 0:D_SC].astype(F32)
        ys_ref[0, pl.ds(start, CONV_SUB), :] = (bg * acc).astype(BF16)
        return carry

    lax.fori_loop(0, n_blk, sc_blk, 0)


def _convs(conf, scu, cw, cb, lnw, lnb, sw, sb, row_len):
    B, L, _ = conf.shape
    t_out = row_len // SUBLANES if L == row_len else row_len
    full = lambda a: pl.BlockSpec(a.shape, lambda b: (0,) * a.ndim)
    cw_p = jnp.zeros((32, D_CONF), F32).at[:CONF_K].set(cw)
    sw_p = jnp.zeros((8, D_SC), F32).at[:SC_K].set(sw)
    small = [cw_p, cb.reshape(1, D_CONF), lnw.reshape(1, D_CONF), lnb.reshape(1, D_CONF),
             sw_p, sb.reshape(1, D_SC)]
    return pl.pallas_call(
        functools.partial(_conv_kernel, row_len=row_len),
        grid=(B,),
        in_specs=[pl.BlockSpec((1, L, 2 * D_CONF), lambda b: (b, 0, 0)),
                  pl.BlockSpec((1, L, 3 * D_SC), lambda b: (b, 0, 0))] + [full(a) for a in small],
        out_specs=[pl.BlockSpec((1, L, D_CONF), lambda b: (b, 0, 0)),
                   pl.BlockSpec((1, L, D_SC), lambda b: (b, 0, 0))],
        out_shape=[jax.ShapeDtypeStruct((B, L, D_CONF), BF16),
                   jax.ShapeDtypeStruct((B, L, D_SC), BF16)],
        scratch_shapes=[pltpu.VMEM((D_CONF // LANES, (t_out + 2 * CONV_PAD) * SUBLANES, LANES), F32),
                        pltpu.VMEM((D_CONF // LANES, t_out * SUBLANES, LANES), F32),
                        pltpu.VMEM((L + 2 * SC_PAD, D_SC), F32)],
        compiler_params=_params(("parallel",)),
        name="convs",
    )(conf, scu, *small)


def _gla_kernel(qkv_ref, afab_ref, r_ref, s0f_ref, s0b_ref, wab_ref, bab_ref, gnw_ref,
                og_ref, sff_ref, sfb_ref, p_ref, qif_ref, qib_ref, spf_ref, ub_ref, gb_ref):
    L = qkv_ref.shape[1]
    C = CHUNK
    G = min(GLA_BLOCK, L // C)
    R = G * C
    n_blk = L // R
    mid = C // 2
    scale = GLA_DK ** -0.5
    nt = (((1,), (1,)), ((), ()))
    T = min(R, CUMSUM_ROWS)
    ii = lax.broadcasted_iota(jnp.int32, (T, T), 0)
    jj = lax.broadcasted_iota(jnp.int32, (T, T), 1)
    same = (ii // C) == (jj // C)
    tri_f = jnp.where(same, jnp.where(ii >= jj, 1.0, 0.0), 0.0).astype(BF16)
    tri_b = jnp.where(same, jnp.where(jj >= ii, 1.0, 0.0), 0.0).astype(BF16)
    PK = 2 * GLA_DK
    PV = 2 * GLA_DV
    ci = lax.broadcasted_iota(jnp.int32, (C, 2 * C), 0)
    cj = lax.broadcasted_iota(jnp.int32, (C, 2 * C), 1) % C
    lower = ci >= cj
    upper = cj >= ci
    kr = lax.broadcasted_iota(jnp.int32, (2 * C, PK), 0) // C
    kc = lax.broadcasted_iota(jnp.int32, (2 * C, PK), 1) // GLA_DK
    key_diag = kr == kc
    vr = lax.broadcasted_iota(jnp.int32, (2 * C, PV), 0) // C
    vc = lax.broadcasted_iota(jnp.int32, (2 * C, PV), 1) // GLA_DV
    val_diag = vr == vc

    def col_bcast(row_vec):
        return jnp.broadcast_to(row_vec, (GLA_DV, GLA_QK)).T

    def chunk_cumsum(tri, la):
        hi = la.astype(BF16)
        lo = (la - hi.astype(F32)).astype(BF16)
        parts = jnp.concatenate([hi, lo], axis=1)
        both = jnp.concatenate([jnp.dot(tri, parts[s * T:(s + 1) * T], preferred_element_type=F32)
                                for s in range(R // T)], axis=0)
        return both[:, :GLA_QK] + both[:, GLA_QK:]

    def v_pair(rows, p):
        return qkv_ref[0, rows, 2 * GLA_QK + p * PV:2 * GLA_QK + (p + 1) * PV]

    def pass1(i, s_f):
        r0 = pl.multiple_of(i * R, R)
        ab = afab_ref[0, pl.ds(r0, R), 0:2 * GLA_RANK].astype(BF16)
        z = jnp.dot(ab, wab_ref[...], preferred_element_type=F32) + bab_ref[...]
        la = _log_sigmoid(z) * (1.0 / GLA_TAU)
        b_f = chunk_cumsum(tri_f, la[:, :GLA_QK])
        b_b = chunk_cumsum(tri_b, la[:, GLA_QK:])
        q_all = qkv_ref[0, pl.ds(r0, R), 0:GLA_QK].astype(F32) * scale
        k_all = qkv_ref[0, pl.ds(r0, R), GLA_QK:2 * GLA_QK].astype(F32)
        for g in range(G):
            c = i * G + g
            rows = pl.ds(pl.multiple_of(r0 + g * C, C), C)
            sl = slice(g * C, (g + 1) * C)
            q, k = q_all[sl], k_all[sl]
            scaled = []
            for b, tot_row in ((b_f[sl], C - 1), (b_b[sl], 0)):
                ref_row = b[mid:mid + 1]
                tot = b[tot_row:tot_row + 1]
                q_rel = q * jnp.exp(b - ref_row)
                k_rel = k * jnp.exp(ref_row - b)
                scaled.append((q_rel.astype(BF16), k_rel.astype(BF16),
                               (q_rel * jnp.exp(ref_row)).astype(BF16),
                               (k_rel * jnp.exp(tot - ref_row)).T.astype(BF16),
                               tot))
            (qf, kf, qif, kuf_t, g_f), (qb, kb, qib, kub_t, g_b) = scaled
            qif_ref[rows, :] = qif
            qib_ref[rows, :] = qib
            p_parts, uf, ub = [], [], []
            for p in range(GLA_HEADS // 2):
                ks = slice(p * PK, (p + 1) * PK)
                kbd_f = jnp.where(key_diag, jnp.concatenate([kf[:, ks]] * 2, axis=0), 0)
                kbd_b = jnp.where(key_diag, jnp.concatenate([kb[:, ks]] * 2, axis=0), 0)
                s_fwd = lax.dot_general(qf[:, ks], kbd_f, nt, preferred_element_type=F32)
                s_bwd = lax.dot_general(qb[:, ks], kbd_b, nt, preferred_element_type=F32)
                p_parts.append((jnp.where(lower, s_fwd, 0.0) + jnp.where(upper, s_bwd, 0.0)).astype(BF16))
                u = jnp.dot(jnp.concatenate([kuf_t[ks, :], kub_t[ks, :]], axis=0), v_pair(rows, p),
                            preferred_element_type=F32)
                for blk, dst in ((u[:PK], uf), (u[PK:], ub)):
                    dst += [blk[:GLA_DK, :GLA_DV], blk[GLA_DK:, GLA_DV:]]
            p_ref[rows, :] = jnp.concatenate(p_parts, axis=1)
            ub_ref[c] = jnp.concatenate(ub, axis=0)
            gb_ref[c] = jnp.broadcast_to(g_b, (8, GLA_QK))
            spf_ref[c] = s_f.astype(BF16)
            s_f = col_bcast(jnp.exp(g_f)) * s_f + jnp.concatenate(uf, axis=0)
        return s_f

    sff_ref[0] = lax.fori_loop(0, n_blk, pass1, s0f_ref[0])

    def pass2(t, s_b):
        i = n_blk - 1 - t
        for g in reversed(range(G)):
            c = i * G + g
            rows = pl.ds(pl.multiple_of(i * R + g * C, C), C)
            spf = spf_ref[c]
            snb = s_b.astype(BF16)
            zero_blk = jnp.zeros((GLA_DK, GLA_DV), BF16)

            def pair_state(s, p):
                top = jnp.concatenate([s[2 * p * GLA_DK:(2 * p + 1) * GLA_DK], zero_blk], axis=1)
                bot = jnp.concatenate([zero_blk, s[(2 * p + 1) * GLA_DK:(2 * p + 2) * GLA_DK]], axis=1)
                return jnp.concatenate([top, bot], axis=0)

            outs = []
            for p in range(GLA_HEADS // 2):
                ks = slice(p * PK, (p + 1) * PK)
                vbd = jnp.where(val_diag, jnp.concatenate([v_pair(rows, p)] * 2, axis=0), 0)
                lhs = jnp.concatenate([p_ref[rows, ks], qif_ref[rows, ks], qib_ref[rows, ks]], axis=1)
                rhs = jnp.concatenate([vbd, pair_state(spf, p), pair_state(snb, p)], axis=0)
                o_pair = jnp.dot(lhs, rhs, preferred_element_type=F32)
                for o in (o_pair[:, :GLA_DV], o_pair[:, GLA_DV:]):
                    ms = jnp.mean(o * o, axis=-1, keepdims=True)
                    outs.append(o * lax.rsqrt(ms + EPS))
            o_all = jnp.concatenate(outs, axis=-1) * gnw_ref[...]
            og_ref[0, rows, :] = (o_all * _silu(r_ref[0, rows, :].astype(F32))).astype(BF16)
            s_b = col_bcast(jnp.exp(gb_ref[c][0:1])) * s_b + ub_ref[c]
        return s_b

    sfb_ref[0] = lax.fori_loop(0, n_blk, pass2, s0b_ref[0])


def _gla(qkv, afab, r, s0f, s0b, wab, bab, gnw):
    B, L, _ = qkv.shape
    n = L // CHUNK
    tok = lambda w: pl.BlockSpec((1, L, w), lambda b: (b, 0, 0))
    st = pl.BlockSpec((1, GLA_QK, GLA_DV), lambda b: (b, 0, 0))
    full = lambda a: pl.BlockSpec(a.shape, lambda b: (0,) * a.ndim)
    small = [wab, bab, gnw.reshape(1, GLA_V)]
    return pl.pallas_call(
        _gla_kernel,
        grid=(B,),
        in_specs=[tok(P_R - P_QKV), tok(AFAB_W), tok(GLA_V), st, st] + [full(a) for a in small],
        out_specs=[tok(GLA_V), st, st],
        out_shape=[jax.ShapeDtypeStruct((B, L, GLA_V), BF16),
                   jax.ShapeDtypeStruct((B, GLA_QK, GLA_DV), F32),
                   jax.ShapeDtypeStruct((B, GLA_QK, GLA_DV), F32)],
        scratch_shapes=[pltpu.VMEM((L, GLA_QK), BF16),
                        pltpu.VMEM((L, GLA_QK), BF16), pltpu.VMEM((L, GLA_QK), BF16),
                        pltpu.VMEM((n, GLA_QK, GLA_DV), BF16),
                        pltpu.VMEM((n, GLA_QK, GLA_DV), F32),
                        pltpu.VMEM((n, 8, GLA_QK), F32)],
        compiler_params=_params(("parallel",)),
        name="gla",
    )(qkv, afab, r, s0f, s0b, *small)


def _outproj_kernel(yc_ref, og_ref, ys_ref, x_ref, wo_ref, g2_ref, nw_ref, sh_ref, sc_ref, rwt_ref,
                    xo_ref, h2_ref, lt_ref):
    y = jnp.dot(yc_ref[0], wo_ref[0:D_CONF, :], preferred_element_type=F32)
    y = y + jnp.dot(og_ref[0], wo_ref[D_CONF:D_CONF + GLA_V, :], preferred_element_type=F32)
    y = y + jnp.dot(ys_ref[0], wo_ref[D_CONF + GLA_V:, :], preferred_element_type=F32)
    x = x_ref[0] + g2_ref[0] * y
    xo_ref[0] = x
    hb = _rms_mod(x, nw_ref[...], sh_ref[0], sc_ref[0]).astype(BF16)
    h2_ref[0] = hb
    lt_ref[0] = lax.dot_general(rwt_ref[...], hb, (((1,), (1,)), ((), ())),
                                preferred_element_type=F32)


def _outproj(yc, og, ys, x, wo, g2, nw, shift, scale, rwt):
    B, L, D = x.shape
    tm = min(L, 512)
    tok = lambda w: pl.BlockSpec((1, tm, w), lambda b, t: (b, t, 0))
    per_b = pl.BlockSpec((1, 1, D), lambda b, t: (b, 0, 0))
    return pl.pallas_call(
        _outproj_kernel,
        grid=(B, L // tm),
        in_specs=[tok(D_CONF), tok(GLA_V), tok(D_SC), tok(D),
                  pl.BlockSpec((D, D), lambda b, t: (0, 0)), per_b,
                  pl.BlockSpec((1, D), lambda b, t: (0, 0)), per_b, per_b,
                  pl.BlockSpec((N_EXPERTS, D), lambda b, t: (0, 0))],
        out_specs=[tok(D), tok(D), pl.BlockSpec((1, N_EXPERTS, tm), lambda b, t: (b, 0, t))],
        out_shape=[jax.ShapeDtypeStruct((B, L, D), F32), jax.ShapeDtypeStruct((B, L, D), BF16),
                   jax.ShapeDtypeStruct((B, N_EXPERTS, L), F32)],
        compiler_params=_params(("parallel", "parallel")),
        name="outproj",
    )(yc, og, ys, x, wo, g2, nw.reshape(1, D), shift, scale, rwt)


def _lane_cumsum(x):
    n = x.shape[-1]
    lane = lax.broadcasted_iota(jnp.int32, x.shape, x.ndim - 1)
    s = 1
    while s < n:
        x = x + jnp.where(lane >= s, pltpu.roll(x, s, axis=x.ndim - 1), 0)
        s *= 2
    return x


def _route_kernel(lt_ref, pos_ref, gate_ref, bnd_ref, *, cap):
    bs, E, L = lt_ref.shape
    lt = lt_ref[...]
    e = jnp.exp(lt - jnp.max(lt, axis=1, keepdims=True))
    aff = (e / jnp.sum(e, axis=1, keepdims=True)).reshape(bs * E, L)

    def search(i, t):
        cand = t | lax.shift_left(jnp.int32(1), 30 - i)
        cnt = jnp.sum((aff >= pltpu.bitcast(cand, F32)).astype(jnp.int32), axis=1, keepdims=True)
        return jnp.where(cnt >= cap, cand, t)

    thr_bits = lax.fori_loop(0, 31, search, jnp.zeros((bs * E, 1), jnp.int32))
    thr = pltpu.bitcast(thr_bits, F32)
    gt = aff > thr
    eq = aff == thr
    need = cap - jnp.sum(gt.astype(jnp.int32), axis=1, keepdims=True)
    eq_i = eq.astype(jnp.int32)
    eq_rank = _lane_cumsum(eq_i) - eq_i
    sel = gt | (eq & (eq_rank < need))
    sel_i = sel.astype(jnp.int32)
    slot = _lane_cumsum(sel_i) - sel_i
    pos_ref[...] = jnp.where(sel, slot, -1).reshape(bs, E, L)
    gate_ref[...] = aff.reshape(bs, E, L)
    bnd = jnp.concatenate([slot[:, k * TOK_TILE:k * TOK_TILE + 1] for k in range(L // TOK_TILE)], axis=1)
    bnd_ref[...] = bnd.reshape(bs, E, L // TOK_TILE)


def _route(lt, cap):
    B, E, L = lt.shape
    bs = min(B, ROUTE_SAMPLES)
    spec = pl.BlockSpec((bs, E, L), lambda b: (b, 0, 0))
    nt = L // TOK_TILE
    return pl.pallas_call(
        functools.partial(_route_kernel, cap=cap),
        grid=(B // bs,),
        in_specs=[spec], out_specs=[spec, spec, pl.BlockSpec((bs, E, nt), lambda b: (b, 0, 0))],
        out_shape=[jax.ShapeDtypeStruct((B, E, L), jnp.int32), jax.ShapeDtypeStruct((B, E, L), F32),
                   jax.ShapeDtypeStruct((B, E, nt), jnp.int32)],
        compiler_params=_params(("parallel",)),
        name="route",
    )(lt)


def _window(lo, hi, width, cap):
    a0 = jnp.minimum((lo // SLOT_ALIGN) * SLOT_ALIGN, cap - width)
    n = jnp.where(hi > lo, (hi - a0 + width - 1) // width, 0)
    return a0, n


def _one_hot_rows(pos_row, start, width, first_row=None):
    slot = lax.broadcasted_iota(jnp.int32, (width, pos_row.shape[1]), 0) + start
    on = 1.0 if first_row is None else jnp.where(slot >= first_row, 1.0, 0.0)
    return jnp.where(slot == pos_row, on, 0.0).astype(BF16)


def _gather_kernel(bnd_ref, pos_ref, h_ref, xs_ref, *, cap, tile, width):
    b = pl.program_id(0)
    L = h_ref.shape[1]
    nt = L // TOK_TILE
    step = tile // TOK_TILE
    n_tiles = L // tile
    xs_ref[...] = jnp.zeros_like(xs_ref)

    def tile_body(k, carry):
        toks = pl.ds(pl.multiple_of(k * tile, tile), tile)
        h_tile = h_ref[0, toks, :]
        ms, wins = [], []
        for e in range(N_EXPERTS):
            base = (b * N_EXPERTS + e) * nt
            lo = bnd_ref[base + k * step]
            hi = jnp.where(k + 1 < n_tiles, bnd_ref[base + jnp.minimum(k + 1, n_tiles - 1) * step], cap)
            a0, n = _window(lo, hi, width, cap)
            a0 = pl.multiple_of(a0, SLOT_ALIGN)
            pos_row = pos_ref[0, e:e + 1, toks]
            ms.append(_one_hot_rows(pos_row, a0, width))
            wins.append((a0, n, pos_row))
        part = jnp.dot(jnp.concatenate(ms, axis=0), h_tile, preferred_element_type=F32)
        for e, (a0, n, pos_row) in enumerate(wins):
            xs_ref[0, e, pl.ds(a0, width), :] += part[e * width:(e + 1) * width].astype(BF16)

            def extra(w, c, e=e, a0=a0, pos_row=pos_row):
                first = a0 + w * width
                start = pl.multiple_of(jnp.minimum(first, cap - width), SLOT_ALIGN)
                m = _one_hot_rows(pos_row, start, width, first)
                xs_ref[0, e, pl.ds(start, width), :] += jnp.dot(
                    m, h_ref[0, toks, :], preferred_element_type=F32).astype(BF16)
                return c

            lax.fori_loop(1, n, extra, 0)
        return carry

    lax.fori_loop(0, n_tiles, tile_body, 0)


def _gather(bnd, pos, h2, cap):
    B, L, D = h2.shape
    E = N_EXPERTS
    tile = min(L, 256)
    width = min(cap, 64)
    return pl.pallas_call(
        functools.partial(_gather_kernel, cap=cap, tile=tile, width=width),
        grid_spec=pltpu.PrefetchScalarGridSpec(
            num_scalar_prefetch=1, grid=(B,),
            in_specs=[pl.BlockSpec((1, E, L), lambda b, s: (b, 0, 0)),
                      pl.BlockSpec((1, L, D), lambda b, s: (b, 0, 0))],
            out_specs=pl.BlockSpec((1, E, cap, D), lambda b, s: (b, 0, 0, 0))),
        out_shape=jax.ShapeDtypeStruct((B, E, cap, D), BF16),
        compiler_params=_params(("arbitrary",)),
        name="gather",
    )(bnd.reshape(-1), pos, h2)


def _expert_ffn(pos_ref, gate_ref, xs_ref, y_ref, wb_ref):
    n, _, cap, D = xs_ref.shape
    L = pos_ref.shape[3]
    slot = lax.broadcasted_iota(jnp.int32, (cap, L), 0)
    gates = [jnp.sum(jnp.where(slot == pos_ref[i, 0], gate_ref[i, 0], 0.0), axis=1, keepdims=True)
             for i in range(n)]
    gate = jnp.concatenate(gates, axis=0)
    xs = xs_ref[:, 0].reshape(n * cap, D)
    hid = _silu(jnp.dot(xs, wb_ref[0], preferred_element_type=F32))
    hid = (hid * jnp.dot(xs, wb_ref[1], preferred_element_type=F32)).astype(BF16)
    y = (jnp.dot(hid, wb_ref[2], preferred_element_type=F32) * gate).astype(BF16)
    y_ref[:, 0] = y.reshape(n, cap, D)


def _ffn_kernel(*refs, with_ctx):
    if with_ctx:
        (pos_ref, gate_ref, xs_ref, cpos_ref, cgate_ref, cxs_ref, w1_ref, w3_ref, w2_ref,
         y_ref, cy_ref, wb_ref) = refs
    else:
        pos_ref, gate_ref, xs_ref, w1_ref, w3_ref, w2_ref, y_ref, wb_ref = refs
    j = pl.program_id(1)
    n_lat = pl.num_programs(1) - (1 if with_ctx else 0)

    @pl.when(j == 0)
    def _():
        for i, w_ref in enumerate((w1_ref, w3_ref, w2_ref)):
            wb_ref[i] = w_ref[0, 0].astype(BF16)

    @pl.when(j < n_lat)
    def _():
        _expert_ffn(pos_ref, gate_ref, xs_ref, y_ref, wb_ref)

    if with_ctx:
        @pl.when(j == n_lat)
        def _():
            _expert_ffn(cpos_ref, cgate_ref, cxs_ref, cy_ref, wb_ref)


def _ffn(lat, ctx, w1, w3, w2, layer, bb):
    pos, gate, xs = lat
    B, E, cap, D = xs.shape
    L = pos.shape[2]
    bb = min(bb, B)
    nb = B // bb
    last = nb - 1
    with_ctx = ctx is not None
    lat_idx = lambda e, j: (jnp.minimum(j, last), e, 0, 0)
    ins = [pos.reshape(B, E, 1, L), gate.reshape(B, E, 1, L), xs]
    in_specs = [pl.BlockSpec((bb, 1, 1, L), lat_idx), pl.BlockSpec((bb, 1, 1, L), lat_idx),
                pl.BlockSpec((bb, 1, cap, D), lat_idx)]
    out_shape = [jax.ShapeDtypeStruct((B, E, cap, D), BF16)]
    out_specs = [pl.BlockSpec((bb, 1, cap, D), lat_idx)]
    if with_ctx:
        cpos, cgate, cxs = ctx
        capc, Lc = cxs.shape[2], cpos.shape[2]
        ctx_idx = lambda e, j: (0, e, 0, 0)
        ins += [cpos.reshape(B, E, 1, Lc), cgate.reshape(B, E, 1, Lc), cxs]
        in_specs += [pl.BlockSpec((B, 1, 1, Lc), ctx_idx), pl.BlockSpec((B, 1, 1, Lc), ctx_idx),
                     pl.BlockSpec((B, 1, capc, D), ctx_idx)]
        out_shape.append(jax.ShapeDtypeStruct((B, E, capc, D), BF16))
        out_specs.append(pl.BlockSpec((B, 1, capc, D), ctx_idx))
    wspec = pl.BlockSpec((1, 1, D, D), lambda e, j: (layer, e, 0, 0))
    outs = pl.pallas_call(
        functools.partial(_ffn_kernel, with_ctx=with_ctx),
        grid=(E, nb + (1 if with_ctx else 0)),
        in_specs=in_specs + [wspec, wspec, wspec],
        out_specs=out_specs, out_shape=out_shape,
        scratch_shapes=[pltpu.VMEM((3, D, D), BF16)],
        compiler_params=_params(("arbitrary", "arbitrary")),
        name="ffn",
    )(*ins, w1, w3, w2)
    return outs if with_ctx else (outs[0], None)


def _combine_kernel(bnd_ref, pos_ref, y_ref, x_ref, g_ref, *rest, cap, width, final_norm):
    nw_ref, out_ref = rest if final_norm else (None, rest[0])
    b = pl.program_id(0)
    L = pos_ref.shape[2]
    nt = L // TOK_TILE
    tile = min(COMB_TILE, out_ref.shape[1])
    step = tile // TOK_TILE
    n_tiles = L // tile
    tiles_per_step = out_ref.shape[1] // tile
    tn = (((0,), (0,)), ((), ()))

    def tile_body(i, carry):
        t = pl.program_id(1) * tiles_per_step + i
        toks = pl.ds(pl.multiple_of(t * tile, tile), tile)
        rows = pl.ds(pl.multiple_of(i * tile, tile), tile)
        ms, ys, wins = [], [], []
        for e in range(N_EXPERTS):
            base = (b * N_EXPERTS + e) * nt
            lo = bnd_ref[base + t * step]
            hi = jnp.where(t + 1 < n_tiles, bnd_ref[base + jnp.minimum(t + 1, n_tiles - 1) * step], cap)
            a0, n = _window(lo, hi, width, cap)
            a0 = pl.multiple_of(a0, SLOT_ALIGN)
            pos_row = pos_ref[0, e:e + 1, toks]
            ms.append(_one_hot_rows(pos_row, a0, width))
            ys.append(y_ref[0, e, pl.ds(a0, width), :])
            wins.append((a0, n, pos_row))
        out_ref[0, rows, :] = lax.dot_general(jnp.concatenate(ms, axis=0), jnp.concatenate(ys, axis=0), tn,
                                              preferred_element_type=F32)
        for e, (a0, n, pos_row) in enumerate(wins):
            def extra(w, c, e=e, a0=a0, pos_row=pos_row):
                first = a0 + w * width
                start = pl.multiple_of(jnp.minimum(first, cap - width), SLOT_ALIGN)
                m = _one_hot_rows(pos_row, start, width, first)
                out_ref[0, rows, :] += lax.dot_general(m, y_ref[0, e, pl.ds(start, width), :], tn,
                                                       preferred_element_type=F32)
                return c

            lax.fori_loop(1, n, extra, 0)
        x = x_ref[0, rows, :] + g_ref[0] * out_ref[0, rows, :]
        if final_norm:
            x = x * lax.rsqrt(jnp.mean(x * x, axis=-1, keepdims=True) + EPS) * nw_ref[...]
        out_ref[0, rows, :] = x
        return carry

    lax.fori_loop(0, tiles_per_step, tile_body, 0)


def _combine(bnd, pos, y, x, g, final_nw):
    B, E, cap, D = y.shape
    L = x.shape[1]
    width = min(cap, 64)
    tm = min(L, 512)
    final_norm = final_nw is not None
    tok = pl.BlockSpec((1, tm, D), lambda b, t, s: (b, t, 0))
    ins = [bnd.reshape(-1), pos, y, x, g]
    in_specs = [pl.BlockSpec((1, E, L), lambda b, t, s: (b, 0, 0)),
                pl.BlockSpec((1, E, cap, D), lambda b, t, s: (b, 0, 0, 0)),
                tok, pl.BlockSpec((1, 1, D), lambda b, t, s: (b, 0, 0))]
    if final_norm:
        ins.append(final_nw.reshape(1, D))
        in_specs.append(pl.BlockSpec((1, D), lambda b, t, s: (0, 0)))
    return pl.pallas_call(
        functools.partial(_combine_kernel, cap=cap, width=width, final_norm=final_norm),
        grid_spec=pltpu.PrefetchScalarGridSpec(
            num_scalar_prefetch=1, grid=(B, L // tm), in_specs=in_specs, out_specs=tok),
        out_shape=jax.ShapeDtypeStruct((B, L, D), F32),
        compiler_params=_params(("arbitrary", "arbitrary")),
        name="combine",
    )(*ins)


def _moe(lat, ctx, w1, w3, w2, layer, bb, final_nw=None):
    routed = []
    for lt, h2, x, g in (lat, ctx) if ctx is not None else (lat,):
        cap = EC_FACTOR * h2.shape[1] // N_EXPERTS
        pos, gate, bnd = _route(lt, cap)
        routed.append((pos, gate, _gather(bnd, pos, h2, cap), bnd, x, g))
    ys = _ffn(routed[0][:3], routed[1][:3] if ctx is not None else None, w1, w3, w2, layer, bb)
    outs = [_combine(bnd, pos, y, x, g, final_nw if i == 0 else None)
            for i, ((pos, _, _, bnd, x, g), y) in enumerate(zip(routed, ys))]
    return outs[0], (outs[1] if ctx is not None else None)


def _permute_w_in(w):
    cols = jnp.concatenate([w[:, 0:OFF_AF], w[:, OFF_R:OFF_SC], w[:, OFF_SC:D_IN], w[:, OFF_AF:OFF_R]],
                           axis=1)
    return jnp.pad(cols, ((0, 0), (0, P_TOTAL - D_IN))).astype(BF16)


def kernel(x, c, ctx, c_ctx, ada_w, ada_b, norm1_w, norm2_w, w_in, conf_dw_w, conf_dw_b, conf_ln_w,
           conf_ln_b, gla_wa_f, gla_ba_f, gla_wa_b, gla_ba_b, gla_gn_w, sc_w, sc_b, w_out, router_w,
           exp_w1, exp_w3, exp_w2, final_norm_w):
    B, L, D = x.shape
    Lc = ctx.shape[1]

    cond = jnp.concatenate([c, c_ctx[None, :], jnp.zeros((7, D), F32)], axis=0)
    mods = _ada(cond, ada_w, ada_b)

    xc = ctx
    zero_state = jnp.zeros((B, GLA_QK, GLA_DV), F32)
    for i in range(DEPTH):
        last = i == DEPTH - 1
        m_lat = [mods[i, :B, j * D:(j + 1) * D].reshape(B, 1, D) for j in range(6)]
        m_ctx = [jnp.broadcast_to(mods[i, B, j * D:(j + 1) * D].reshape(1, 1, D), (B, 1, D))
                 for j in range(6)]
        w_perm = _permute_w_in(w_in[i])
        wo = w_out[i].astype(BF16)
        rwt = router_w[i].T.astype(BF16)
        zero_wa = jnp.zeros((GLA_RANK, GLA_QK), F32)
        wa = jnp.block([[gla_wa_f[i], zero_wa], [zero_wa, gla_wa_b[i]]]).astype(BF16)
        ba = jnp.concatenate([gla_ba_f[i], gla_ba_b[i]]).reshape(1, 2 * GLA_QK)
        experts = (exp_w1, exp_w3, exp_w2, i)
        conv_w = (conf_dw_w[i], conf_dw_b[i], conf_ln_w[i], conf_ln_b[i], sc_w[i], sc_b[i])

        c_conf, c_qkv, c_r, c_sc, c_afab = _proj(xc, norm1_w[i], m_ctx[0], m_ctx[1], w_perm)
        c_og, c_sf, c_sb = _gla(c_qkv, c_afab, c_r, zero_state, zero_state, wa, ba, gla_gn_w[i])

        conf, qkv, r, scu, afab = _proj(x, norm1_w[i], m_lat[0], m_lat[1], w_perm)
        yc, ys = _convs(conf, scu, *conv_w, row_len=GRID_W)
        og, _, _ = _gla(qkv, afab, r, c_sf, c_sb, wa, ba, gla_gn_w[i])
        x, h2, lt = _outproj(yc, og, ys, x, wo, m_lat[2], norm2_w[i], m_lat[3], m_lat[4], rwt)
        moe_ctx = None
        if not last:
            c_yc, c_ys = _convs(c_conf, c_sc, *conv_w, row_len=Lc)
            xc, c_h2, c_lt = _outproj(c_yc, c_og, c_ys, xc, wo, m_ctx[2], norm2_w[i], m_ctx[3], m_ctx[4],
                                      rwt)
            moe_ctx = (c_lt, c_h2, xc, m_ctx[5])
        x, xc = _moe((lt, h2, x, m_lat[5]), moe_ctx, *experts, FFN_SAMPLES,
                     final_norm_w if last else None)
    return x
```

```python
import functools

import jax
import jax.numpy as jnp
from jax import lax
from jax.experimental import pallas as pl
from jax.experimental.pallas import tpu as pltpu

F32 = jnp.float32
BF16 = jnp.bfloat16
HI = lax.Precision.HIGHEST

D_MODEL = 1024
DEPTH = 2
GRID_W = 64
D_CONF = 256
D_SC = 256
GLA_HEADS = 4
GLA_DK = 64
GLA_DV = 128
GLA_QK = GLA_HEADS * GLA_DK
GLA_V = GLA_HEADS * GLA_DV
GLA_RANK = 16
GLA_TAU = 16.0
CONF_K = 31
SC_K = 3
N_EXPERTS = 16
EC_FACTOR = 2
EPS = 1e-6

OFF_Q = 2 * D_CONF
OFF_K = OFF_Q + GLA_QK
OFF_V = OFF_K + GLA_QK
OFF_AF = OFF_V + GLA_V
OFF_AB = OFF_AF + GLA_RANK
OFF_R = OFF_AB + GLA_RANK
OFF_SC = OFF_R + GLA_V
D_IN = OFF_SC + 3 * D_SC

P_CONF = 0
P_QKV = 512
P_R = 1536
P_SC = 2048
P_AFAB = 2816
P_TOTAL = 2944
AFAB_W = 128

CHUNK = 64
GLA_BLOCK = 8
TOKEN_BLOCK = 1024
CUMSUM_ROWS = 256
CONV_SUB = 64
CONV_TOKENS = 8
SUBLANES = 8
LANES = 128
CONV_PAD = 16
SC_PAD = 8
TOK_TILE = 128
COMB_TILE = 256
TILES_PER_ITER = 2
ROUTE_SAMPLES = 4
FFN_SAMPLES = 4
SLOT_ALIGN = 16
VMEM_LIMIT = 56 * 1024 * 1024


def _sigmoid(x):
    return 1.0 / (1.0 + jnp.exp(-x))


def _silu(x):
    return x * _sigmoid(x)


def _log_sigmoid(z):
    return jnp.minimum(z, 0.0) - jnp.log(1.0 + jnp.exp(-jnp.abs(z)))


def _params(sem):
    return pltpu.CompilerParams(dimension_semantics=sem, vmem_limit_bytes=VMEM_LIMIT)


def _ada_kernel(s_ref, w_ref, b_ref, o_ref):
    s = _silu(s_ref[...])
    o_ref[0] = jnp.dot(s, w_ref[0], precision=HI, preferred_element_type=F32) + b_ref[0]


def _ada(s_in, ada_w, ada_b):
    rows = s_in.shape[0]
    tn = 1024
    return pl.pallas_call(
        _ada_kernel,
        grid=(DEPTH, 6 * D_MODEL // tn),
        in_specs=[pl.BlockSpec((rows, D_MODEL), lambda l, n: (0, 0)),
                  pl.BlockSpec((1, D_MODEL, tn), lambda l, n: (l, 0, n)),
                  pl.BlockSpec((1, 1, tn), lambda l, n: (l, 0, n))],
        out_specs=pl.BlockSpec((1, rows, tn), lambda l, n: (l, 0, n)),
        out_shape=jax.ShapeDtypeStruct((DEPTH, rows, 6 * D_MODEL), F32),
        compiler_params=_params(("arbitrary", "arbitrary")),
        name="ada",
    )(s_in, ada_w, ada_b.reshape(DEPTH, 1, 6 * D_MODEL))


def _rms_mod(x, nw, shift, scale):
    ms = jnp.mean(x * x, axis=-1, keepdims=True)
    return (x * lax.rsqrt(ms + EPS) * nw) * (1.0 + scale) + shift


def _proj_kernel(x_ref, nw_ref, sh_ref, sc_ref, w_ref, conf_ref, qkv_ref, r_ref, scu_ref, afab_ref):
    hb = _rms_mod(x_ref[0], nw_ref[...], sh_ref[0], sc_ref[0]).astype(BF16)

    def proj(lo, hi):
        return jnp.dot(hb, w_ref[:, lo:hi], preferred_element_type=F32)

    conf_ref[0] = proj(P_CONF, P_QKV).astype(BF16)
    qkv_ref[0] = proj(P_QKV, P_R).astype(BF16)
    r_ref[0] = proj(P_R, P_SC).astype(BF16)
    scu_ref[0] = proj(P_SC, P_AFAB).astype(BF16)
    afab_ref[0] = proj(P_AFAB, P_TOTAL)


def _proj(x, nw, shift, scale, w_perm):
    B, L, D = x.shape
    tm = min(L, TOKEN_BLOCK)
    tok = lambda w: pl.BlockSpec((1, tm, w), lambda b, t: (b, t, 0))
    per_b = pl.BlockSpec((1, 1, D), lambda b, t: (b, 0, 0))
    widths = [(2 * D_CONF, BF16), (P_R - P_QKV, BF16), (GLA_V, BF16), (3 * D_SC, BF16), (AFAB_W, F32)]
    return pl.pallas_call(
        _proj_kernel,
        grid=(B, L // tm),
        in_specs=[tok(D), pl.BlockSpec((1, D), lambda b, t: (0, 0)), per_b, per_b,
                  pl.BlockSpec((D, P_TOTAL), lambda b, t: (0, 0))],
        out_specs=[tok(w) for w, _ in widths],
        out_shape=[jax.ShapeDtypeStruct((B, L, w), dt) for w, dt in widths],
        compiler_params=_params(("parallel", "parallel")),
        name="proj",
    )(x, nw.reshape(1, D), shift, scale, w_perm)


def _conv_kernel(conf_ref, scu_ref, cw_ref, cb_ref, lnw_ref, lnb_ref, sw_ref, sb_ref,
                 yc_ref, ys_ref, yt_ref, ot_ref, pz_ref, *, row_len):
    L = conf_ref.shape[1]
    n_rows = L // row_len
    C = D_CONF
    S = SUBLANES

    split_row = n_rows == 1
    assert split_row or n_rows % S == 0
    t_out = row_len // S if split_row else row_len
    span = t_out + 2 * CONV_PAD

    halves = [slice(h * LANES, (h + 1) * LANES) for h in range(C // LANES)]

    def group(g, carry):
        for h in range(len(halves)):
            if not split_row:
                yt_ref[h, 0:CONV_PAD * S, :] = jnp.zeros((CONV_PAD * S, LANES), F32)
                yt_ref[h, (CONV_PAD + t_out) * S:, :] = jnp.zeros((CONV_PAD * S, LANES), F32)
        for k in range(S):
            if split_row:
                tok0 = k * t_out - CONV_PAD
                lo, hi = max(tok0, 0), min(tok0 + span, L)
                src = conf_ref[0, lo:hi, :]
                for a, b in ((0, lo - tok0), (hi - tok0, span)):
                    for h in range(len(halves)):
                        if b > a:
                            yt_ref[h, pl.ds(k + S * a, b - a, stride=S), :] = jnp.zeros((b - a, LANES), F32)
                first = lo - tok0
            else:
                src = conf_ref[0, pl.ds(pl.multiple_of((g * S + k) * row_len, row_len), row_len), :]
                first = CONV_PAD
            u = src.astype(F32)
            z = u[:, :C] * _sigmoid(u[:, C:])
            for h, ls in enumerate(halves):
                yt_ref[h, pl.ds(k + S * first, z.shape[0], stride=S), :] = z[:, ls]

        def out_rows(t):
            return pl.ds(pl.multiple_of(t * S, S), S)

        for h, ls in enumerate(halves):
            def taps(i, c, h=h, ls=ls):
                t0 = i * CONV_TOKENS
                accs = [None] * CONV_TOKENS
                for m in range(CONV_TOKENS + CONF_K - 1):
                    v = yt_ref[h, out_rows(t0 + CONV_PAD - CONF_K // 2 + m), :]
                    for tt in range(max(0, m - CONF_K + 1), min(CONV_TOKENS, m + 1)):
                        term = cw_ref[m - tt:m - tt + 1, ls] * v
                        accs[tt] = term if accs[tt] is None else accs[tt] + term
                for tt in range(CONV_TOKENS):
                    ot_ref[h, out_rows(t0 + tt), :] = accs[tt] + cb_ref[:, ls]
                return c

            lax.fori_loop(0, t_out // CONV_TOKENS, taps, 0)

        for k in range(S):
            dst = k * t_out if split_row else pl.multiple_of((g * S + k) * row_len, row_len)
            o = jnp.concatenate([ot_ref[h, pl.ds(k, t_out, stride=S), :] for h in range(len(halves))], axis=1)
            mu = jnp.mean(o, axis=-1, keepdims=True)
            cen = o - mu
            var = jnp.mean(cen * cen, axis=-1, keepdims=True)
            y = cen * lax.rsqrt(var + EPS) * lnw_ref[...] + lnb_ref[...]
            yc_ref[0, pl.ds(dst, t_out), :] = _silu(y).astype(BF16)
        return carry

    lax.fori_loop(0, 1 if split_row else n_rows // S, group, 0)

    n_blk = L // CONV_SUB
    pz_ref[0:SC_PAD, :] = jnp.zeros((SC_PAD, D_SC), F32)
    pz_ref[SC_PAD + L:, :] = jnp.zeros((SC_PAD, D_SC), F32)

    def fill_p(i, carry):
        start = pl.multiple_of(i * CONV_SUB, CONV_SUB)
        u = scu_ref[0, pl.ds(start, CONV_SUB), :].astype(F32)
        pz_ref[pl.ds(SC_PAD + start, CONV_SUB), :] = u[:, D_SC:2 * D_SC] * u[:, 2 * D_SC:]
        return carry

    lax.fori_loop(0, n_blk, fill_p, 0)

    def sc_blk(i, carry):
        start = pl.multiple_of(i * CONV_SUB, CONV_SUB)
        win = pz_ref[pl.ds(start, CONV_SUB + 2 * SC_PAD), :]
        acc = sb_ref[...] + sw_ref[0:1, :] * win[SC_PAD - 1:SC_PAD - 1 + CONV_SUB]
        acc = acc + sw_ref[1:2, :] * win[SC_PAD:SC_PAD + CONV_SUB]
        acc = acc + sw_ref[2:3, :] * win[SC_PAD + 1:SC_PAD + 1 + CONV_SUB]
        bg = scu_ref[0, pl.ds(start, CONV_SUB), 0:D_SC].astype(F32)
        ys_ref[0, pl.ds(start, CONV_SUB), :] = (bg * acc).astype(BF16)
        return carry

    lax.fori_loop(0, n_blk, sc_blk, 0)


def _convs(conf, scu, cw, cb, lnw, lnb, sw, sb, row_len):
    B, L, _ = conf.shape
    t_out = row_len // SUBLANES if L == row_len else row_len
    full = lambda a: pl.BlockSpec(a.shape, lambda b: (0,) * a.ndim)
    cw_p = jnp.zeros((32, D_CONF), F32).at[:CONF_K].set(cw)
    sw_p = jnp.zeros((8, D_SC), F32).at[:SC_K].set(sw)
    small = [cw_p, cb.reshape(1, D_CONF), lnw.reshape(1, D_CONF), lnb.reshape(1, D_CONF),
             sw_p, sb.reshape(1, D_SC)]
    return pl.pallas_call(
        functools.partial(_conv_kernel, row_len=row_len),
        grid=(B,),
        in_specs=[pl.BlockSpec((1, L, 2 * D_CONF), lambda b: (b, 0, 0)),
                  pl.BlockSpec((1, L, 3 * D_SC), lambda b: (b, 0, 0))] + [full(a) for a in small],
        out_specs=[pl.BlockSpec((1, L, D_CONF), lambda b: (b, 0, 0)),
                   pl.BlockSpec((1, L, D_SC), lambda b: (b, 0, 0))],
        out_shape=[jax.ShapeDtypeStruct((B, L, D_CONF), BF16),
                   jax.ShapeDtypeStruct((B, L, D_SC), BF16)],
        scratch_shapes=[pltpu.VMEM((D_CONF // LANES, (t_out + 2 * CONV_PAD) * SUBLANES, LANES), F32),
                        pltpu.VMEM((D_CONF // LANES, t_out * SUBLANES, LANES), F32),
                        pltpu.VMEM((L + 2 * SC_PAD, D_SC), F32)],
        compiler_params=_params(("parallel",)),
        name="convs",
    )(conf, scu, *small)


def _gla_kernel(qkv_ref, afab_ref, r_ref, s0f_ref, s0b_ref, wab_ref, bab_ref, gnw_ref,
                og_ref, sff_ref, sfb_ref, p_ref, qif_ref, qib_ref, spf_ref, ub_ref, gb_ref):
    L = qkv_ref.shape[1]
    C = CHUNK
    G = min(GLA_BLOCK, L // C)
    R = G * C
    n_blk = L // R
    mid = C // 2
    scale = GLA_DK ** -0.5
    nt = (((1,), (1,)), ((), ()))
    T = min(R, CUMSUM_ROWS)
    ii = lax.broadcasted_iota(jnp.int32, (T, T), 0)
    jj = lax.broadcasted_iota(jnp.int32, (T, T), 1)
    same = (ii // C) == (jj // C)
    tri_f = jnp.where(same, jnp.where(ii >= jj, 1.0, 0.0), 0.0).astype(BF16)
    tri_b = jnp.where(same, jnp.where(jj >= ii, 1.0, 0.0), 0.0).astype(BF16)
    PK = 2 * GLA_DK
    PV = 2 * GLA_DV
    ci = lax.broadcasted_iota(jnp.int32, (C, 2 * C), 0)
    cj = lax.broadcasted_iota(jnp.int32, (C, 2 * C), 1) % C
    lower = ci >= cj
    upper = cj >= ci
    kr = lax.broadcasted_iota(jnp.int32, (2 * C, PK), 0) // C
    kc = lax.broadcasted_iota(jnp.int32, (2 * C, PK), 1) // GLA_DK
    key_diag = kr == kc
    vr = lax.broadcasted_iota(jnp.int32, (2 * C, PV), 0) // C
    vc = lax.broadcasted_iota(jnp.int32, (2 * C, PV), 1) // GLA_DV
    val_diag = vr == vc

    def col_bcast(row_vec):
        return jnp.broadcast_to(row_vec, (GLA_DV, GLA_QK)).T

    def chunk_cumsum(tri, la):
        hi = la.astype(BF16)
        lo = (la - hi.astype(F32)).astype(BF16)
        parts = jnp.concatenate([hi, lo], axis=1)
        both = jnp.concatenate([jnp.dot(tri, parts[s * T:(s + 1) * T], preferred_element_type=F32)
                                for s in range(R // T)], axis=0)
        return both[:, :GLA_QK] + both[:, GLA_QK:]

    def v_pair(rows, p):
        return qkv_ref[0, rows, 2 * GLA_QK + p * PV:2 * GLA_QK + (p + 1) * PV]

    def pass1(i, s_f):
        r0 = pl.multiple_of(i * R, R)
        ab = afab_ref[0, pl.ds(r0, R), 0:2 * GLA_RANK].astype(BF16)
        z = jnp.dot(ab, wab_ref[...], preferred_element_type=F32) + bab_ref[...]
        la = _log_sigmoid(z) * (1.0 / GLA_TAU)
        b_f = chunk_cumsum(tri_f, la[:, :GLA_QK])
        b_b = chunk_cumsum(tri_b, la[:, GLA_QK:])
        q_all = qkv_ref[0, pl.ds(r0, R), 0:GLA_QK].astype(F32) * scale
        k_all = qkv_ref[0, pl.ds(r0, R), GLA_QK:2 * GLA_QK].astype(F32)
        for g in range(G):
            c = i * G + g
            rows = pl.ds(pl.multiple_of(r0 + g * C, C), C)
            sl = slice(g * C, (g + 1) * C)
            q, k = q_all[sl], k_all[sl]
            scaled = []
            for b, tot_row in ((b_f[sl], C - 1), (b_b[sl], 0)):
                ref_row = b[mid:mid + 1]
                tot = b[tot_row:tot_row + 1]
                q_rel = q * jnp.exp(b - ref_row)
                k_rel = k * jnp.exp(ref_row - b)
                scaled.append((q_rel.astype(BF16), k_rel.astype(BF16),
                               (q_rel * jnp.exp(ref_row)).astype(BF16),
                               (k_rel * jnp.exp(tot - ref_row)).T.astype(BF16),
                               tot))
            (qf, kf, qif, kuf_t, g_f), (qb, kb, qib, kub_t, g_b) = scaled
            qif_ref[rows, :] = qif
            qib_ref[rows, :] = qib
            p_parts, uf, ub = [], [], []
            for p in range(GLA_HEADS // 2):
                ks = slice(p * PK, (p + 1) * PK)
                kbd_f = jnp.where(key_diag, jnp.concatenate([kf[:, ks]] * 2, axis=0), 0)
                kbd_b = jnp.where(key_diag, jnp.concatenate([kb[:, ks]] * 2, axis=0), 0)
                s_fwd = lax.dot_general(qf[:, ks], kbd_f, nt, preferred_element_type=F32)
                s_bwd = lax.dot_general(qb[:, ks], kbd_b, nt, preferred_element_type=F32)
                p_parts.append((jnp.where(lower, s_fwd, 0.0) + jnp.where(upper, s_bwd, 0.0)).astype(BF16))
                u = jnp.dot(jnp.concatenate([kuf_t[ks, :], kub_t[ks, :]], axis=0), v_pair(rows, p),
                            preferred_element_type=F32)
                for blk, dst in ((u[:PK], uf), (u[PK:], ub)):
                    dst += [blk[:GLA_DK, :GLA_DV], blk[GLA_DK:, GLA_DV:]]
            p_ref[rows, :] = jnp.concatenate(p_parts, axis=1)
            ub_ref[c] = jnp.concatenate(ub, axis=0)
            gb_ref[c] = jnp.broadcast_to(g_b, (8, GLA_QK))
            spf_ref[c] = s_f.astype(BF16)
            s_f = col_bcast(jnp.exp(g_f)) * s_f + jnp.concatenate(uf, axis=0)
        return s_f

    sff_ref[0] = lax.fori_loop(0, n_blk, pass1, s0f_ref[0])

    def pass2(t, s_b):
        i = n_blk - 1 - t
        for g in reversed(range(G)):
            c = i * G + g
            rows = pl.ds(pl.multiple_of(i * R + g * C, C), C)
            spf = spf_ref[c]
            snb = s_b.astype(BF16)
            zero_blk = jnp.zeros((GLA_DK, GLA_DV), BF16)

            def pair_state(s, p):
                top = jnp.concatenate([s[2 * p * GLA_DK:(2 * p + 1) * GLA_DK], zero_blk], axis=1)
                bot = jnp.concatenate([zero_blk, s[(2 * p + 1) * GLA_DK:(2 * p + 2) * GLA_DK]], axis=1)
                return jnp.concatenate([top, bot], axis=0)

            outs = []
            for p in range(GLA_HEADS // 2):
                ks = slice(p * PK, (p + 1) * PK)
                vbd = jnp.where(val_diag, jnp.concatenate([v_pair(rows, p)] * 2, axis=0), 0)
                lhs = jnp.concatenate([p_ref[rows, ks], qif_ref[rows, ks], qib_ref[rows, ks]], axis=1)
                rhs = jnp.concatenate([vbd, pair_state(spf, p), pair_state(snb, p)], axis=0)
                o_pair = jnp.dot(lhs, rhs, preferred_element_type=F32)
                for o in (o_pair[:, :GLA_DV], o_pair[:, GLA_DV:]):
                    ms = jnp.mean(o * o, axis=-1, keepdims=True)
                    outs.append(o * lax.rsqrt(ms + EPS))
            o_all = jnp.concatenate(outs, axis=-1) * gnw_ref[...]
            og_ref[0, rows, :] = (o_all * _silu(r_ref[0, rows, :].astype(F32))).astype(BF16)
            s_b = col_bcast(jnp.exp(gb_ref[c][0:1])) * s_b + ub_ref[c]
        return s_b

    sfb_ref[0] = lax.fori_loop(0, n_blk, pass2, s0b_ref[0])


def _gla(qkv, afab, r, s0f, s0b, wab, bab, gnw):
    B, L, _ = qkv.shape
    n = L // CHUNK
    tok = lambda w: pl.BlockSpec((1, L, w), lambda b: (b, 0, 0))
    st = pl.BlockSpec((1, GLA_QK, GLA_DV), lambda b: (b, 0, 0))
    full = lambda a: pl.BlockSpec(a.shape, lambda b: (0,) * a.ndim)
    small = [wab, bab, gnw.reshape(1, GLA_V)]
    return pl.pallas_call(
        _gla_kernel,
        grid=(B,),
        in_specs=[tok(P_R - P_QKV), tok(AFAB_W), tok(GLA_V), st, st] + [full(a) for a in small],
        out_specs=[tok(GLA_V), st, st],
        out_shape=[jax.ShapeDtypeStruct((B, L, GLA_V), BF16),
                   jax.ShapeDtypeStruct((B, GLA_QK, GLA_DV), F32),
                   jax.ShapeDtypeStruct((B, GLA_QK, GLA_DV), F32)],
        scratch_shapes=[pltpu.VMEM((L, GLA_QK), BF16),
                        pltpu.VMEM((L, GLA_QK), BF16), pltpu.VMEM((L, GLA_QK), BF16),
                        pltpu.VMEM((n, GLA_QK, GLA_DV), BF16),
                        pltpu.VMEM((n, GLA_QK, GLA_DV), F32),
                        pltpu.VMEM((n, 8, GLA_QK), F32)],
        compiler_params=_params(("parallel",)),
        name="gla",
    )(qkv, afab, r, s0f, s0b, *small)


def _outproj_kernel(yc_ref, og_ref, ys_ref, x_ref, wo_ref, g2_ref, nw_ref, sh_ref, sc_ref, rwt_ref,
                    xo_ref, h2_ref, lt_ref):
    y = jnp.dot(yc_ref[0], wo_ref[0:D_CONF, :], preferred_element_type=F32)
    y = y + jnp.dot(og_ref[0], wo_ref[D_CONF:D_CONF + GLA_V, :], preferred_element_type=F32)
    y = y + jnp.dot(ys_ref[0], wo_ref[D_CONF + GLA_V:, :], preferred_element_type=F32)
    x = x_ref[0] + g2_ref[0] * y
    xo_ref[0] = x
    hb = _rms_mod(x, nw_ref[...], sh_ref[0], sc_ref[0]).astype(BF16)
    h2_ref[0] = hb
    lt_ref[0] = lax.dot_general(rwt_ref[...], hb, (((1,), (1,)), ((), ())),
                                preferred_element_type=F32)


def _outproj(yc, og, ys, x, wo, g2, nw, shift, scale, rwt):
    B, L, D = x.shape
    tm = min(L, TOKEN_BLOCK)
    tok = lambda w: pl.BlockSpec((1, tm, w), lambda b, t: (b, t, 0))
    per_b = pl.BlockSpec((1, 1, D), lambda b, t: (b, 0, 0))
    return pl.pallas_call(
        _outproj_kernel,
        grid=(B, L // tm),
        in_specs=[tok(D_CONF), tok(GLA_V), tok(D_SC), tok(D),
                  pl.BlockSpec((D, D), lambda b, t: (0, 0)), per_b,
                  pl.BlockSpec((1, D), lambda b, t: (0, 0)), per_b, per_b,
                  pl.BlockSpec((N_EXPERTS, D), lambda b, t: (0, 0))],
        out_specs=[tok(D), tok(D), pl.BlockSpec((1, N_EXPERTS, tm), lambda b, t: (b, 0, t))],
        out_shape=[jax.ShapeDtypeStruct((B, L, D), F32), jax.ShapeDtypeStruct((B, L, D), BF16),
                   jax.ShapeDtypeStruct((B, N_EXPERTS, L), F32)],
        compiler_params=_params(("parallel", "parallel")),
        name="outproj",
    )(yc, og, ys, x, wo, g2, nw.reshape(1, D), shift, scale, rwt)


def _lane_cumsum(x):
    n = x.shape[-1]
    lane = lax.broadcasted_iota(jnp.int32, x.shape, x.ndim - 1)
    s = 1
    while s < n:
        x = x + jnp.where(lane >= s, pltpu.roll(x, s, axis=x.ndim - 1), 0)
        s *= 2
    return x


def _route_kernel(lt_ref, pos_ref, gate_ref, bnd_ref, *, cap):
    bs, E, L = lt_ref.shape
    lt = lt_ref[...]
    e = jnp.exp(lt - jnp.max(lt, axis=1, keepdims=True))
    aff = (e / jnp.sum(e, axis=1, keepdims=True)).reshape(bs * E, L)

    def search(i, t):
        cand = t | lax.shift_left(jnp.int32(1), 30 - i)
        cnt = jnp.sum((aff >= pltpu.bitcast(cand, F32)).astype(jnp.int32), axis=1, keepdims=True)
        return jnp.where(cnt >= cap, cand, t)

    thr_bits = lax.fori_loop(0, 31, search, jnp.zeros((bs * E, 1), jnp.int32))
    thr = pltpu.bitcast(thr_bits, F32)
    gt = aff > thr
    eq = aff == thr
    need = cap - jnp.sum(gt.astype(jnp.int32), axis=1, keepdims=True)
    eq_i = eq.astype(jnp.int32)
    eq_rank = _lane_cumsum(eq_i) - eq_i
    sel = gt | (eq & (eq_rank < need))
    sel_i = sel.astype(jnp.int32)
    slot = _lane_cumsum(sel_i) - sel_i
    pos_ref[...] = jnp.where(sel, slot, -1).reshape(bs, E, L)
    gate_ref[...] = aff.reshape(bs, E, L)
    bnd = jnp.concatenate([slot[:, k * TOK_TILE:k * TOK_TILE + 1] for k in range(L // TOK_TILE)], axis=1)
    bnd_ref[...] = bnd.reshape(bs, E, L // TOK_TILE)


def _route(lt, cap):
    B, E, L = lt.shape
    bs = min(B, ROUTE_SAMPLES)
    spec = pl.BlockSpec((bs, E, L), lambda b: (b, 0, 0))
    nt = L // TOK_TILE
    return pl.pallas_call(
        functools.partial(_route_kernel, cap=cap),
        grid=(B // bs,),
        in_specs=[spec], out_specs=[spec, spec, pl.BlockSpec((bs, E, nt), lambda b: (b, 0, 0))],
        out_shape=[jax.ShapeDtypeStruct((B, E, L), jnp.int32), jax.ShapeDtypeStruct((B, E, L), F32),
                   jax.ShapeDtypeStruct((B, E, nt), jnp.int32)],
        compiler_params=_params(("parallel",)),
        name="route",
    )(lt)


def _window(lo, hi, width, cap):
    a0 = jnp.minimum((lo // SLOT_ALIGN) * SLOT_ALIGN, cap - width)
    n = jnp.where(hi > lo, (hi - a0 + width - 1) // width, 0)
    return a0, n


def _one_hot_rows(pos_row, start, width, first_row=None):
    slot = lax.broadcasted_iota(jnp.int32, (width, pos_row.shape[1]), 0) + start
    on = 1.0 if first_row is None else jnp.where(slot >= first_row, 1.0, 0.0)
    return jnp.where(slot == pos_row, on, 0.0).astype(BF16)


def _tile_windows(bnd_ref, pos_ref, b, t, toks, step, n_tiles, cap, width):
    nt = pos_ref.shape[2] // TOK_TILE
    wins = []
    for e in range(N_EXPERTS):
        base = (b * N_EXPERTS + e) * nt
        lo = bnd_ref[base + t * step]
        hi = jnp.where(t + 1 < n_tiles, bnd_ref[base + jnp.minimum(t + 1, n_tiles - 1) * step], cap)
        a0, n = _window(lo, hi, width, cap)
        wins.append((pl.multiple_of(a0, SLOT_ALIGN), n, pos_ref[0, e:e + 1, toks]))
    return wins


def _window_one_hots(wins, w, cap, width):
    ms, starts = [], []
    for a0, n, pos_row in wins:
        if w is None:
            start, first = a0, None
        else:
            first = a0 + w * width
            start = pl.multiple_of(jnp.minimum(first, cap - width), SLOT_ALIGN)
            first = jnp.where(w < n, first, cap)
        ms.append(_one_hot_rows(pos_row, start, width, first))
        starts.append(start)
    return jnp.concatenate(ms, axis=0), starts


def _max_windows(wins):
    return functools.reduce(jnp.maximum, [n for _, n, _ in wins])


def _gather_kernel(bnd_ref, pos_ref, h_ref, xs_ref, *, cap, tile, width):
    b = pl.program_id(0)
    L = h_ref.shape[1]
    step = tile // TOK_TILE
    n_tiles = L // tile
    unroll = min(TILES_PER_ITER, n_tiles)
    xs_ref[...] = jnp.zeros_like(xs_ref)

    def scatter_windows(m_all, starts, toks):
        part = jnp.dot(m_all, h_ref[0, toks, :], preferred_element_type=F32)
        for e, start in enumerate(starts):
            xs_ref[0, e, pl.ds(start, width), :] += part[e * width:(e + 1) * width].astype(BF16)

    def tiles_body(it, carry):
        tiles = []
        for u in range(unroll):
            k = it * unroll + u
            toks = pl.ds(pl.multiple_of(k * tile, tile), tile)
            wins = _tile_windows(bnd_ref, pos_ref, b, k, toks, step, n_tiles, cap, width)
            scatter_windows(*_window_one_hots(wins, None, cap, width), toks)
            tiles.append((toks, wins))
        for toks, wins in tiles:
            def extra(w, c, toks=toks, wins=wins):
                scatter_windows(*_window_one_hots(wins, w, cap, width), toks)
                return c

            lax.fori_loop(1, _max_windows(wins), extra, 0)
        return carry

    lax.fori_loop(0, n_tiles // unroll, tiles_body, 0)


def _gather(bnd, pos, h2, cap):
    B, L, D = h2.shape
    E = N_EXPERTS
    tile = min(L, 256)
    width = min(cap, 64)
    return pl.pallas_call(
        functools.partial(_gather_kernel, cap=cap, tile=tile, width=width),
        grid_spec=pltpu.PrefetchScalarGridSpec(
            num_scalar_prefetch=1, grid=(B,),
            in_specs=[pl.BlockSpec((1, E, L), lambda b, s: (b, 0, 0)),
                      pl.BlockSpec((1, L, D), lambda b, s: (b, 0, 0))],
            out_specs=pl.BlockSpec((1, E, cap, D), lambda b, s: (b, 0, 0, 0))),
        out_shape=jax.ShapeDtypeStruct((B, E, cap, D), BF16),
        compiler_params=_params(("arbitrary",)),
        name="gather",
    )(bnd.reshape(-1), pos, h2)


def _expert_ffn(pos_ref, gate_ref, xs_ref, y_ref, wb_ref):
    n, _, cap, D = xs_ref.shape
    L = pos_ref.shape[3]
    slot = lax.broadcasted_iota(jnp.int32, (cap, L), 0)
    gates = [jnp.sum(jnp.where(slot == pos_ref[i, 0], gate_ref[i, 0], 0.0), axis=1, keepdims=True)
             for i in range(n)]
    gate = jnp.concatenate(gates, axis=0)
    xs = xs_ref[:, 0].reshape(n * cap, D)
    hid = _silu(jnp.dot(xs, wb_ref[0], preferred_element_type=F32))
    hid = (hid * jnp.dot(xs, wb_ref[1], preferred_element_type=F32)).astype(BF16)
    y = (jnp.dot(hid, wb_ref[2], preferred_element_type=F32) * gate).astype(BF16)
    y_ref[:, 0] = y.reshape(n, cap, D)


def _ffn_kernel(*refs, with_ctx):
    if with_ctx:
        (pos_ref, gate_ref, xs_ref, cpos_ref, cgate_ref, cxs_ref, w1_ref, w3_ref, w2_ref,
         y_ref, cy_ref, wb_ref) = refs
    else:
        pos_ref, gate_ref, xs_ref, w1_ref, w3_ref, w2_ref, y_ref, wb_ref = refs
    j = pl.program_id(1)
    n_lat = pl.num_programs(1) - (1 if with_ctx else 0)

    @pl.when(j == 0)
    def _():
        for i, w_ref in enumerate((w1_ref, w3_ref, w2_ref)):
            wb_ref[i] = w_ref[0, 0].astype(BF16)

    @pl.when(j < n_lat)
    def _():
        _expert_ffn(pos_ref, gate_ref, xs_ref, y_ref, wb_ref)

    if with_ctx:
        @pl.when(j == n_lat)
        def _():
            _expert_ffn(cpos_ref, cgate_ref, cxs_ref, cy_ref, wb_ref)


def _ffn(lat, ctx, w1, w3, w2, layer, bb):
    pos, gate, xs = lat
    B, E, cap, D = xs.shape
    L = pos.shape[2]
    bb = min(bb, B)
    nb = B // bb
    last = nb - 1
    with_ctx = ctx is not None
    lat_idx = lambda e, j: (jnp.minimum(j, last), e, 0, 0)
    ins = [pos.reshape(B, E, 1, L), gate.reshape(B, E, 1, L), xs]
    in_specs = [pl.BlockSpec((bb, 1, 1, L), lat_idx), pl.BlockSpec((bb, 1, 1, L), lat_idx),
                pl.BlockSpec((bb, 1, cap, D), lat_idx)]
    out_shape = [jax.ShapeDtypeStruct((B, E, cap, D), BF16)]
    out_specs = [pl.BlockSpec((bb, 1, cap, D), lat_idx)]
    if with_ctx:
        cpos, cgate, cxs = ctx
        capc, Lc = cxs.shape[2], cpos.shape[2]
        ctx_idx = lambda e, j: (0, e, 0, 0)
        ins += [cpos.reshape(B, E, 1, Lc), cgate.reshape(B, E, 1, Lc), cxs]
        in_specs += [pl.BlockSpec((B, 1, 1, Lc), ctx_idx), pl.BlockSpec((B, 1, 1, Lc), ctx_idx),
                     pl.BlockSpec((B, 1, capc, D), ctx_idx)]
        out_shape.append(jax.ShapeDtypeStruct((B, E, capc, D), BF16))
        out_specs.append(pl.BlockSpec((B, 1, capc, D), ctx_idx))
    wspec = pl.BlockSpec((1, 1, D, D), lambda e, j: (layer, e, 0, 0))
    outs = pl.pallas_call(
        functools.partial(_ffn_kernel, with_ctx=with_ctx),
        grid=(E, nb + (1 if with_ctx else 0)),
        in_specs=in_specs + [wspec, wspec, wspec],
        out_specs=out_specs, out_shape=out_shape,
        scratch_shapes=[pltpu.VMEM((3, D, D), BF16)],
        compiler_params=_params(("arbitrary", "arbitrary")),
        name="ffn",
    )(*ins, w1, w3, w2)
    return outs if with_ctx else (outs[0], None)


def _combine_kernel(bnd_ref, pos_ref, y_ref, x_ref, g_ref, *rest, cap, width, final_norm):
    nw_ref, out_ref = rest if final_norm else (None, rest[0])
    b = pl.program_id(0)
    L = pos_ref.shape[2]
    tile = min(COMB_TILE, out_ref.shape[1])
    step = tile // TOK_TILE
    n_tiles = L // tile
    tiles_per_step = out_ref.shape[1] // tile
    tn = (((0,), (0,)), ((), ()))

    def gathered_sum(m_all, starts):
        y_all = jnp.concatenate([y_ref[0, e, pl.ds(start, width), :] for e, start in enumerate(starts)],
                                axis=0)
        return lax.dot_general(m_all, y_all, tn, preferred_element_type=F32)

    tiles = []
    for i in range(tiles_per_step):
        t = pl.program_id(1) * tiles_per_step + i
        toks = pl.ds(pl.multiple_of(t * tile, tile), tile)
        rows = slice(i * tile, (i + 1) * tile)
        wins = _tile_windows(bnd_ref, pos_ref, b, t, toks, step, n_tiles, cap, width)
        out_ref[0, rows, :] = gathered_sum(*_window_one_hots(wins, None, cap, width))
        tiles.append((rows, wins))
    for rows, wins in tiles:
        def extra(w, c, rows=rows, wins=wins):
            out_ref[0, rows, :] += gathered_sum(*_window_one_hots(wins, w, cap, width))
            return c

        lax.fori_loop(1, _max_windows(wins), extra, 0)
    for rows, _ in tiles:
        x = x_ref[0, rows, :] + g_ref[0] * out_ref[0, rows, :]
        if final_norm:
            x = x * lax.rsqrt(jnp.mean(x * x, axis=-1, keepdims=True) + EPS) * nw_ref[...]
        out_ref[0, rows, :] = x


def _combine(bnd, pos, y, x, g, final_nw):
    B, E, cap, D = y.shape
    L = x.shape[1]
    width = min(cap, 64)
    tm = min(L, 512)
    final_norm = final_nw is not None
    tok = pl.BlockSpec((1, tm, D), lambda b, t, s: (b, t, 0))
    ins = [bnd.reshape(-1), pos, y, x, g]
    in_specs = [pl.BlockSpec((1, E, L), lambda b, t, s: (b, 0, 0)),
                pl.BlockSpec((1, E, cap, D), lambda b, t, s: (b, 0, 0, 0)),
                tok, pl.BlockSpec((1, 1, D), lambda b, t, s: (b, 0, 0))]
    if final_norm:
        ins.append(final_nw.reshape(1, D))
        in_specs.append(pl.BlockSpec((1, D), lambda b, t, s: (0, 0)))
    return pl.pallas_call(
        functools.partial(_combine_kernel, cap=cap, width=width, final_norm=final_norm),
        grid_spec=pltpu.PrefetchScalarGridSpec(
            num_scalar_prefetch=1, grid=(B, L // tm), in_specs=in_specs, out_specs=tok),
        out_shape=jax.ShapeDtypeStruct((B, L, D), F32),
        compiler_params=_params(("arbitrary", "arbitrary")),
        name="combine",
    )(*ins)


def _moe(lat, ctx, w1, w3, w2, layer, bb, final_nw=None):
    routed = []
    for lt, h2, x, g in (lat, ctx) if ctx is not None else (lat,):
        cap = EC_FACTOR * h2.shape[1] // N_EXPERTS
        pos, gate, bnd = _route(lt, cap)
        routed.append((pos, gate, _gather(bnd, pos, h2, cap), bnd, x, g))
    ys = _ffn(routed[0][:3], routed[1][:3] if ctx is not None else None, w1, w3, w2, layer, bb)
    outs = [_combine(bnd, pos, y, x, g, final_nw if i == 0 else None)
            for i, ((pos, _, _, bnd, x, g), y) in enumerate(zip(routed, ys))]
    return outs[0], (outs[1] if ctx is not None else None)


def _permute_w_in(w):
    cols = jnp.concatenate([w[:, 0:OFF_AF], w[:, OFF_R:OFF_SC], w[:, OFF_SC:D_IN], w[:, OFF_AF:OFF_R]],
                           axis=1)
    return jnp.pad(cols, ((0, 0), (0, P_TOTAL - D_IN))).astype(BF16)


def kernel(x, c, ctx, c_ctx, ada_w, ada_b, norm1_w, norm2_w, w_in, conf_dw_w, conf_dw_b, conf_ln_w,
           conf_ln_b, gla_wa_f, gla_ba_f, gla_wa_b, gla_ba_b, gla_gn_w, sc_w, sc_b, w_out, router_w,
           exp_w1, exp_w3, exp_w2, final_norm_w):
    B, L, D = x.shape
    Lc = ctx.shape[1]

    cond = jnp.concatenate([c, c_ctx[None, :], jnp.zeros((7, D), F32)], axis=0)
    mods = _ada(cond, ada_w, ada_b)

    xc = ctx
    zero_state = jnp.zeros((B, GLA_QK, GLA_DV), F32)
    for i in range(DEPTH):
        last = i == DEPTH - 1
        m_lat = [mods[i, :B, j * D:(j + 1) * D].reshape(B, 1, D) for j in range(6)]
        m_ctx = [jnp.broadcast_to(mods[i, B, j * D:(j + 1) * D].reshape(1, 1, D), (B, 1, D))
                 for j in range(6)]
        w_perm = _permute_w_in(w_in[i])
        wo = w_out[i].astype(BF16)
        rwt = router_w[i].T.astype(BF16)
        zero_wa = jnp.zeros((GLA_RANK, GLA_QK), F32)
        wa = jnp.block([[gla_wa_f[i], zero_wa], [zero_wa, gla_wa_b[i]]]).astype(BF16)
        ba = jnp.concatenate([gla_ba_f[i], gla_ba_b[i]]).reshape(1, 2 * GLA_QK)
        experts = (exp_w1, exp_w3, exp_w2, i)
        conv_w = (conf_dw_w[i], conf_dw_b[i], conf_ln_w[i], conf_ln_b[i], sc_w[i], sc_b[i])

        c_conf, c_qkv, c_r, c_sc, c_afab = _proj(xc, norm1_w[i], m_ctx[0], m_ctx[1], w_perm)
        c_og, c_sf, c_sb = _gla(c_qkv, c_afab, c_r, zero_state, zero_state, wa, ba, gla_gn_w[i])

        conf, qkv, r, scu, afab = _proj(x, norm1_w[i], m_lat[0], m_lat[1], w_perm)
        yc, ys = _convs(conf, scu, *conv_w, row_len=GRID_W)
        og, _, _ = _gla(qkv, afab, r, c_sf, c_sb, wa, ba, gla_gn_w[i])
        x, h2, lt = _outproj(yc, og, ys, x, wo, m_lat[2], norm2_w[i], m_lat[3], m_lat[4], rwt)
        moe_ctx = None
        if not last:
            c_yc, c_ys = _convs(c_conf, c_sc, *conv_w, row_len=Lc)
            xc, c_h2, c_lt = _outproj(c_yc, c_og, c_ys, xc, wo, m_ctx[2], norm2_w[i], m_ctx[3], m_ctx[4],
                                      rwt)
            moe_ctx = (c_lt, c_h2, xc, m_ctx[5])
        x, xc = _moe((lt, h2, x, m_lat[5]), moe_ctx, *experts, FFN_SAMPLES,
                     final_norm_w if last else None)
    return x
```

```python
import functools

import jax
import jax.numpy as jnp
from jax import lax
from jax.experimental import pallas as pl
from jax.experimental.pallas import tpu as pltpu

F32 = jnp.float32
BF16 = jnp.bfloat16
HI = lax.Precision.HIGHEST

D_MODEL = 1024
DEPTH = 2
GRID_W = 64
D_CONF = 256
D_SC = 256
GLA_HEADS = 4
GLA_DK = 64
GLA_DV = 128
GLA_QK = GLA_HEADS * GLA_DK
GLA_V = GLA_HEADS * GLA_DV
GLA_RANK = 16
GLA_TAU = 16.0
CONF_K = 31
SC_K = 3
N_EXPERTS = 16
EC_FACTOR = 2
EPS = 1e-6

OFF_Q = 2 * D_CONF
OFF_K = OFF_Q + GLA_QK
OFF_V = OFF_K + GLA_QK
OFF_AF = OFF_V + GLA_V
OFF_AB = OFF_AF + GLA_RANK
OFF_R = OFF_AB + GLA_RANK
OFF_SC = OFF_R + GLA_V
D_IN = OFF_SC + 3 * D_SC

P_CONF = 0
P_QKV = 512
P_R = 1536
P_SC = 2048
P_AFAB = 2816
P_TOTAL = 2944
AFAB_W = 128

CHUNK = 64
GLA_BLOCK = 8
TOKEN_BLOCK = 1024
CUMSUM_ROWS = 256
CONV_SUB = 64
CONV_TOKENS = 8
SUBLANES = 8
LANES = 128
CONV_PAD = 16
SC_PAD = 8
TOK_TILE = 128
COMB_TILE = 256
TILES_PER_ITER = 4
ROUTE_SAMPLES = 4
FFN_SAMPLES = 4
SLOT_ALIGN = 16
VMEM_LIMIT = 56 * 1024 * 1024


def _sigmoid(x):
    return 1.0 / (1.0 + jnp.exp(-x))


def _silu(x):
    return x * _sigmoid(x)


def _log_sigmoid(z):
    return jnp.minimum(z, 0.0) - jnp.log(1.0 + jnp.exp(-jnp.abs(z)))


def _params(sem):
    return pltpu.CompilerParams(dimension_semantics=sem, vmem_limit_bytes=VMEM_LIMIT)


def _ada_kernel(s_ref, w_ref, b_ref, o_ref):
    s = _silu(s_ref[...])
    o_ref[0] = jnp.dot(s, w_ref[0], precision=HI, preferred_element_type=F32) + b_ref[0]


def _ada(s_in, ada_w, ada_b):
    rows = s_in.shape[0]
    tn = 1024
    return pl.pallas_call(
        _ada_kernel,
        grid=(DEPTH, 6 * D_MODEL // tn),
        in_specs=[pl.BlockSpec((rows, D_MODEL), lambda l, n: (0, 0)),
                  pl.BlockSpec((1, D_MODEL, tn), lambda l, n: (l, 0, n)),
                  pl.BlockSpec((1, 1, tn), lambda l, n: (l, 0, n))],
        out_specs=pl.BlockSpec((1, rows, tn), lambda l, n: (l, 0, n)),
        out_shape=jax.ShapeDtypeStruct((DEPTH, rows, 6 * D_MODEL), F32),
        compiler_params=_params(("arbitrary", "arbitrary")),
        name="ada",
    )(s_in, ada_w, ada_b.reshape(DEPTH, 1, 6 * D_MODEL))


def _rms_mod(x, nw, shift, scale):
    ms = jnp.mean(x * x, axis=-1, keepdims=True)
    return (x * lax.rsqrt(ms + EPS) * nw) * (1.0 + scale) + shift


PROJ_GROUPS = {"conf": (P_CONF, P_QKV, BF16), "qkv": (P_QKV, P_R, BF16), "r": (P_R, P_SC, BF16),
               "sc": (P_SC, P_AFAB, BF16), "afab": (P_AFAB, P_TOTAL, F32)}


def _proj_kernel(x_ref, nw_ref, sh_ref, sc_ref, w_ref, *out_refs, groups):
    hb = _rms_mod(x_ref[0], nw_ref[...], sh_ref[0], sc_ref[0]).astype(BF16)
    for name, o_ref in zip(groups, out_refs):
        lo, hi, dt = PROJ_GROUPS[name]
        o_ref[0] = jnp.dot(hb, w_ref[:, lo:hi], preferred_element_type=F32).astype(dt)


def _proj(x, nw, shift, scale, w_perm, groups=tuple(PROJ_GROUPS)):
    B, L, D = x.shape
    tm = min(L, TOKEN_BLOCK)
    tok = lambda w: pl.BlockSpec((1, tm, w), lambda b, t: (b, t, 0))
    per_b = pl.BlockSpec((1, 1, D), lambda b, t: (b, 0, 0))
    widths = [(PROJ_GROUPS[g][1] - PROJ_GROUPS[g][0], PROJ_GROUPS[g][2]) for g in groups]
    return pl.pallas_call(
        functools.partial(_proj_kernel, groups=groups),
        grid=(B, L // tm),
        in_specs=[tok(D), pl.BlockSpec((1, D), lambda b, t: (0, 0)), per_b, per_b,
                  pl.BlockSpec((D, P_TOTAL), lambda b, t: (0, 0))],
        out_specs=[tok(w) for w, _ in widths],
        out_shape=[jax.ShapeDtypeStruct((B, L, w), dt) for w, dt in widths],
        compiler_params=_params(("parallel", "parallel")),
        name="proj",
    )(x, nw.reshape(1, D), shift, scale, w_perm)


def _conv_kernel(conf_ref, scu_ref, cw_ref, cb_ref, lnw_ref, lnb_ref, sw_ref, sb_ref,
                 yc_ref, ys_ref, yt_ref, ot_ref, pz_ref, *, row_len):
    L = conf_ref.shape[1]
    n_rows = L // row_len
    C = D_CONF
    S = SUBLANES

    split_row = n_rows == 1
    assert split_row or n_rows % S == 0
    t_out = row_len // S if split_row else row_len
    span = t_out + 2 * CONV_PAD

    halves = [slice(h * LANES, (h + 1) * LANES) for h in range(C // LANES)]

    def group(g, carry):
        for h in range(len(halves)):
            if not split_row:
                yt_ref[h, 0:CONV_PAD * S, :] = jnp.zeros((CONV_PAD * S, LANES), F32)
                yt_ref[h, (CONV_PAD + t_out) * S:, :] = jnp.zeros((CONV_PAD * S, LANES), F32)
        for k in range(S):
            if split_row:
                tok0 = k * t_out - CONV_PAD
                lo, hi = max(tok0, 0), min(tok0 + span, L)
                src = conf_ref[0, lo:hi, :]
                for a, b in ((0, lo - tok0), (hi - tok0, span)):
                    for h in range(len(halves)):
                        if b > a:
                            yt_ref[h, pl.ds(k + S * a, b - a, stride=S), :] = jnp.zeros((b - a, LANES), F32)
                first = lo - tok0
            else:
                src = conf_ref[0, pl.ds(pl.multiple_of((g * S + k) * row_len, row_len), row_len), :]
                first = CONV_PAD
            u = src.astype(F32)
            z = u[:, :C] * _sigmoid(u[:, C:])
            for h, ls in enumerate(halves):
                yt_ref[h, pl.ds(k + S * first, z.shape[0], stride=S), :] = z[:, ls]

        def out_rows(t):
            return pl.ds(pl.multiple_of(t * S, S), S)

        for h, ls in enumerate(halves):
            def taps(i, c, h=h, ls=ls):
                t0 = i * CONV_TOKENS
                accs = [None] * CONV_TOKENS
                for m in range(CONV_TOKENS + CONF_K - 1):
                    v = yt_ref[h, out_rows(t0 + CONV_PAD - CONF_K // 2 + m), :]
                    for tt in range(max(0, m - CONF_K + 1), min(CONV_TOKENS, m + 1)):
                        term = cw_ref[m - tt:m - tt + 1, ls] * v
                        accs[tt] = term if accs[tt] is None else accs[tt] + term
                for tt in range(CONV_TOKENS):
                    ot_ref[h, out_rows(t0 + tt), :] = accs[tt] + cb_ref[:, ls]
                return c

            lax.fori_loop(0, t_out // CONV_TOKENS, taps, 0)

        for k in range(S):
            dst = k * t_out if split_row else pl.multiple_of((g * S + k) * row_len, row_len)
            o = jnp.concatenate([ot_ref[h, pl.ds(k, t_out, stride=S), :] for h in range(len(halves))], axis=1)
            mu = jnp.mean(o, axis=-1, keepdims=True)
            cen = o - mu
            var = jnp.mean(cen * cen, axis=-1, keepdims=True)
            y = cen * lax.rsqrt(var + EPS) * lnw_ref[...] + lnb_ref[...]
            yc_ref[0, pl.ds(dst, t_out), :] = _silu(y).astype(BF16)
        return carry

    lax.fori_loop(0, 1 if split_row else n_rows // S, group, 0)

    n_blk = L // CONV_SUB
    pz_ref[0:SC_PAD, :] = jnp.zeros((SC_PAD, D_SC), F32)
    pz_ref[SC_PAD + L:, :] = jnp.zeros((SC_PAD, D_SC), F32)

    def fill_p(i, carry):
        start = pl.multiple_of(i * CONV_SUB, CONV_SUB)
        u = scu_ref[0, pl.ds(start, CONV_SUB), :].astype(F32)
        pz_ref[pl.ds(SC_PAD + start, CONV_SUB), :] = u[:, D_SC:2 * D_SC] * u[:, 2 * D_SC:]
        return carry

    lax.fori_loop(0, n_blk, fill_p, 0)

    def sc_blk(i, carry):
        start = pl.multiple_of(i * CONV_SUB, CONV_SUB)
        win = pz_ref[pl.ds(start, CONV_SUB + 2 * SC_PAD), :]
        acc = sb_ref[...] + sw_ref[0:1, :] * win[SC_PAD - 1:SC_PAD - 1 + CONV_SUB]
        acc = acc + sw_ref[1:2, :] * win[SC_PAD:SC_PAD + CONV_SUB]
        acc = acc + sw_ref[2:3, :] * win[SC_PAD + 1:SC_PAD + 1 + CONV_SUB]
        bg = scu_ref[0, pl.ds(start, CONV_SUB), 0:D_SC].astype(F32)
        ys_ref[0, pl.ds(start, CONV_SUB), :] = (bg * acc).astype(BF16)
        return carry

    lax.fori_loop(0, n_blk, sc_blk, 0)


def _convs(conf, scu, cw, cb, lnw, lnb, sw, sb, row_len):
    B, L, _ = conf.shape
    t_out = row_len // SUBLANES if L == row_len else row_len
    full = lambda a: pl.BlockSpec(a.shape, lambda b: (0,) * a.ndim)
    cw_p = jnp.zeros((32, D_CONF), F32).at[:CONF_K].set(cw)
    sw_p = jnp.zeros((8, D_SC), F32).at[:SC_K].set(sw)
    small = [cw_p, cb.reshape(1, D_CONF), lnw.reshape(1, D_CONF), lnb.reshape(1, D_CONF),
             sw_p, sb.reshape(1, D_SC)]
    return pl.pallas_call(
        functools.partial(_conv_kernel, row_len=row_len),
        grid=(B,),
        in_specs=[pl.BlockSpec((1, L, 2 * D_CONF), lambda b: (b, 0, 0)),
                  pl.BlockSpec((1, L, 3 * D_SC), lambda b: (b, 0, 0))] + [full(a) for a in small],
        out_specs=[pl.BlockSpec((1, L, D_CONF), lambda b: (b, 0, 0)),
                   pl.BlockSpec((1, L, D_SC), lambda b: (b, 0, 0))],
        out_shape=[jax.ShapeDtypeStruct((B, L, D_CONF), BF16),
                   jax.ShapeDtypeStruct((B, L, D_SC), BF16)],
        scratch_shapes=[pltpu.VMEM((D_CONF // LANES, (t_out + 2 * CONV_PAD) * SUBLANES, LANES), F32),
                        pltpu.VMEM((D_CONF // LANES, t_out * SUBLANES, LANES), F32),
                        pltpu.VMEM((L + 2 * SC_PAD, D_SC), F32)],
        compiler_params=_params(("parallel",)),
        name="convs",
    )(conf, scu, *small)


def _gla_kernel(qkv_ref, afab_ref, r_ref, s0f_ref, s0b_ref, wab_ref, bab_ref, gnw_ref,
                og_ref, sff_ref, sfb_ref, p_ref, qif_ref, qib_ref, spf_ref, ub_ref, gb_ref):
    L = qkv_ref.shape[1]
    C = CHUNK
    G = min(GLA_BLOCK, L // C)
    R = G * C
    n_blk = L // R
    mid = C // 2
    scale = GLA_DK ** -0.5
    nt = (((1,), (1,)), ((), ()))
    T = min(R, CUMSUM_ROWS)
    ii = lax.broadcasted_iota(jnp.int32, (T, T), 0)
    jj = lax.broadcasted_iota(jnp.int32, (T, T), 1)
    same = (ii // C) == (jj // C)
    tri_f = jnp.where(same, jnp.where(ii >= jj, 1.0, 0.0), 0.0).astype(BF16)
    tri_b = jnp.where(same, jnp.where(jj >= ii, 1.0, 0.0), 0.0).astype(BF16)
    PK = 2 * GLA_DK
    PV = 2 * GLA_DV
    ci = lax.broadcasted_iota(jnp.int32, (C, 2 * C), 0)
    cj = lax.broadcasted_iota(jnp.int32, (C, 2 * C), 1) % C
    lower = ci >= cj
    upper = cj >= ci
    kr = lax.broadcasted_iota(jnp.int32, (2 * C, PK), 0) // C
    kc = lax.broadcasted_iota(jnp.int32, (2 * C, PK), 1) // GLA_DK
    key_diag = kr == kc
    vr = lax.broadcasted_iota(jnp.int32, (2 * C, PV), 0) // C
    vc = lax.broadcasted_iota(jnp.int32, (2 * C, PV), 1) // GLA_DV
    val_diag = vr == vc

    def col_bcast(row_vec):
        return jnp.broadcast_to(row_vec, (GLA_DV, GLA_QK)).T

    def chunk_cumsum(tri, la):
        hi = la.astype(BF16)
        lo = (la - hi.astype(F32)).astype(BF16)
        parts = jnp.concatenate([hi, lo], axis=1)
        both = jnp.concatenate([jnp.dot(tri, parts[s * T:(s + 1) * T], preferred_element_type=F32)
                                for s in range(R // T)], axis=0)
        return both[:, :GLA_QK] + both[:, GLA_QK:]

    def v_pair(rows, p):
        return qkv_ref[0, rows, 2 * GLA_QK + p * PV:2 * GLA_QK + (p + 1) * PV]

    def pass1(i, s_f):
        r0 = pl.multiple_of(i * R, R)
        ab = afab_ref[0, pl.ds(r0, R), 0:2 * GLA_RANK].astype(BF16)
        z = jnp.dot(ab, wab_ref[...], preferred_element_type=F32) + bab_ref[...]
        la = _log_sigmoid(z) * (1.0 / GLA_TAU)
        b_f = chunk_cumsum(tri_f, la[:, :GLA_QK])
        b_b = chunk_cumsum(tri_b, la[:, GLA_QK:])
        q_all = qkv_ref[0, pl.ds(r0, R), 0:GLA_QK].astype(F32) * scale
        k_all = qkv_ref[0, pl.ds(r0, R), GLA_QK:2 * GLA_QK].astype(F32)
        for g in range(G):
            c = i * G + g
            rows = pl.ds(pl.multiple_of(r0 + g * C, C), C)
            sl = slice(g * C, (g + 1) * C)
            q, k = q_all[sl], k_all[sl]
            scaled = []
            for b, tot_row in ((b_f[sl], C - 1), (b_b[sl], 0)):
                ref_row = b[mid:mid + 1]
                tot = b[tot_row:tot_row + 1]
                q_rel = q * jnp.exp(b - ref_row)
                k_rel = k * jnp.exp(ref_row - b)
                scaled.append((q_rel.astype(BF16), k_rel.astype(BF16),
                               (q_rel * jnp.exp(ref_row)).astype(BF16),
                               (k_rel * jnp.exp(tot - ref_row)).T.astype(BF16),
                               tot))
            (qf, kf, qif, kuf_t, g_f), (qb, kb, qib, kub_t, g_b) = scaled
            qif_ref[rows, :] = qif
            qib_ref[rows, :] = qib
            p_parts, uf, ub = [], [], []
            for p in range(GLA_HEADS // 2):
                ks = slice(p * PK, (p + 1) * PK)
                kbd_f = jnp.where(key_diag, jnp.concatenate([kf[:, ks]] * 2, axis=0), 0)
                kbd_b = jnp.where(key_diag, jnp.concatenate([kb[:, ks]] * 2, axis=0), 0)
                s_fwd = lax.dot_general(qf[:, ks], kbd_f, nt, preferred_element_type=F32)
                s_bwd = lax.dot_general(qb[:, ks], kbd_b, nt, preferred_element_type=F32)
                p_parts.append((jnp.where(lower, s_fwd, 0.0) + jnp.where(upper, s_bwd, 0.0)).astype(BF16))
                u = jnp.dot(jnp.concatenate([kuf_t[ks, :], kub_t[ks, :]], axis=0), v_pair(rows, p),
                            preferred_element_type=F32)
                for blk, dst in ((u[:PK], uf), (u[PK:], ub)):
                    dst += [blk[:GLA_DK, :GLA_DV], blk[GLA_DK:, GLA_DV:]]
            p_ref[rows, :] = jnp.concatenate(p_parts, axis=1)
            ub_ref[c] = jnp.concatenate(ub, axis=0)
            gb_ref[c] = jnp.broadcast_to(g_b, (8, GLA_QK))
            spf_ref[c] = s_f.astype(BF16)
            s_f = col_bcast(jnp.exp(g_f)) * s_f + jnp.concatenate(uf, axis=0)
        return s_f

    sff_ref[0] = lax.fori_loop(0, n_blk, pass1, s0f_ref[0])

    def pass2(t, s_b):
        i = n_blk - 1 - t
        for g in reversed(range(G)):
            c = i * G + g
            rows = pl.ds(pl.multiple_of(i * R + g * C, C), C)
            spf = spf_ref[c]
            snb = s_b.astype(BF16)
            zero_blk = jnp.zeros((GLA_DK, GLA_DV), BF16)

            def pair_state(s, p):
                top = jnp.concatenate([s[2 * p * GLA_DK:(2 * p + 1) * GLA_DK], zero_blk], axis=1)
                bot = jnp.concatenate([zero_blk, s[(2 * p + 1) * GLA_DK:(2 * p + 2) * GLA_DK]], axis=1)
                return jnp.concatenate([top, bot], axis=0)

            outs = []
            for p in range(GLA_HEADS // 2):
                ks = slice(p * PK, (p + 1) * PK)
                vbd = jnp.where(val_diag, jnp.concatenate([v_pair(rows, p)] * 2, axis=0), 0)
                lhs = jnp.concatenate([p_ref[rows, ks], qif_ref[rows, ks], qib_ref[rows, ks]], axis=1)
                rhs = jnp.concatenate([vbd, pair_state(spf, p), pair_state(snb, p)], axis=0)
                o_pair = jnp.dot(lhs, rhs, preferred_element_type=F32)
                for o in (o_pair[:, :GLA_DV], o_pair[:, GLA_DV:]):
                    ms = jnp.mean(o * o, axis=-1, keepdims=True)
                    outs.append(o * lax.rsqrt(ms + EPS))
            o_all = jnp.concatenate(outs, axis=-1) * gnw_ref[...]
            og_ref[0, rows, :] = (o_all * _silu(r_ref[0, rows, :].astype(F32))).astype(BF16)
            s_b = col_bcast(jnp.exp(gb_ref[c][0:1])) * s_b + ub_ref[c]
        return s_b

    sfb_ref[0] = lax.fori_loop(0, n_blk, pass2, s0b_ref[0])


def _gla(qkv, afab, r, s0f, s0b, wab, bab, gnw):
    B, L, _ = qkv.shape
    n = L // CHUNK
    tok = lambda w: pl.BlockSpec((1, L, w), lambda b: (b, 0, 0))
    st = pl.BlockSpec((1, GLA_QK, GLA_DV), lambda b: (b, 0, 0))
    full = lambda a: pl.BlockSpec(a.shape, lambda b: (0,) * a.ndim)
    small = [wab, bab, gnw.reshape(1, GLA_V)]
    return pl.pallas_call(
        _gla_kernel,
        grid=(B,),
        in_specs=[tok(P_R - P_QKV), tok(AFAB_W), tok(GLA_V), st, st] + [full(a) for a in small],
        out_specs=[tok(GLA_V), st, st],
        out_shape=[jax.ShapeDtypeStruct((B, L, GLA_V), BF16),
                   jax.ShapeDtypeStruct((B, GLA_QK, GLA_DV), F32),
                   jax.ShapeDtypeStruct((B, GLA_QK, GLA_DV), F32)],
        scratch_shapes=[pltpu.VMEM((L, GLA_QK), BF16),
                        pltpu.VMEM((L, GLA_QK), BF16), pltpu.VMEM((L, GLA_QK), BF16),
                        pltpu.VMEM((n, GLA_QK, GLA_DV), BF16),
                        pltpu.VMEM((n, GLA_QK, GLA_DV), F32),
                        pltpu.VMEM((n, 8, GLA_QK), F32)],
        compiler_params=_params(("parallel",)),
        name="gla",
    )(qkv, afab, r, s0f, s0b, *small)


def _outproj_kernel(yc_ref, og_ref, ys_ref, x_ref, wo_ref, g2_ref, nw_ref, sh_ref, sc_ref, rwt_ref,
                    xo_ref, h2_ref, lt_ref):
    y = jnp.dot(yc_ref[0], wo_ref[0:D_CONF, :], preferred_element_type=F32)
    y = y + jnp.dot(og_ref[0], wo_ref[D_CONF:D_CONF + GLA_V, :], preferred_element_type=F32)
    y = y + jnp.dot(ys_ref[0], wo_ref[D_CONF + GLA_V:, :], preferred_element_type=F32)
    x = x_ref[0] + g2_ref[0] * y
    xo_ref[0] = x
    hb = _rms_mod(x, nw_ref[...], sh_ref[0], sc_ref[0]).astype(BF16)
    h2_ref[0] = hb
    lt_ref[0] = lax.dot_general(rwt_ref[...], hb, (((1,), (1,)), ((), ())),
                                preferred_element_type=F32)


def _outproj(yc, og, ys, x, wo, g2, nw, shift, scale, rwt):
    B, L, D = x.shape
    tm = min(L, TOKEN_BLOCK)
    tok = lambda w: pl.BlockSpec((1, tm, w), lambda b, t: (b, t, 0))
    per_b = pl.BlockSpec((1, 1, D), lambda b, t: (b, 0, 0))
    return pl.pallas_call(
        _outproj_kernel,
        grid=(B, L // tm),
        in_specs=[tok(D_CONF), tok(GLA_V), tok(D_SC), tok(D),
                  pl.BlockSpec((D, D), lambda b, t: (0, 0)), per_b,
                  pl.BlockSpec((1, D), lambda b, t: (0, 0)), per_b, per_b,
                  pl.BlockSpec((N_EXPERTS, D), lambda b, t: (0, 0))],
        out_specs=[tok(D), tok(D), pl.BlockSpec((1, N_EXPERTS, tm), lambda b, t: (b, 0, t))],
        out_shape=[jax.ShapeDtypeStruct((B, L, D), F32), jax.ShapeDtypeStruct((B, L, D), BF16),
                   jax.ShapeDtypeStruct((B, N_EXPERTS, L), F32)],
        compiler_params=_params(("parallel", "parallel")),
        name="outproj",
    )(yc, og, ys, x, wo, g2, nw.reshape(1, D), shift, scale, rwt)


def _lane_cumsum(x):
    n = x.shape[-1]
    lane = lax.broadcasted_iota(jnp.int32, x.shape, x.ndim - 1)
    s = 1
    while s < n:
        x = x + jnp.where(lane >= s, pltpu.roll(x, s, axis=x.ndim - 1), 0)
        s *= 2
    return x


def _route_kernel(lt_ref, pos_ref, gate_ref, bnd_ref, *, cap):
    bs, E, L = lt_ref.shape
    lt = lt_ref[...]
    e = jnp.exp(lt - jnp.max(lt, axis=1, keepdims=True))
    aff = (e / jnp.sum(e, axis=1, keepdims=True)).reshape(bs * E, L)

    def search(i, t):
        cand = t | lax.shift_left(jnp.int32(1), 30 - i)
        cnt = jnp.sum((aff >= pltpu.bitcast(cand, F32)).astype(jnp.int32), axis=1, keepdims=True)
        return jnp.where(cnt >= cap, cand, t)

    thr_bits = lax.fori_loop(0, 31, search, jnp.zeros((bs * E, 1), jnp.int32))
    thr = pltpu.bitcast(thr_bits, F32)
    gt = aff > thr
    eq = aff == thr
    need = cap - jnp.sum(gt.astype(jnp.int32), axis=1, keepdims=True)
    eq_i = eq.astype(jnp.int32)
    eq_rank = _lane_cumsum(eq_i) - eq_i
    sel = gt | (eq & (eq_rank < need))
    sel_i = sel.astype(jnp.int32)
    slot = _lane_cumsum(sel_i) - sel_i
    pos_ref[...] = jnp.where(sel, slot, -1).reshape(bs, E, L)
    gate_ref[...] = aff.reshape(bs, E, L)
    bnd = jnp.concatenate([slot[:, k * TOK_TILE:k * TOK_TILE + 1] for k in range(L // TOK_TILE)], axis=1)
    bnd_ref[...] = bnd.reshape(bs, E, L // TOK_TILE)


def _route(lt, cap):
    B, E, L = lt.shape
    bs = min(B, ROUTE_SAMPLES)
    spec = pl.BlockSpec((bs, E, L), lambda b: (b, 0, 0))
    nt = L // TOK_TILE
    return pl.pallas_call(
        functools.partial(_route_kernel, cap=cap),
        grid=(B // bs,),
        in_specs=[spec], out_specs=[spec, spec, pl.BlockSpec((bs, E, nt), lambda b: (b, 0, 0))],
        out_shape=[jax.ShapeDtypeStruct((B, E, L), jnp.int32), jax.ShapeDtypeStruct((B, E, L), F32),
                   jax.ShapeDtypeStruct((B, E, nt), jnp.int32)],
        compiler_params=_params(("parallel",)),
        name="route",
    )(lt)


def _window(lo, hi, width, cap):
    a0 = jnp.minimum((lo // SLOT_ALIGN) * SLOT_ALIGN, cap - width)
    n = jnp.where(hi > lo, (hi - a0 + width - 1) // width, 0)
    return a0, n


def _one_hot_rows(pos_row, start, width, first_row=None):
    slot = lax.broadcasted_iota(jnp.int32, (width, pos_row.shape[1]), 0) + start
    on = 1.0 if first_row is None else jnp.where(slot >= first_row, 1.0, 0.0)
    return jnp.where(slot == pos_row, on, 0.0).astype(BF16)


def _tile_windows(bnd_ref, pos_ref, b, t, toks, step, n_tiles, cap, width):
    nt = pos_ref.shape[2] // TOK_TILE
    wins = []
    for e in range(N_EXPERTS):
        base = (b * N_EXPERTS + e) * nt
        lo = bnd_ref[base + t * step]
        hi = jnp.where(t + 1 < n_tiles, bnd_ref[base + jnp.minimum(t + 1, n_tiles - 1) * step], cap)
        a0, n = _window(lo, hi, width, cap)
        wins.append((pl.multiple_of(a0, SLOT_ALIGN), n, pos_ref[0, e:e + 1, toks]))
    return wins


def _window_one_hots(wins, w, cap, width):
    ms, starts = [], []
    for a0, n, pos_row in wins:
        if w is None:
            start, first = a0, None
        else:
            first = a0 + w * width
            start = pl.multiple_of(jnp.minimum(first, cap - width), SLOT_ALIGN)
            first = jnp.where(w < n, first, cap)
        ms.append(_one_hot_rows(pos_row, start, width, first))
        starts.append(start)
    return jnp.concatenate(ms, axis=0), starts


def _max_windows(wins):
    return functools.reduce(jnp.maximum, [n for _, n, _ in wins])


def _gather_kernel(bnd_ref, pos_ref, h_ref, xs_ref, *, cap, tile, width):
    b = pl.program_id(0)
    L = h_ref.shape[1]
    step = tile // TOK_TILE
    n_tiles = L // tile
    unroll = min(TILES_PER_ITER, n_tiles)
    xs_ref[...] = jnp.zeros_like(xs_ref)

    def scatter_windows(m_all, starts, toks):
        part = jnp.dot(m_all, h_ref[0, toks, :], preferred_element_type=F32)
        for e, start in enumerate(starts):
            xs_ref[0, e, pl.ds(start, width), :] += part[e * width:(e + 1) * width].astype(BF16)

    def tiles_body(it, carry):
        tiles = []
        for u in range(unroll):
            k = it * unroll + u
            toks = pl.ds(pl.multiple_of(k * tile, tile), tile)
            wins = _tile_windows(bnd_ref, pos_ref, b, k, toks, step, n_tiles, cap, width)
            scatter_windows(*_window_one_hots(wins, None, cap, width), toks)
            tiles.append((toks, wins))
        for toks, wins in tiles:
            def extra(w, c, toks=toks, wins=wins):
                scatter_windows(*_window_one_hots(wins, w, cap, width), toks)
                return c

            lax.fori_loop(1, _max_windows(wins), extra, 0)
        return carry

    lax.fori_loop(0, n_tiles // unroll, tiles_body, 0)


def _gather(bnd, pos, h2, cap):
    B, L, D = h2.shape
    E = N_EXPERTS
    tile = min(L, 256)
    width = min(cap, 64)
    return pl.pallas_call(
        functools.partial(_gather_kernel, cap=cap, tile=tile, width=width),
        grid_spec=pltpu.PrefetchScalarGridSpec(
            num_scalar_prefetch=1, grid=(B,),
            in_specs=[pl.BlockSpec((1, E, L), lambda b, s: (b, 0, 0)),
                      pl.BlockSpec((1, L, D), lambda b, s: (b, 0, 0))],
            out_specs=pl.BlockSpec((1, E, cap, D), lambda b, s: (b, 0, 0, 0))),
        out_shape=jax.ShapeDtypeStruct((B, E, cap, D), BF16),
        compiler_params=_params(("arbitrary",)),
        name="gather",
    )(bnd.reshape(-1), pos, h2)


def _expert_ffn(pos_ref, gate_ref, xs_ref, y_ref, wb_ref):
    n, _, cap, D = xs_ref.shape
    L = pos_ref.shape[3]
    slot = lax.broadcasted_iota(jnp.int32, (cap, L), 0)
    gates = [jnp.sum(jnp.where(slot == pos_ref[i, 0], gate_ref[i, 0], 0.0), axis=1, keepdims=True)
             for i in range(n)]
    gate = jnp.concatenate(gates, axis=0)
    xs = xs_ref[:, 0].reshape(n * cap, D)
    hid = _silu(jnp.dot(xs, wb_ref[0], preferred_element_type=F32))
    hid = (hid * jnp.dot(xs, wb_ref[1], preferred_element_type=F32)).astype(BF16)
    y = (jnp.dot(hid, wb_ref[2], preferred_element_type=F32) * gate).astype(BF16)
    y_ref[:, 0] = y.reshape(n, cap, D)


def _ffn_kernel(*refs, with_ctx):
    if with_ctx:
        (pos_ref, gate_ref, xs_ref, cpos_ref, cgate_ref, cxs_ref, w1_ref, w3_ref, w2_ref,
         y_ref, cy_ref, wb_ref) = refs
    else:
        pos_ref, gate_ref, xs_ref, w1_ref, w3_ref, w2_ref, y_ref, wb_ref = refs
    j = pl.program_id(1)

    @pl.when(j == 0)
    def _():
        for i, w_ref in enumerate((w1_ref, w3_ref, w2_ref)):
            wb_ref[i] = w_ref[0, 0].astype(BF16)

    if with_ctx:
        @pl.when(j == 0)
        def _():
            _expert_ffn(cpos_ref, cgate_ref, cxs_ref, cy_ref, wb_ref)

    @pl.when(j >= (1 if with_ctx else 0))
    def _():
        _expert_ffn(pos_ref, gate_ref, xs_ref, y_ref, wb_ref)


def _ffn(lat, ctx, w1, w3, w2, layer, bb):
    pos, gate, xs = lat
    B, E, cap, D = xs.shape
    L = pos.shape[2]
    bb = min(bb, B)
    nb = B // bb
    with_ctx = ctx is not None
    first_lat = 1 if with_ctx else 0
    lat_idx = lambda e, j: (jnp.maximum(j - first_lat, 0), e, 0, 0)
    ins = [pos.reshape(B, E, 1, L), gate.reshape(B, E, 1, L), xs]
    in_specs = [pl.BlockSpec((bb, 1, 1, L), lat_idx), pl.BlockSpec((bb, 1, 1, L), lat_idx),
                pl.BlockSpec((bb, 1, cap, D), lat_idx)]
    out_shape = [jax.ShapeDtypeStruct((B, E, cap, D), BF16)]
    out_specs = [pl.BlockSpec((bb, 1, cap, D), lat_idx)]
    if with_ctx:
        cpos, cgate, cxs = ctx
        capc, Lc = cxs.shape[2], cpos.shape[2]
        ctx_idx = lambda e, j: (0, e, 0, 0)
        ins += [cpos.reshape(B, E, 1, Lc), cgate.reshape(B, E, 1, Lc), cxs]
        in_specs += [pl.BlockSpec((B, 1, 1, Lc), ctx_idx), pl.BlockSpec((B, 1, 1, Lc), ctx_idx),
                     pl.BlockSpec((B, 1, capc, D), ctx_idx)]
        out_shape.append(jax.ShapeDtypeStruct((B, E, capc, D), BF16))
        out_specs.append(pl.BlockSpec((B, 1, capc, D), ctx_idx))
    wspec = pl.BlockSpec((1, 1, D, D), lambda e, j: (layer, e, 0, 0))
    outs = pl.pallas_call(
        functools.partial(_ffn_kernel, with_ctx=with_ctx),
        grid=(E, nb + (1 if with_ctx else 0)),
        in_specs=in_specs + [wspec, wspec, wspec],
        out_specs=out_specs, out_shape=out_shape,
        scratch_shapes=[pltpu.VMEM((3, D, D), BF16)],
        compiler_params=_params(("arbitrary", "arbitrary")),
        name="ffn",
    )(*ins, w1, w3, w2)
    return outs if with_ctx else (outs[0], None)


def _combine_kernel(bnd_ref, pos_ref, y_ref, x_ref, g_ref, *rest, cap, width, final_norm):
    nw_ref, out_ref = rest if final_norm else (None, rest[0])
    b = pl.program_id(0)
    L = pos_ref.shape[2]
    tile = min(COMB_TILE, out_ref.shape[1])
    step = tile // TOK_TILE
    n_tiles = L // tile
    tiles_per_step = out_ref.shape[1] // tile
    tn = (((0,), (0,)), ((), ()))

    def gathered_sum(m_all, starts):
        y_all = jnp.concatenate([y_ref[0, e, pl.ds(start, width), :] for e, start in enumerate(starts)],
                                axis=0)
        return lax.dot_general(m_all, y_all, tn, preferred_element_type=F32)

    tiles = []
    for i in range(tiles_per_step):
        t = pl.program_id(1) * tiles_per_step + i
        toks = pl.ds(pl.multiple_of(t * tile, tile), tile)
        rows = slice(i * tile, (i + 1) * tile)
        wins = _tile_windows(bnd_ref, pos_ref, b, t, toks, step, n_tiles, cap, width)
        out_ref[0, rows, :] = gathered_sum(*_window_one_hots(wins, None, cap, width))
        tiles.append((rows, wins))
    for rows, wins in tiles:
        def extra(w, c, rows=rows, wins=wins):
            out_ref[0, rows, :] += gathered_sum(*_window_one_hots(wins, w, cap, width))
            return c

        lax.fori_loop(1, _max_windows(wins), extra, 0)
    for rows, _ in tiles:
        x = x_ref[0, rows, :] + g_ref[0] * out_ref[0, rows, :]
        if final_norm:
            x = x * lax.rsqrt(jnp.mean(x * x, axis=-1, keepdims=True) + EPS) * nw_ref[...]
        out_ref[0, rows, :] = x


def _combine(bnd, pos, y, x, g, final_nw):
    B, E, cap, D = y.shape
    L = x.shape[1]
    width = min(cap, 64)
    tm = min(L, TOKEN_BLOCK)
    final_norm = final_nw is not None
    tok = pl.BlockSpec((1, tm, D), lambda b, t, s: (b, t, 0))
    ins = [bnd.reshape(-1), pos, y, x, g]
    in_specs = [pl.BlockSpec((1, E, L), lambda b, t, s: (b, 0, 0)),
                pl.BlockSpec((1, E, cap, D), lambda b, t, s: (b, 0, 0, 0)),
                tok, pl.BlockSpec((1, 1, D), lambda b, t, s: (b, 0, 0))]
    if final_norm:
        ins.append(final_nw.reshape(1, D))
        in_specs.append(pl.BlockSpec((1, D), lambda b, t, s: (0, 0)))
    return pl.pallas_call(
        functools.partial(_combine_kernel, cap=cap, width=width, final_norm=final_norm),
        grid_spec=pltpu.PrefetchScalarGridSpec(
            num_scalar_prefetch=1, grid=(B, L // tm), in_specs=in_specs, out_specs=tok),
        out_shape=jax.ShapeDtypeStruct((B, L, D), F32),
        compiler_params=_params(("arbitrary", "arbitrary")),
        name="combine",
    )(*ins)


def _moe(lat, ctx, w1, w3, w2, layer, bb, final_nw=None):
    routed = []
    for lt, h2, x, g in (lat, ctx) if ctx is not None else (lat,):
        cap = EC_FACTOR * h2.shape[1] // N_EXPERTS
        pos, gate, bnd = _route(lt, cap)
        routed.append((pos, gate, _gather(bnd, pos, h2, cap), bnd, x, g))
    ys = _ffn(routed[0][:3], routed[1][:3] if ctx is not None else None, w1, w3, w2, layer, bb)
    outs = [_combine(bnd, pos, y, x, g, final_nw if i == 0 else None)
            for i, ((pos, _, _, bnd, x, g), y) in enumerate(zip(routed, ys))]
    return outs[0], (outs[1] if ctx is not None else None)


def _permute_w_in(w):
    cols = jnp.concatenate([w[:, 0:OFF_AF], w[:, OFF_R:OFF_SC], w[:, OFF_SC:D_IN], w[:, OFF_AF:OFF_R]],
                           axis=1)
    return jnp.pad(cols, ((0, 0), (0, P_TOTAL - D_IN))).astype(BF16)


def kernel(x, c, ctx, c_ctx, ada_w, ada_b, norm1_w, norm2_w, w_in, conf_dw_w, conf_dw_b, conf_ln_w,
           conf_ln_b, gla_wa_f, gla_ba_f, gla_wa_b, gla_ba_b, gla_gn_w, sc_w, sc_b, w_out, router_w,
           exp_w1, exp_w3, exp_w2, final_norm_w):
    B, L, D = x.shape
    Lc = ctx.shape[1]

    cond = jnp.concatenate([c, c_ctx[None, :], jnp.zeros((7, D), F32)], axis=0)
    mods = _ada(cond, ada_w, ada_b)

    xc = ctx
    zero_state = jnp.zeros((B, GLA_QK, GLA_DV), F32)
    for i in range(DEPTH):
        last = i == DEPTH - 1
        m_lat = [mods[i, :B, j * D:(j + 1) * D].reshape(B, 1, D) for j in range(6)]
        m_ctx = [jnp.broadcast_to(mods[i, B, j * D:(j + 1) * D].reshape(1, 1, D), (B, 1, D))
                 for j in range(6)]
        w_perm = _permute_w_in(w_in[i])
        wo = w_out[i].astype(BF16)
        rwt = router_w[i].T.astype(BF16)
        zero_wa = jnp.zeros((GLA_RANK, GLA_QK), F32)
        wa = jnp.block([[gla_wa_f[i], zero_wa], [zero_wa, gla_wa_b[i]]]).astype(BF16)
        ba = jnp.concatenate([gla_ba_f[i], gla_ba_b[i]]).reshape(1, 2 * GLA_QK)
        experts = (exp_w1, exp_w3, exp_w2, i)
        conv_w = (conf_dw_w[i], conf_dw_b[i], conf_ln_w[i], conf_ln_b[i], sc_w[i], sc_b[i])

        if last:
            c_qkv, c_r, c_afab = _proj(xc, norm1_w[i], m_ctx[0], m_ctx[1], w_perm, ("qkv", "r", "afab"))
        else:
            c_conf, c_qkv, c_r, c_sc, c_afab = _proj(xc, norm1_w[i], m_ctx[0], m_ctx[1], w_perm)
        c_og, c_sf, c_sb = _gla(c_qkv, c_afab, c_r, zero_state, zero_state, wa, ba, gla_gn_w[i])

        conf, qkv, r, scu, afab = _proj(x, norm1_w[i], m_lat[0], m_lat[1], w_perm)
        yc, ys = _convs(conf, scu, *conv_w, row_len=GRID_W)
        og, _, _ = _gla(qkv, afab, r, c_sf, c_sb, wa, ba, gla_gn_w[i])
        x, h2, lt = _outproj(yc, og, ys, x, wo, m_lat[2], norm2_w[i], m_lat[3], m_lat[4], rwt)
        moe_ctx = None
        if not last:
            c_yc, c_ys = _convs(c_conf, c_sc, *conv_w, row_len=Lc)
            xc, c_h2, c_lt = _outproj(c_yc, c_og, c_ys, xc, wo, m_ctx[2], norm2_w[i], m_ctx[3], m_ctx[4],
                                      rwt)
            moe_ctx = (c_lt, c_h2, xc, m_ctx[5])
        x, xc = _moe((lt, h2, x, m_lat[5]), moe_ctx, *experts, FFN_SAMPLES,
                     final_norm_w if last else None)
    return x
```

```python
import functools

import jax
import jax.numpy as jnp
from jax import lax
from jax.experimental import pallas as pl
from jax.experimental.pallas import tpu as pltpu

F32 = jnp.float32
BF16 = jnp.bfloat16
HI = lax.Precision.HIGHEST

D_MODEL = 1024
DEPTH = 2
GRID_W = 64
D_CONF = 256
D_SC = 256
GLA_HEADS = 4
GLA_DK = 64
GLA_DV = 128
GLA_QK = GLA_HEADS * GLA_DK
GLA_V = GLA_HEADS * GLA_DV
GLA_RANK = 16
GLA_TAU = 16.0
CONF_K = 31
SC_K = 3
N_EXPERTS = 16
EC_FACTOR = 2
EPS = 1e-6

OFF_Q = 2 * D_CONF
OFF_K = OFF_Q + GLA_QK
OFF_V = OFF_K + GLA_QK
OFF_AF = OFF_V + GLA_V
OFF_AB = OFF_AF + GLA_RANK
OFF_R = OFF_AB + GLA_RANK
OFF_SC = OFF_R + GLA_V
D_IN = OFF_SC + 3 * D_SC

P_CONF = 0
P_QKV = 512
P_R = 1536
P_SC = 2048
P_AFAB = 2816
P_TOTAL = 2944
AFAB_W = 128

CHUNK = 64
GLA_BLOCK = 8
TOKEN_BLOCK = 1024
CUMSUM_ROWS = 256
CONV_SUB = 64
CONV_TOKENS = 8
SUBLANES = 8
LANES = 128
CONV_PAD = 16
SC_PAD = 8
TOK_TILE = 128
COMB_TILE = 256
TILES_PER_ITER = 4
ROUTE_SAMPLES = 8
FFN_SAMPLES = 4
SLOT_ALIGN = 16
VMEM_LIMIT = 56 * 1024 * 1024


def _sigmoid(x):
    return 1.0 / (1.0 + jnp.exp(-x))


def _silu(x):
    return x * _sigmoid(x)


def _log_sigmoid(z):
    return jnp.minimum(z, 0.0) - jnp.log(1.0 + jnp.exp(-jnp.abs(z)))


def _params(sem):
    return pltpu.CompilerParams(dimension_semantics=sem, vmem_limit_bytes=VMEM_LIMIT)


def _ada_kernel(s_ref, w_ref, b_ref, o_ref):
    s = _silu(s_ref[...])
    o_ref[0] = jnp.dot(s, w_ref[0], precision=HI, preferred_element_type=F32) + b_ref[0]


def _ada(s_in, ada_w, ada_b):
    rows = s_in.shape[0]
    tn = 1024
    return pl.pallas_call(
        _ada_kernel,
        grid=(DEPTH, 6 * D_MODEL // tn),
        in_specs=[pl.BlockSpec((rows, D_MODEL), lambda l, n: (0, 0)),
                  pl.BlockSpec((1, D_MODEL, tn), lambda l, n: (l, 0, n)),
                  pl.BlockSpec((1, 1, tn), lambda l, n: (l, 0, n))],
        out_specs=pl.BlockSpec((1, rows, tn), lambda l, n: (l, 0, n)),
        out_shape=jax.ShapeDtypeStruct((DEPTH, rows, 6 * D_MODEL), F32),
        compiler_params=_params(("arbitrary", "arbitrary")),
        name="ada",
    )(s_in, ada_w, ada_b.reshape(DEPTH, 1, 6 * D_MODEL))


def _rms_mod(x, nw, shift, scale):
    ms = jnp.mean(x * x, axis=-1, keepdims=True)
    return (x * lax.rsqrt(ms + EPS) * nw) * (1.0 + scale) + shift


PROJ_GROUPS = {"conf": (P_CONF, P_QKV, BF16), "qkv": (P_QKV, P_R, BF16), "r": (P_R, P_SC, BF16),
               "sc": (P_SC, P_AFAB, BF16), "afab": (P_AFAB, P_TOTAL, F32)}


def _proj_kernel(x_ref, nw_ref, sh_ref, sc_ref, w_ref, *out_refs, groups):
    hb = _rms_mod(x_ref[0], nw_ref[...], sh_ref[0], sc_ref[0]).astype(BF16)
    for name, o_ref in zip(groups, out_refs):
        lo, hi, dt = PROJ_GROUPS[name]
        o_ref[0] = jnp.dot(hb, w_ref[:, lo:hi], preferred_element_type=F32).astype(dt)


def _proj(x, nw, shift, scale, w_perm, groups=tuple(PROJ_GROUPS)):
    B, L, D = x.shape
    tm = min(L, TOKEN_BLOCK)
    tok = lambda w: pl.BlockSpec((1, tm, w), lambda b, t: (b, t, 0))
    per_b = pl.BlockSpec((1, 1, D), lambda b, t: (b, 0, 0))
    widths = [(PROJ_GROUPS[g][1] - PROJ_GROUPS[g][0], PROJ_GROUPS[g][2]) for g in groups]
    return pl.pallas_call(
        functools.partial(_proj_kernel, groups=groups),
        grid=(B, L // tm),
        in_specs=[tok(D), pl.BlockSpec((1, D), lambda b, t: (0, 0)), per_b, per_b,
                  pl.BlockSpec((D, P_TOTAL), lambda b, t: (0, 0))],
        out_specs=[tok(w) for w, _ in widths],
        out_shape=[jax.ShapeDtypeStruct((B, L, w), dt) for w, dt in widths],
        compiler_params=_params(("parallel", "parallel")),
        name="proj",
    )(x, nw.reshape(1, D), shift, scale, w_perm)


def _conv_kernel(conf_ref, scu_ref, cw_ref, cb_ref, lnw_ref, lnb_ref, sw_ref, sb_ref,
                 yc_ref, ys_ref, yt_ref, ot_ref, pz_ref, *, row_len):
    L = conf_ref.shape[1]
    n_rows = L // row_len
    C = D_CONF
    S = SUBLANES

    split_row = n_rows == 1
    assert split_row or n_rows % S == 0
    t_out = row_len // S if split_row else row_len
    span = t_out + 2 * CONV_PAD

    halves = [slice(h * LANES, (h + 1) * LANES) for h in range(C // LANES)]

    def group(g, carry):
        for h in range(len(halves)):
            if not split_row:
                yt_ref[h, 0:CONV_PAD * S, :] = jnp.zeros((CONV_PAD * S, LANES), F32)
                yt_ref[h, (CONV_PAD + t_out) * S:, :] = jnp.zeros((CONV_PAD * S, LANES), F32)
        for k in range(S):
            if split_row:
                tok0 = k * t_out - CONV_PAD
                lo, hi = max(tok0, 0), min(tok0 + span, L)
                src = conf_ref[0, lo:hi, :]
                for a, b in ((0, lo - tok0), (hi - tok0, span)):
                    for h in range(len(halves)):
                        if b > a:
                            yt_ref[h, pl.ds(k + S * a, b - a, stride=S), :] = jnp.zeros((b - a, LANES), F32)
                first = lo - tok0
            else:
                src = conf_ref[0, pl.ds(pl.multiple_of((g * S + k) * row_len, row_len), row_len), :]
                first = CONV_PAD
            u = src.astype(F32)
            z = u[:, :C] * _sigmoid(u[:, C:])
            for h, ls in enumerate(halves):
                yt_ref[h, pl.ds(k + S * first, z.shape[0], stride=S), :] = z[:, ls]

        def out_rows(t):
            return pl.ds(pl.multiple_of(t * S, S), S)

        for h, ls in enumerate(halves):
            def taps(i, c, h=h, ls=ls):
                t0 = i * CONV_TOKENS
                accs = [None] * CONV_TOKENS
                for m in range(CONV_TOKENS + CONF_K - 1):
                    v = yt_ref[h, out_rows(t0 + CONV_PAD - CONF_K // 2 + m), :]
                    for tt in range(max(0, m - CONF_K + 1), min(CONV_TOKENS, m + 1)):
                        term = cw_ref[m - tt:m - tt + 1, ls] * v
                        accs[tt] = term if accs[tt] is None else accs[tt] + term
                for tt in range(CONV_TOKENS):
                    ot_ref[h, out_rows(t0 + tt), :] = accs[tt] + cb_ref[:, ls]
                return c

            lax.fori_loop(0, t_out // CONV_TOKENS, taps, 0)

        for k in range(S):
            dst = k * t_out if split_row else pl.multiple_of((g * S + k) * row_len, row_len)
            o = jnp.concatenate([ot_ref[h, pl.ds(k, t_out, stride=S), :] for h in range(len(halves))], axis=1)
            mu = jnp.mean(o, axis=-1, keepdims=True)
            cen = o - mu
            var = jnp.mean(cen * cen, axis=-1, keepdims=True)
            y = cen * lax.rsqrt(var + EPS) * lnw_ref[...] + lnb_ref[...]
            yc_ref[0, pl.ds(dst, t_out), :] = _silu(y).astype(BF16)
        return carry

    lax.fori_loop(0, 1 if split_row else n_rows // S, group, 0)

    n_blk = L // CONV_SUB
    pz_ref[0:SC_PAD, :] = jnp.zeros((SC_PAD, D_SC), F32)
    pz_ref[SC_PAD + L:, :] = jnp.zeros((SC_PAD, D_SC), F32)

    def fill_p(i, carry):
        start = pl.multiple_of(i * CONV_SUB, CONV_SUB)
        u = scu_ref[0, pl.ds(start, CONV_SUB), :].astype(F32)
        pz_ref[pl.ds(SC_PAD + start, CONV_SUB), :] = u[:, D_SC:2 * D_SC] * u[:, 2 * D_SC:]
        return carry

    lax.fori_loop(0, n_blk, fill_p, 0)

    def sc_blk(i, carry):
        start = pl.multiple_of(i * CONV_SUB, CONV_SUB)
        win = pz_ref[pl.ds(start, CONV_SUB + 2 * SC_PAD), :]
        acc = sb_ref[...] + sw_ref[0:1, :] * win[SC_PAD - 1:SC_PAD - 1 + CONV_SUB]
        acc = acc + sw_ref[1:2, :] * win[SC_PAD:SC_PAD + CONV_SUB]
        acc = acc + sw_ref[2:3, :] * win[SC_PAD + 1:SC_PAD + 1 + CONV_SUB]
        bg = scu_ref[0, pl.ds(start, CONV_SUB), 0:D_SC].astype(F32)
        ys_ref[0, pl.ds(start, CONV_SUB), :] = (bg * acc).astype(BF16)
        return carry

    lax.fori_loop(0, n_blk, sc_blk, 0)


def _convs(conf, scu, cw, cb, lnw, lnb, sw, sb, row_len):
    B, L, _ = conf.shape
    t_out = row_len // SUBLANES if L == row_len else row_len
    full = lambda a: pl.BlockSpec(a.shape, lambda b: (0,) * a.ndim)
    cw_p = jnp.zeros((32, D_CONF), F32).at[:CONF_K].set(cw)
    sw_p = jnp.zeros((8, D_SC), F32).at[:SC_K].set(sw)
    small = [cw_p, cb.reshape(1, D_CONF), lnw.reshape(1, D_CONF), lnb.reshape(1, D_CONF),
             sw_p, sb.reshape(1, D_SC)]
    return pl.pallas_call(
        functools.partial(_conv_kernel, row_len=row_len),
        grid=(B,),
        in_specs=[pl.BlockSpec((1, L, 2 * D_CONF), lambda b: (b, 0, 0)),
                  pl.BlockSpec((1, L, 3 * D_SC), lambda b: (b, 0, 0))] + [full(a) for a in small],
        out_specs=[pl.BlockSpec((1, L, D_CONF), lambda b: (b, 0, 0)),
                   pl.BlockSpec((1, L, D_SC), lambda b: (b, 0, 0))],
        out_shape=[jax.ShapeDtypeStruct((B, L, D_CONF), BF16),
                   jax.ShapeDtypeStruct((B, L, D_SC), BF16)],
        scratch_shapes=[pltpu.VMEM((D_CONF // LANES, (t_out + 2 * CONV_PAD) * SUBLANES, LANES), F32),
                        pltpu.VMEM((D_CONF // LANES, t_out * SUBLANES, LANES), F32),
                        pltpu.VMEM((L + 2 * SC_PAD, D_SC), F32)],
        compiler_params=_params(("parallel",)),
        name="convs",
    )(conf, scu, *small)


def _gla_kernel(qkv_ref, afab_ref, r_ref, s0f_ref, s0b_ref, wab_ref, bab_ref, gnw_ref,
                og_ref, sff_ref, sfb_ref, p_ref, qif_ref, qib_ref, spf_ref, ub_ref, gb_ref):
    L = qkv_ref.shape[1]
    C = CHUNK
    G = min(GLA_BLOCK, L // C)
    R = G * C
    n_blk = L // R
    mid = C // 2
    scale = GLA_DK ** -0.5
    nt = (((1,), (1,)), ((), ()))
    T = min(R, CUMSUM_ROWS)
    ii = lax.broadcasted_iota(jnp.int32, (T, T), 0)
    jj = lax.broadcasted_iota(jnp.int32, (T, T), 1)
    same = (ii // C) == (jj // C)
    tri_f = jnp.where(same, jnp.where(ii >= jj, 1.0, 0.0), 0.0).astype(BF16)
    tri_b = jnp.where(same, jnp.where(jj >= ii, 1.0, 0.0), 0.0).astype(BF16)
    PK = 2 * GLA_DK
    PV = 2 * GLA_DV
    ci = lax.broadcasted_iota(jnp.int32, (C, 2 * C), 0)
    cj = lax.broadcasted_iota(jnp.int32, (C, 2 * C), 1) % C
    lower = ci >= cj
    upper = cj >= ci
    kr = lax.broadcasted_iota(jnp.int32, (2 * C, PK), 0) // C
    kc = lax.broadcasted_iota(jnp.int32, (2 * C, PK), 1) // GLA_DK
    key_diag = kr == kc
    vr = lax.broadcasted_iota(jnp.int32, (2 * C, PV), 0) // C
    vc = lax.broadcasted_iota(jnp.int32, (2 * C, PV), 1) // GLA_DV
    val_diag = vr == vc

    def col_bcast(row_vec):
        return jnp.broadcast_to(row_vec, (GLA_DV, GLA_QK)).T

    def chunk_cumsum(tri, la):
        hi = la.astype(BF16)
        lo = (la - hi.astype(F32)).astype(BF16)
        parts = jnp.concatenate([hi, lo], axis=1)
        both = jnp.concatenate([jnp.dot(tri, parts[s * T:(s + 1) * T], preferred_element_type=F32)
                                for s in range(R // T)], axis=0)
        return both[:, :GLA_QK] + both[:, GLA_QK:]

    def v_pair(rows, p):
        return qkv_ref[0, rows, 2 * GLA_QK + p * PV:2 * GLA_QK + (p + 1) * PV]

    def pass1(i, s_f):
        r0 = pl.multiple_of(i * R, R)
        ab = afab_ref[0, pl.ds(r0, R), 0:2 * GLA_RANK].astype(BF16)
        z = jnp.dot(ab, wab_ref[...], preferred_element_type=F32) + bab_ref[...]
        la = _log_sigmoid(z) * (1.0 / GLA_TAU)
        b_f = chunk_cumsum(tri_f, la[:, :GLA_QK])
        b_b = chunk_cumsum(tri_b, la[:, GLA_QK:])
        q_all = qkv_ref[0, pl.ds(r0, R), 0:GLA_QK].astype(F32) * scale
        k_all = qkv_ref[0, pl.ds(r0, R), GLA_QK:2 * GLA_QK].astype(F32)
        for g in range(G):
            c = i * G + g
            rows = pl.ds(pl.multiple_of(r0 + g * C, C), C)
            sl = slice(g * C, (g + 1) * C)
            q, k = q_all[sl], k_all[sl]
            scaled = []
            for b, tot_row in ((b_f[sl], C - 1), (b_b[sl], 0)):
                ref_row = b[mid:mid + 1]
                tot = b[tot_row:tot_row + 1]
                q_rel = q * jnp.exp(b - ref_row)
                k_rel = k * jnp.exp(ref_row - b)
                scaled.append((q_rel.astype(BF16), k_rel.astype(BF16),
                               (q_rel * jnp.exp(ref_row)).astype(BF16),
                               (k_rel * jnp.exp(tot - ref_row)).T.astype(BF16),
                               tot))
            (qf, kf, qif, kuf_t, g_f), (qb, kb, qib, kub_t, g_b) = scaled
            qif_ref[rows, :] = qif
            qib_ref[rows, :] = qib
            p_parts, uf, ub = [], [], []
            for p in range(GLA_HEADS // 2):
                ks = slice(p * PK, (p + 1) * PK)
                kbd_f = jnp.where(key_diag, jnp.concatenate([kf[:, ks]] * 2, axis=0), 0)
                kbd_b = jnp.where(key_diag, jnp.concatenate([kb[:, ks]] * 2, axis=0), 0)
                s_fwd = lax.dot_general(qf[:, ks], kbd_f, nt, preferred_element_type=F32)
                s_bwd = lax.dot_general(qb[:, ks], kbd_b, nt, preferred_element_type=F32)
                p_parts.append((jnp.where(lower, s_fwd, 0.0) + jnp.where(upper, s_bwd, 0.0)).astype(BF16))
                u = jnp.dot(jnp.concatenate([kuf_t[ks, :], kub_t[ks, :]], axis=0), v_pair(rows, p),
                            preferred_element_type=F32)
                for blk, dst in ((u[:PK], uf), (u[PK:], ub)):
                    dst += [blk[:GLA_DK, :GLA_DV], blk[GLA_DK:, GLA_DV:]]
            p_ref[rows, :] = jnp.concatenate(p_parts, axis=1)
            ub_ref[c] = jnp.concatenate(ub, axis=0)
            gb_ref[c] = jnp.broadcast_to(g_b, (8, GLA_QK))
            spf_ref[c] = s_f.astype(BF16)
            s_f = col_bcast(jnp.exp(g_f)) * s_f + jnp.concatenate(uf, axis=0)
        return s_f

    sff_ref[0] = lax.fori_loop(0, n_blk, pass1, s0f_ref[0])

    def pass2(t, s_b):
        i = n_blk - 1 - t
        for g in reversed(range(G)):
            c = i * G + g
            rows = pl.ds(pl.multiple_of(i * R + g * C, C), C)
            spf = spf_ref[c]
            snb = s_b.astype(BF16)
            zero_blk = jnp.zeros((GLA_DK, GLA_DV), BF16)

            def pair_state(s, p):
                top = jnp.concatenate([s[2 * p * GLA_DK:(2 * p + 1) * GLA_DK], zero_blk], axis=1)
                bot = jnp.concatenate([zero_blk, s[(2 * p + 1) * GLA_DK:(2 * p + 2) * GLA_DK]], axis=1)
                return jnp.concatenate([top, bot], axis=0)

            outs = []
            for p in range(GLA_HEADS // 2):
                ks = slice(p * PK, (p + 1) * PK)
                vbd = jnp.where(val_diag, jnp.concatenate([v_pair(rows, p)] * 2, axis=0), 0)
                lhs = jnp.concatenate([p_ref[rows, ks], qif_ref[rows, ks], qib_ref[rows, ks]], axis=1)
                rhs = jnp.concatenate([vbd, pair_state(spf, p), pair_state(snb, p)], axis=0)
                o_pair = jnp.dot(lhs, rhs, preferred_element_type=F32)
                for o in (o_pair[:, :GLA_DV], o_pair[:, GLA_DV:]):
                    ms = jnp.mean(o * o, axis=-1, keepdims=True)
                    outs.append(o * lax.rsqrt(ms + EPS))
            o_all = jnp.concatenate(outs, axis=-1) * gnw_ref[...]
            og_ref[0, rows, :] = (o_all * _silu(r_ref[0, rows, :].astype(F32))).astype(BF16)
            s_b = col_bcast(jnp.exp(gb_ref[c][0:1])) * s_b + ub_ref[c]
        return s_b

    sfb_ref[0] = lax.fori_loop(0, n_blk, pass2, s0b_ref[0])


def _gla(qkv, afab, r, s0f, s0b, wab, bab, gnw):
    B, L, _ = qkv.shape
    n = L // CHUNK
    tok = lambda w: pl.BlockSpec((1, L, w), lambda b: (b, 0, 0))
    st = pl.BlockSpec((1, GLA_QK, GLA_DV), lambda b: (b, 0, 0))
    full = lambda a: pl.BlockSpec(a.shape, lambda b: (0,) * a.ndim)
    small = [wab, bab, gnw.reshape(1, GLA_V)]
    return pl.pallas_call(
        _gla_kernel,
        grid=(B,),
        in_specs=[tok(P_R - P_QKV), tok(AFAB_W), tok(GLA_V), st, st] + [full(a) for a in small],
        out_specs=[tok(GLA_V), st, st],
        out_shape=[jax.ShapeDtypeStruct((B, L, GLA_V), BF16),
                   jax.ShapeDtypeStruct((B, GLA_QK, GLA_DV), F32),
                   jax.ShapeDtypeStruct((B, GLA_QK, GLA_DV), F32)],
        scratch_shapes=[pltpu.VMEM((L, GLA_QK), BF16),
                        pltpu.VMEM((L, GLA_QK), BF16), pltpu.VMEM((L, GLA_QK), BF16),
                        pltpu.VMEM((n, GLA_QK, GLA_DV), BF16),
                        pltpu.VMEM((n, GLA_QK, GLA_DV), F32),
                        pltpu.VMEM((n, 8, GLA_QK), F32)],
        compiler_params=_params(("parallel",)),
        name="gla",
    )(qkv, afab, r, s0f, s0b, *small)


def _outproj_kernel(yc_ref, og_ref, ys_ref, x_ref, wo_ref, g2_ref, nw_ref, sh_ref, sc_ref, rwt_ref,
                    xo_ref, h2_ref, lt_ref):
    y = jnp.dot(yc_ref[0], wo_ref[0:D_CONF, :], preferred_element_type=F32)
    y = y + jnp.dot(og_ref[0], wo_ref[D_CONF:D_CONF + GLA_V, :], preferred_element_type=F32)
    y = y + jnp.dot(ys_ref[0], wo_ref[D_CONF + GLA_V:, :], preferred_element_type=F32)
    x = x_ref[0] + g2_ref[0] * y
    xo_ref[0] = x
    hb = _rms_mod(x, nw_ref[...], sh_ref[0], sc_ref[0]).astype(BF16)
    h2_ref[0] = hb
    lt_ref[0] = lax.dot_general(rwt_ref[...], hb, (((1,), (1,)), ((), ())),
                                preferred_element_type=F32)


def _outproj(yc, og, ys, x, wo, g2, nw, shift, scale, rwt):
    B, L, D = x.shape
    tm = min(L, TOKEN_BLOCK)
    tok = lambda w: pl.BlockSpec((1, tm, w), lambda b, t: (b, t, 0))
    per_b = pl.BlockSpec((1, 1, D), lambda b, t: (b, 0, 0))
    return pl.pallas_call(
        _outproj_kernel,
        grid=(B, L // tm),
        in_specs=[tok(D_CONF), tok(GLA_V), tok(D_SC), tok(D),
                  pl.BlockSpec((D, D), lambda b, t: (0, 0)), per_b,
                  pl.BlockSpec((1, D), lambda b, t: (0, 0)), per_b, per_b,
                  pl.BlockSpec((N_EXPERTS, D), lambda b, t: (0, 0))],
        out_specs=[tok(D), tok(D), pl.BlockSpec((1, N_EXPERTS, tm), lambda b, t: (b, 0, t))],
        out_shape=[jax.ShapeDtypeStruct((B, L, D), F32), jax.ShapeDtypeStruct((B, L, D), BF16),
                   jax.ShapeDtypeStruct((B, N_EXPERTS, L), F32)],
        compiler_params=_params(("parallel", "parallel")),
        name="outproj",
    )(yc, og, ys, x, wo, g2, nw.reshape(1, D), shift, scale, rwt)


def _lane_cumsum(flags):
    rows, n = flags.shape
    i = lax.broadcasted_iota(jnp.int32, (LANES, LANES), 0)
    j = lax.broadcasted_iota(jnp.int32, (LANES, LANES), 1)
    upper = jnp.where(i <= j, 1.0, 0.0).astype(BF16)
    xb = flags.astype(F32).astype(BF16)
    out, offset = [], jnp.zeros((rows, 1), F32)
    for blk in range(n // LANES):
        part = jnp.dot(xb[:, blk * LANES:(blk + 1) * LANES], upper, preferred_element_type=F32) + offset
        out.append(part)
        offset = part[:, LANES - 1:LANES]
    return jnp.concatenate(out, axis=1).astype(jnp.int32)


def _route_kernel(lt_ref, pos_ref, gate_ref, bnd_ref, *, cap):
    bs, E, L = lt_ref.shape
    lt = lt_ref[...]
    e = jnp.exp(lt - jnp.max(lt, axis=1, keepdims=True))
    aff = (e / jnp.sum(e, axis=1, keepdims=True)).reshape(bs * E, L)

    def search(i, t):
        cand = t | lax.shift_left(jnp.int32(1), 30 - i)
        cnt = jnp.sum((aff >= pltpu.bitcast(cand, F32)).astype(jnp.int32), axis=1, keepdims=True)
        return jnp.where(cnt >= cap, cand, t)

    thr_bits = lax.fori_loop(0, 31, search, jnp.zeros((bs * E, 1), jnp.int32))
    thr = pltpu.bitcast(thr_bits, F32)
    gt = aff > thr
    eq = aff == thr
    need = cap - jnp.sum(gt.astype(jnp.int32), axis=1, keepdims=True)
    eq_i = eq.astype(jnp.int32)
    eq_rank = _lane_cumsum(eq_i) - eq_i
    sel = gt | (eq & (eq_rank < need))
    sel_i = sel.astype(jnp.int32)
    slot = _lane_cumsum(sel_i) - sel_i
    pos_ref[...] = jnp.where(sel, slot, -1).reshape(bs, E, L)
    gate_ref[...] = aff.reshape(bs, E, L)
    bnd = jnp.concatenate([slot[:, k * TOK_TILE:k * TOK_TILE + 1] for k in range(L // TOK_TILE)], axis=1)
    bnd_ref[...] = bnd.reshape(bs, E, L // TOK_TILE)


def _route(lt, cap):
    B, E, L = lt.shape
    bs = min(B, ROUTE_SAMPLES)
    spec = pl.BlockSpec((bs, E, L), lambda b: (b, 0, 0))
    nt = L // TOK_TILE
    return pl.pallas_call(
        functools.partial(_route_kernel, cap=cap),
        grid=(B // bs,),
        in_specs=[spec], out_specs=[spec, spec, pl.BlockSpec((bs, E, nt), lambda b: (b, 0, 0))],
        out_shape=[jax.ShapeDtypeStruct((B, E, L), jnp.int32), jax.ShapeDtypeStruct((B, E, L), F32),
                   jax.ShapeDtypeStruct((B, E, nt), jnp.int32)],
        compiler_params=_params(("parallel",)),
        name="route",
    )(lt)


def _window(lo, hi, width, cap):
    a0 = jnp.minimum((lo // SLOT_ALIGN) * SLOT_ALIGN, cap - width)
    n = jnp.where(hi > lo, (hi - a0 + width - 1) // width, 0)
    return a0, n


def _one_hot_rows(pos_row, start, width, first_row=None):
    slot = lax.broadcasted_iota(jnp.int32, (width, pos_row.shape[1]), 0) + start
    on = 1.0 if first_row is None else jnp.where(slot >= first_row, 1.0, 0.0)
    return jnp.where(slot == pos_row, on, 0.0).astype(BF16)


def _tile_windows(bnd_ref, pos_ref, b, t, toks, step, n_tiles, cap, width):
    nt = pos_ref.shape[2] // TOK_TILE
    wins = []
    for e in range(N_EXPERTS):
        base = (b * N_EXPERTS + e) * nt
        lo = bnd_ref[base + t * step]
        hi = jnp.where(t + 1 < n_tiles, bnd_ref[base + jnp.minimum(t + 1, n_tiles - 1) * step], cap)
        a0, n = _window(lo, hi, width, cap)
        wins.append((pl.multiple_of(a0, SLOT_ALIGN), n, pos_ref[0, e:e + 1, toks]))
    return wins


def _window_one_hots(wins, w, cap, width):
    ms, starts = [], []
    for a0, n, pos_row in wins:
        if w is None:
            start, first = a0, None
        else:
            first = a0 + w * width
            start = pl.multiple_of(jnp.minimum(first, cap - width), SLOT_ALIGN)
            first = jnp.where(w < n, first, cap)
        ms.append(_one_hot_rows(pos_row, start, width, first))
        starts.append(start)
    return jnp.concatenate(ms, axis=0), starts


def _max_windows(wins):
    return functools.reduce(jnp.maximum, [n for _, n, _ in wins])


def _gather_kernel(bnd_ref, pos_ref, h_ref, xs_ref, *, cap, tile, width):
    b = pl.program_id(0)
    L = h_ref.shape[1]
    step = tile // TOK_TILE
    n_tiles = L // tile
    unroll = min(TILES_PER_ITER, n_tiles)
    xs_ref[...] = jnp.zeros_like(xs_ref)

    def scatter_windows(m_all, starts, toks):
        part = jnp.dot(m_all, h_ref[0, toks, :], preferred_element_type=F32)
        for e, start in enumerate(starts):
            xs_ref[0, e, pl.ds(start, width), :] += part[e * width:(e + 1) * width].astype(BF16)

    def tiles_body(it, carry):
        tiles = []
        for u in range(unroll):
            k = it * unroll + u
            toks = pl.ds(pl.multiple_of(k * tile, tile), tile)
            wins = _tile_windows(bnd_ref, pos_ref, b, k, toks, step, n_tiles, cap, width)
            scatter_windows(*_window_one_hots(wins, None, cap, width), toks)
            tiles.append((toks, wins))
        for toks, wins in tiles:
            def extra(w, c, toks=toks, wins=wins):
                scatter_windows(*_window_one_hots(wins, w, cap, width), toks)
                return c

            lax.fori_loop(1, _max_windows(wins), extra, 0)
        return carry

    lax.fori_loop(0, n_tiles // unroll, tiles_body, 0)


def _gather(bnd, pos, h2, cap):
    B, L, D = h2.shape
    E = N_EXPERTS
    tile = min(L, 256)
    width = min(cap, 64)
    return pl.pallas_call(
        functools.partial(_gather_kernel, cap=cap, tile=tile, width=width),
        grid_spec=pltpu.PrefetchScalarGridSpec(
            num_scalar_prefetch=1, grid=(B,),
            in_specs=[pl.BlockSpec((1, E, L), lambda b, s: (b, 0, 0)),
                      pl.BlockSpec((1, L, D), lambda b, s: (b, 0, 0))],
            out_specs=pl.BlockSpec((1, E, cap, D), lambda b, s: (b, 0, 0, 0))),
        out_shape=jax.ShapeDtypeStruct((B, E, cap, D), BF16),
        compiler_params=_params(("arbitrary",)),
        name="gather",
    )(bnd.reshape(-1), pos, h2)


def _expert_ffn(pos_ref, gate_ref, xs_ref, y_ref, wb_ref):
    n, _, cap, D = xs_ref.shape
    L = pos_ref.shape[2]
    row = pl.ds(pl.program_id(0) % SUBLANES, 1)
    slot = lax.broadcasted_iota(jnp.int32, (cap, L), 0)
    gates = [jnp.sum(jnp.where(slot == pos_ref[i, row, :], gate_ref[i, row, :], 0.0), axis=1, keepdims=True)
             for i in range(n)]
    gate = jnp.concatenate(gates, axis=0)
    xs = xs_ref[:, 0].reshape(n * cap, D)
    hid = _silu(jnp.dot(xs, wb_ref[0], preferred_element_type=F32))
    hid = (hid * jnp.dot(xs, wb_ref[1], preferred_element_type=F32)).astype(BF16)
    y = (jnp.dot(hid, wb_ref[2], preferred_element_type=F32) * gate).astype(BF16)
    y_ref[:, 0] = y.reshape(n, cap, D)


def _ffn_kernel(*refs, with_ctx):
    if with_ctx:
        (pos_ref, gate_ref, xs_ref, cpos_ref, cgate_ref, cxs_ref, w1_ref, w3_ref, w2_ref,
         y_ref, cy_ref, wb_ref) = refs
    else:
        pos_ref, gate_ref, xs_ref, w1_ref, w3_ref, w2_ref, y_ref, wb_ref = refs
    j = pl.program_id(1)

    @pl.when(j == 0)
    def _():
        for i, w_ref in enumerate((w1_ref, w3_ref, w2_ref)):
            wb_ref[i] = w_ref[0, 0].astype(BF16)

    if with_ctx:
        @pl.when(j == 0)
        def _():
            _expert_ffn(cpos_ref, cgate_ref, cxs_ref, cy_ref, wb_ref)

    @pl.when(j >= (1 if with_ctx else 0))
    def _():
        _expert_ffn(pos_ref, gate_ref, xs_ref, y_ref, wb_ref)


def _ffn(lat, ctx, w1, w3, w2, layer, bb):
    pos, gate, xs = lat
    B, E, cap, D = xs.shape
    L = pos.shape[2]
    bb = min(bb, B)
    nb = B // bb
    with_ctx = ctx is not None
    first_lat = 1 if with_ctx else 0
    lat_idx = lambda e, j: (jnp.maximum(j - first_lat, 0), e, 0, 0)
    lat_rows = lambda e, j: (jnp.maximum(j - first_lat, 0), e // SUBLANES, 0)
    ins = [pos, gate, xs]
    in_specs = [pl.BlockSpec((bb, SUBLANES, L), lat_rows), pl.BlockSpec((bb, SUBLANES, L), lat_rows),
                pl.BlockSpec((bb, 1, cap, D), lat_idx)]
    out_shape = [jax.ShapeDtypeStruct((B, E, cap, D), BF16)]
    out_specs = [pl.BlockSpec((bb, 1, cap, D), lat_idx)]
    if with_ctx:
        cpos, cgate, cxs = ctx
        capc, Lc = cxs.shape[2], cpos.shape[2]
        ctx_idx = lambda e, j: (0, e, 0, 0)
        ctx_rows = lambda e, j: (0, e // SUBLANES, 0)
        ins += [cpos, cgate, cxs]
        in_specs += [pl.BlockSpec((B, SUBLANES, Lc), ctx_rows), pl.BlockSpec((B, SUBLANES, Lc), ctx_rows),
                     pl.BlockSpec((B, 1, capc, D), ctx_idx)]
        out_shape.append(jax.ShapeDtypeStruct((B, E, capc, D), BF16))
        out_specs.append(pl.BlockSpec((B, 1, capc, D), ctx_idx))
    wspec = pl.BlockSpec((1, 1, D, D), lambda e, j: (layer, e, 0, 0))
    outs = pl.pallas_call(
        functools.partial(_ffn_kernel, with_ctx=with_ctx),
        grid=(E, nb + (1 if with_ctx else 0)),
        in_specs=in_specs + [wspec, wspec, wspec],
        out_specs=out_specs, out_shape=out_shape,
        scratch_shapes=[pltpu.VMEM((3, D, D), BF16)],
        compiler_params=_params(("arbitrary", "arbitrary")),
        name="ffn",
    )(*ins, w1, w3, w2)
    return outs if with_ctx else (outs[0], None)


def _combine_kernel(bnd_ref, pos_ref, y_ref, x_ref, g_ref, *rest, cap, width, final_norm):
    nw_ref, out_ref = rest if final_norm else (None, rest[0])
    b = pl.program_id(0)
    L = pos_ref.shape[2]
    tile = min(COMB_TILE, out_ref.shape[1])
    step = tile // TOK_TILE
    n_tiles = L // tile
    tiles_per_step = out_ref.shape[1] // tile
    tn = (((0,), (0,)), ((), ()))

    def gathered_sum(m_all, starts):
        y_all = jnp.concatenate([y_ref[0, e, pl.ds(start, width), :] for e, start in enumerate(starts)],
                                axis=0)
        return lax.dot_general(m_all, y_all, tn, preferred_element_type=F32)

    tiles = []
    for i in range(tiles_per_step):
        t = pl.program_id(1) * tiles_per_step + i
        toks = pl.ds(pl.multiple_of(t * tile, tile), tile)
        rows = slice(i * tile, (i + 1) * tile)
        wins = _tile_windows(bnd_ref, pos_ref, b, t, toks, step, n_tiles, cap, width)
        out_ref[0, rows, :] = gathered_sum(*_window_one_hots(wins, None, cap, width))
        tiles.append((rows, wins))
    for rows, wins in tiles:
        def extra(w, c, rows=rows, wins=wins):
            out_ref[0, rows, :] += gathered_sum(*_window_one_hots(wins, w, cap, width))
            return c

        lax.fori_loop(1, _max_windows(wins), extra, 0)
    for rows, _ in tiles:
        x = x_ref[0, rows, :] + g_ref[0] * out_ref[0, rows, :]
        if final_norm:
            x = x * lax.rsqrt(jnp.mean(x * x, axis=-1, keepdims=True) + EPS) * nw_ref[...]
        out_ref[0, rows, :] = x


def _combine(bnd, pos, y, x, g, final_nw):
    B, E, cap, D = y.shape
    L = x.shape[1]
    width = min(cap, 64)
    tm = min(L, TOKEN_BLOCK)
    final_norm = final_nw is not None
    tok = pl.BlockSpec((1, tm, D), lambda b, t, s: (b, t, 0))
    ins = [bnd.reshape(-1), pos, y, x, g]
    in_specs = [pl.BlockSpec((1, E, L), lambda b, t, s: (b, 0, 0)),
                pl.BlockSpec((1, E, cap, D), lambda b, t, s: (b, 0, 0, 0)),
                tok, pl.BlockSpec((1, 1, D), lambda b, t, s: (b, 0, 0))]
    if final_norm:
        ins.append(final_nw.reshape(1, D))
        in_specs.append(pl.BlockSpec((1, D), lambda b, t, s: (0, 0)))
    return pl.pallas_call(
        functools.partial(_combine_kernel, cap=cap, width=width, final_norm=final_norm),
        grid_spec=pltpu.PrefetchScalarGridSpec(
            num_scalar_prefetch=1, grid=(B, L // tm), in_specs=in_specs, out_specs=tok),
        out_shape=jax.ShapeDtypeStruct((B, L, D), F32),
        compiler_params=_params(("arbitrary", "arbitrary")),
        name="combine",
    )(*ins)


def _moe(lat, ctx, w1, w3, w2, layer, bb, final_nw=None):
    routed = []
    for lt, h2, x, g in (lat, ctx) if ctx is not None else (lat,):
        cap = EC_FACTOR * h2.shape[1] // N_EXPERTS
        pos, gate, bnd = _route(lt, cap)
        routed.append((pos, gate, _gather(bnd, pos, h2, cap), bnd, x, g))
    ys = _ffn(routed[0][:3], routed[1][:3] if ctx is not None else None, w1, w3, w2, layer, bb)
    outs = [_combine(bnd, pos, y, x, g, final_nw if i == 0 else None)
            for i, ((pos, _, _, bnd, x, g), y) in enumerate(zip(routed, ys))]
    return outs[0], (outs[1] if ctx is not None else None)


def _permute_w_in(w):
    cols = jnp.concatenate([w[:, 0:OFF_AF], w[:, OFF_R:OFF_SC], w[:, OFF_SC:D_IN], w[:, OFF_AF:OFF_R]],
                           axis=1)
    return jnp.pad(cols, ((0, 0), (0, P_TOTAL - D_IN))).astype(BF16)


def kernel(x, c, ctx, c_ctx, ada_w, ada_b, norm1_w, norm2_w, w_in, conf_dw_w, conf_dw_b, conf_ln_w,
           conf_ln_b, gla_wa_f, gla_ba_f, gla_wa_b, gla_ba_b, gla_gn_w, sc_w, sc_b, w_out, router_w,
           exp_w1, exp_w3, exp_w2, final_norm_w):
    B, L, D = x.shape
    Lc = ctx.shape[1]

    cond = jnp.concatenate([c, c_ctx[None, :], jnp.zeros((7, D), F32)], axis=0)
    mods = _ada(cond, ada_w, ada_b)

    xc = ctx
    zero_state = jnp.zeros((B, GLA_QK, GLA_DV), F32)
    for i in range(DEPTH):
        last = i == DEPTH - 1
        m_lat = [mods[i, :B, j * D:(j + 1) * D].reshape(B, 1, D) for j in range(6)]
        m_ctx = [jnp.broadcast_to(mods[i, B, j * D:(j + 1) * D].reshape(1, 1, D), (B, 1, D))
                 for j in range(6)]
        w_perm = _permute_w_in(w_in[i])
        wo = w_out[i].astype(BF16)
        rwt = router_w[i].T.astype(BF16)
        zero_wa = jnp.zeros((GLA_RANK, GLA_QK), F32)
        wa = jnp.block([[gla_wa_f[i], zero_wa], [zero_wa, gla_wa_b[i]]]).astype(BF16)
        ba = jnp.concatenate([gla_ba_f[i], gla_ba_b[i]]).reshape(1, 2 * GLA_QK)
        experts = (exp_w1, exp_w3, exp_w2, i)
        conv_w = (conf_dw_w[i], conf_dw_b[i], conf_ln_w[i], conf_ln_b[i], sc_w[i], sc_b[i])

        if last:
            c_qkv, c_r, c_afab = _proj(xc, norm1_w[i], m_ctx[0], m_ctx[1], w_perm, ("qkv", "r", "afab"))
        else:
            c_conf, c_qkv, c_r, c_sc, c_afab = _proj(xc, norm1_w[i], m_ctx[0], m_ctx[1], w_perm)
        c_og, c_sf, c_sb = _gla(c_qkv, c_afab, c_r, zero_state, zero_state, wa, ba, gla_gn_w[i])

        conf, qkv, r, scu, afab = _proj(x, norm1_w[i], m_lat[0], m_lat[1], w_perm)
        yc, ys = _convs(conf, scu, *conv_w, row_len=GRID_W)
        og, _, _ = _gla(qkv, afab, r, c_sf, c_sb, wa, ba, gla_gn_w[i])
        x, h2, lt = _outproj(yc, og, ys, x, wo, m_lat[2], norm2_w[i], m_lat[3], m_lat[4], rwt)
        moe_ctx = None
        if not last:
            c_yc, c_ys = _convs(c_conf, c_sc, *conv_w, row_len=Lc)
            xc, c_h2, c_lt = _outproj(c_yc, c_og, c_ys, xc, wo, m_ctx[2], norm2_w[i], m_ctx[3], m_ctx[4],
                                      rwt)
            moe_ctx = (c_lt, c_h2, xc, m_ctx[5])
        x, xc = _moe((lt, h2, x, m_lat[5]), moe_ctx, *experts, FFN_SAMPLES,
                     final_norm_w if last else None)
    return x
```

```python
import functools

import jax
import jax.numpy as jnp
from jax import lax
from jax.experimental import pallas as pl
from jax.experimental.pallas import tpu as pltpu

F32 = jnp.float32
BF16 = jnp.bfloat16
HI = lax.Precision.HIGHEST

D_MODEL = 1024
DEPTH = 2
GRID_W = 64
D_CONF = 256
D_SC = 256
GLA_HEADS = 4
GLA_DK = 64
GLA_DV = 128
GLA_QK = GLA_HEADS * GLA_DK
GLA_V = GLA_HEADS * GLA_DV
GLA_RANK = 16
GLA_TAU = 16.0
CONF_K = 31
SC_K = 3
N_EXPERTS = 16
EC_FACTOR = 2
EPS = 1e-6

OFF_Q = 2 * D_CONF
OFF_K = OFF_Q + GLA_QK
OFF_V = OFF_K + GLA_QK
OFF_AF = OFF_V + GLA_V
OFF_AB = OFF_AF + GLA_RANK
OFF_R = OFF_AB + GLA_RANK
OFF_SC = OFF_R + GLA_V
D_IN = OFF_SC + 3 * D_SC

P_CONF = 0
P_QKV = 512
P_R = 1536
P_SC = 2048
P_AFAB = 2816
P_TOTAL = 2944
AFAB_W = 128

CHUNK = 64
GLA_BLOCK = 16
TOKEN_BLOCK = 1024
CUMSUM_ROWS = 256
CONV_SUB = 64
CONV_TOKENS = 16
SUBLANES = 8
LANES = 128
CONV_PAD = 16
SC_PAD = 8
TOK_TILE = 128
COMB_TILE = 256
TILES_PER_ITER = 4
ROUTE_SAMPLES = 8
FFN_SAMPLES = 4
SLOT_ALIGN = 16
VMEM_LIMIT = 56 * 1024 * 1024


def _sigmoid(x):
    return 1.0 / (1.0 + jnp.exp(-x))


def _silu(x):
    return x * _sigmoid(x)


def _log_sigmoid(z):
    return jnp.minimum(z, 0.0) - jnp.log(1.0 + jnp.exp(-jnp.abs(z)))


def _params(sem):
    return pltpu.CompilerParams(dimension_semantics=sem, vmem_limit_bytes=VMEM_LIMIT)


def _ada_kernel(s_ref, w_ref, b_ref, o_ref):
    s = _silu(s_ref[...])
    o_ref[0] = jnp.dot(s, w_ref[0], precision=HI, preferred_element_type=F32) + b_ref[0]


def _ada(s_in, ada_w, ada_b):
    rows = s_in.shape[0]
    tn = 1024
    return pl.pallas_call(
        _ada_kernel,
        grid=(DEPTH, 6 * D_MODEL // tn),
        in_specs=[pl.BlockSpec((rows, D_MODEL), lambda l, n: (0, 0)),
                  pl.BlockSpec((1, D_MODEL, tn), lambda l, n: (l, 0, n)),
                  pl.BlockSpec((1, 1, tn), lambda l, n: (l, 0, n))],
        out_specs=pl.BlockSpec((1, rows, tn), lambda l, n: (l, 0, n)),
        out_shape=jax.ShapeDtypeStruct((DEPTH, rows, 6 * D_MODEL), F32),
        compiler_params=_params(("arbitrary", "arbitrary")),
        name="ada",
    )(s_in, ada_w, ada_b.reshape(DEPTH, 1, 6 * D_MODEL))


def _rms_mod(x, nw, shift, scale):
    ms = jnp.mean(x * x, axis=-1, keepdims=True)
    return (x * lax.rsqrt(ms + EPS) * nw) * (1.0 + scale) + shift


PROJ_GROUPS = {"conf": (P_CONF, P_QKV, BF16), "qkv": (P_QKV, P_R, BF16), "r": (P_R, P_SC, BF16),
               "sc": (P_SC, P_AFAB, BF16), "afab": (P_AFAB, P_TOTAL, F32)}


def _proj_kernel(x_ref, nw_ref, sh_ref, sc_ref, w_ref, *out_refs, groups):
    hb = _rms_mod(x_ref[0], nw_ref[...], sh_ref[0], sc_ref[0]).astype(BF16)
    for name, o_ref in zip(groups, out_refs):
        lo, hi, dt = PROJ_GROUPS[name]
        o_ref[0] = jnp.dot(hb, w_ref[:, lo:hi], preferred_element_type=F32).astype(dt)


def _proj(x, nw, shift, scale, w_perm, groups=tuple(PROJ_GROUPS)):
    B, L, D = x.shape
    tm = min(L, TOKEN_BLOCK)
    tok = lambda w: pl.BlockSpec((1, tm, w), lambda b, t: (b, t, 0))
    per_b = pl.BlockSpec((1, 1, D), lambda b, t: (b, 0, 0))
    widths = [(PROJ_GROUPS[g][1] - PROJ_GROUPS[g][0], PROJ_GROUPS[g][2]) for g in groups]
    return pl.pallas_call(
        functools.partial(_proj_kernel, groups=groups),
        grid=(B, L // tm),
        in_specs=[tok(D), pl.BlockSpec((1, D), lambda b, t: (0, 0)), per_b, per_b,
                  pl.BlockSpec((D, P_TOTAL), lambda b, t: (0, 0))],
        out_specs=[tok(w) for w, _ in widths],
        out_shape=[jax.ShapeDtypeStruct((B, L, w), dt) for w, dt in widths],
        compiler_params=_params(("parallel", "parallel")),
        name="proj",
    )(x, nw.reshape(1, D), shift, scale, w_perm)


def _conv_kernel(conf_ref, scu_ref, cw_ref, cb_ref, lnw_ref, lnb_ref, sw_ref, sb_ref,
                 yc_ref, ys_ref, yt_ref, ot_ref, pz_ref, *, row_len):
    L = conf_ref.shape[1]
    n_rows = L // row_len
    C = D_CONF
    S = SUBLANES

    split_row = n_rows == 1
    assert split_row or n_rows % S == 0
    t_out = row_len // S if split_row else row_len
    span = t_out + 2 * CONV_PAD

    halves = [slice(h * LANES, (h + 1) * LANES) for h in range(C // LANES)]

    def group(g, carry):
        for h in range(len(halves)):
            if not split_row:
                yt_ref[h, 0:CONV_PAD * S, :] = jnp.zeros((CONV_PAD * S, LANES), F32)
                yt_ref[h, (CONV_PAD + t_out) * S:, :] = jnp.zeros((CONV_PAD * S, LANES), F32)
        for k in range(S):
            if split_row:
                tok0 = k * t_out - CONV_PAD
                lo, hi = max(tok0, 0), min(tok0 + span, L)
                src = conf_ref[0, lo:hi, :]
                for a, b in ((0, lo - tok0), (hi - tok0, span)):
                    for h in range(len(halves)):
                        if b > a:
                            yt_ref[h, pl.ds(k + S * a, b - a, stride=S), :] = jnp.zeros((b - a, LANES), F32)
                first = lo - tok0
            else:
                src = conf_ref[0, pl.ds(pl.multiple_of((g * S + k) * row_len, row_len), row_len), :]
                first = CONV_PAD
            u = src.astype(F32)
            z = u[:, :C] * _sigmoid(u[:, C:])
            for h, ls in enumerate(halves):
                yt_ref[h, pl.ds(k + S * first, z.shape[0], stride=S), :] = z[:, ls]

        def out_rows(t):
            return pl.ds(pl.multiple_of(t * S, S), S)

        for h, ls in enumerate(halves):
            def taps(i, c, h=h, ls=ls):
                t0 = i * CONV_TOKENS
                accs = [None] * CONV_TOKENS
                for m in range(CONV_TOKENS + CONF_K - 1):
                    v = yt_ref[h, out_rows(t0 + CONV_PAD - CONF_K // 2 + m), :]
                    for tt in range(max(0, m - CONF_K + 1), min(CONV_TOKENS, m + 1)):
                        term = cw_ref[m - tt:m - tt + 1, ls] * v
                        accs[tt] = term if accs[tt] is None else accs[tt] + term
                for tt in range(CONV_TOKENS):
                    ot_ref[h, out_rows(t0 + tt), :] = accs[tt] + cb_ref[:, ls]
                return c

            lax.fori_loop(0, t_out // CONV_TOKENS, taps, 0)

        for k in range(S):
            dst = k * t_out if split_row else pl.multiple_of((g * S + k) * row_len, row_len)
            o = jnp.concatenate([ot_ref[h, pl.ds(k, t_out, stride=S), :] for h in range(len(halves))], axis=1)
            mu = jnp.mean(o, axis=-1, keepdims=True)
            cen = o - mu
            var = jnp.mean(cen * cen, axis=-1, keepdims=True)
            y = cen * lax.rsqrt(var + EPS) * lnw_ref[...] + lnb_ref[...]
            yc_ref[0, pl.ds(dst, t_out), :] = _silu(y).astype(BF16)
        return carry

    lax.fori_loop(0, 1 if split_row else n_rows // S, group, 0)

    n_blk = L // CONV_SUB
    pz_ref[0:SC_PAD, :] = jnp.zeros((SC_PAD, D_SC), F32)
    pz_ref[SC_PAD + L:, :] = jnp.zeros((SC_PAD, D_SC), F32)

    def fill_p(i, carry):
        start = pl.multiple_of(i * CONV_SUB, CONV_SUB)
        u = scu_ref[0, pl.ds(start, CONV_SUB), :].astype(F32)
        pz_ref[pl.ds(SC_PAD + start, CONV_SUB), :] = u[:, D_SC:2 * D_SC] * u[:, 2 * D_SC:]
        return carry

    lax.fori_loop(0, n_blk, fill_p, 0)

    def sc_blk(i, carry):
        start = pl.multiple_of(i * CONV_SUB, CONV_SUB)
        win = pz_ref[pl.ds(start, CONV_SUB + 2 * SC_PAD), :]
        acc = sb_ref[...] + sw_ref[0:1, :] * win[SC_PAD - 1:SC_PAD - 1 + CONV_SUB]
        acc = acc + sw_ref[1:2, :] * win[SC_PAD:SC_PAD + CONV_SUB]
        acc = acc + sw_ref[2:3, :] * win[SC_PAD + 1:SC_PAD + 1 + CONV_SUB]
        bg = scu_ref[0, pl.ds(start, CONV_SUB), 0:D_SC].astype(F32)
        ys_ref[0, pl.ds(start, CONV_SUB), :] = (bg * acc).astype(BF16)
        return carry

    lax.fori_loop(0, n_blk, sc_blk, 0)


def _convs(conf, scu, cw, cb, lnw, lnb, sw, sb, row_len):
    B, L, _ = conf.shape
    t_out = row_len // SUBLANES if L == row_len else row_len
    full = lambda a: pl.BlockSpec(a.shape, lambda b: (0,) * a.ndim)
    cw_p = jnp.zeros((32, D_CONF), F32).at[:CONF_K].set(cw)
    sw_p = jnp.zeros((8, D_SC), F32).at[:SC_K].set(sw)
    small = [cw_p, cb.reshape(1, D_CONF), lnw.reshape(1, D_CONF), lnb.reshape(1, D_CONF),
             sw_p, sb.reshape(1, D_SC)]
    return pl.pallas_call(
        functools.partial(_conv_kernel, row_len=row_len),
        grid=(B,),
        in_specs=[pl.BlockSpec((1, L, 2 * D_CONF), lambda b: (b, 0, 0)),
                  pl.BlockSpec((1, L, 3 * D_SC), lambda b: (b, 0, 0))] + [full(a) for a in small],
        out_specs=[pl.BlockSpec((1, L, D_CONF), lambda b: (b, 0, 0)),
                   pl.BlockSpec((1, L, D_SC), lambda b: (b, 0, 0))],
        out_shape=[jax.ShapeDtypeStruct((B, L, D_CONF), BF16),
                   jax.ShapeDtypeStruct((B, L, D_SC), BF16)],
        scratch_shapes=[pltpu.VMEM((D_CONF // LANES, (t_out + 2 * CONV_PAD) * SUBLANES, LANES), F32),
                        pltpu.VMEM((D_CONF // LANES, t_out * SUBLANES, LANES), F32),
                        pltpu.VMEM((L + 2 * SC_PAD, D_SC), F32)],
        compiler_params=_params(("parallel",)),
        name="convs",
    )(conf, scu, *small)


def _gla_kernel(qkv_ref, afab_ref, r_ref, s0f_ref, s0b_ref, wab_ref, bab_ref, gnw_ref,
                og_ref, sff_ref, sfb_ref, p_ref, qif_ref, qib_ref, spf_ref, ub_ref, gb_ref):
    L = qkv_ref.shape[1]
    C = CHUNK
    G = min(GLA_BLOCK, L // C)
    R = G * C
    n_blk = L // R
    mid = C // 2
    scale = GLA_DK ** -0.5
    nt = (((1,), (1,)), ((), ()))
    T = min(R, CUMSUM_ROWS)
    ii = lax.broadcasted_iota(jnp.int32, (T, T), 0)
    jj = lax.broadcasted_iota(jnp.int32, (T, T), 1)
    same = (ii // C) == (jj // C)
    tri_f = jnp.where(same, jnp.where(ii >= jj, 1.0, 0.0), 0.0).astype(BF16)
    tri_b = jnp.where(same, jnp.where(jj >= ii, 1.0, 0.0), 0.0).astype(BF16)
    PK = 2 * GLA_DK
    PV = 2 * GLA_DV
    ci = lax.broadcasted_iota(jnp.int32, (C, 2 * C), 0)
    cj = lax.broadcasted_iota(jnp.int32, (C, 2 * C), 1) % C
    lower = ci >= cj
    upper = cj >= ci
    kr = lax.broadcasted_iota(jnp.int32, (2 * C, PK), 0) // C
    kc = lax.broadcasted_iota(jnp.int32, (2 * C, PK), 1) // GLA_DK
    key_diag = kr == kc
    vr = lax.broadcasted_iota(jnp.int32, (2 * C, PV), 0) // C
    vc = lax.broadcasted_iota(jnp.int32, (2 * C, PV), 1) // GLA_DV
    val_diag = vr == vc

    def col_bcast(row_vec):
        return jnp.broadcast_to(row_vec, (GLA_DV, GLA_QK)).T

    def chunk_cumsum(tri, la):
        hi = la.astype(BF16)
        lo = (la - hi.astype(F32)).astype(BF16)
        parts = jnp.concatenate([hi, lo], axis=1)
        both = jnp.concatenate([jnp.dot(tri, parts[s * T:(s + 1) * T], preferred_element_type=F32)
                                for s in range(R // T)], axis=0)
        return both[:, :GLA_QK] + both[:, GLA_QK:]

    def v_pair(rows, p):
        return qkv_ref[0, rows, 2 * GLA_QK + p * PV:2 * GLA_QK + (p + 1) * PV]

    def pass1(i, s_f):
        r0 = pl.multiple_of(i * R, R)
        ab = afab_ref[0, pl.ds(r0, R), 0:2 * GLA_RANK].astype(BF16)
        z = jnp.dot(ab, wab_ref[...], preferred_element_type=F32) + bab_ref[...]
        la = _log_sigmoid(z) * (1.0 / GLA_TAU)
        b_f = chunk_cumsum(tri_f, la[:, :GLA_QK])
        b_b = chunk_cumsum(tri_b, la[:, GLA_QK:])
        q_all = qkv_ref[0, pl.ds(r0, R), 0:GLA_QK].astype(F32) * scale
        k_all = qkv_ref[0, pl.ds(r0, R), GLA_QK:2 * GLA_QK].astype(F32)
        for g in range(G):
            c = i * G + g
            rows = pl.ds(pl.multiple_of(r0 + g * C, C), C)
            sl = slice(g * C, (g + 1) * C)
            q, k = q_all[sl], k_all[sl]
            scaled = []
            for b, tot_row in ((b_f[sl], C - 1), (b_b[sl], 0)):
                ref_row = b[mid:mid + 1]
                tot = b[tot_row:tot_row + 1]
                q_rel = q * jnp.exp(b - ref_row)
                k_rel = k * jnp.exp(ref_row - b)
                scaled.append((q_rel.astype(BF16), k_rel.astype(BF16),
                               (q_rel * jnp.exp(ref_row)).astype(BF16),
                               (k_rel * jnp.exp(tot - ref_row)).T.astype(BF16),
                               tot))
            (qf, kf, qif, kuf_t, g_f), (qb, kb, qib, kub_t, g_b) = scaled
            qif_ref[rows, :] = qif
            qib_ref[rows, :] = qib
            p_parts, uf, ub = [], [], []
            for p in range(GLA_HEADS // 2):
                ks = slice(p * PK, (p + 1) * PK)
                kbd_f = jnp.where(key_diag, jnp.concatenate([kf[:, ks]] * 2, axis=0), 0)
                kbd_b = jnp.where(key_diag, jnp.concatenate([kb[:, ks]] * 2, axis=0), 0)
                s_fwd = lax.dot_general(qf[:, ks], kbd_f, nt, preferred_element_type=F32)
                s_bwd = lax.dot_general(qb[:, ks], kbd_b, nt, preferred_element_type=F32)
                p_parts.append((jnp.where(lower, s_fwd, 0.0) + jnp.where(upper, s_bwd, 0.0)).astype(BF16))
                u = jnp.dot(jnp.concatenate([kuf_t[ks, :], kub_t[ks, :]], axis=0), v_pair(rows, p),
                            preferred_element_type=F32)
                for blk, dst in ((u[:PK], uf), (u[PK:], ub)):
                    dst += [blk[:GLA_DK, :GLA_DV], blk[GLA_DK:, GLA_DV:]]
            p_ref[rows, :] = jnp.concatenate(p_parts, axis=1)
            ub_ref[c] = jnp.concatenate(ub, axis=0)
            gb_ref[c] = jnp.broadcast_to(g_b, (8, GLA_QK))
            spf_ref[c] = s_f.astype(BF16)
            s_f = col_bcast(jnp.exp(g_f)) * s_f + jnp.concatenate(uf, axis=0)
        return s_f

    sff_ref[0] = lax.fori_loop(0, n_blk, pass1, s0f_ref[0])

    def pass2(t, s_b):
        i = n_blk - 1 - t
        for g in reversed(range(G)):
            c = i * G + g
            rows = pl.ds(pl.multiple_of(i * R + g * C, C), C)
            spf = spf_ref[c]
            snb = s_b.astype(BF16)
            zero_blk = jnp.zeros((GLA_DK, GLA_DV), BF16)

            def pair_state(s, p):
                top = jnp.concatenate([s[2 * p * GLA_DK:(2 * p + 1) * GLA_DK], zero_blk], axis=1)
                bot = jnp.concatenate([zero_blk, s[(2 * p + 1) * GLA_DK:(2 * p + 2) * GLA_DK]], axis=1)
                return jnp.concatenate([top, bot], axis=0)

            outs = []
            for p in range(GLA_HEADS // 2):
                ks = slice(p * PK, (p + 1) * PK)
                vbd = jnp.where(val_diag, jnp.concatenate([v_pair(rows, p)] * 2, axis=0), 0)
                lhs = jnp.concatenate([p_ref[rows, ks], qif_ref[rows, ks], qib_ref[rows, ks]], axis=1)
                rhs = jnp.concatenate([vbd, pair_state(spf, p), pair_state(snb, p)], axis=0)
                o_pair = jnp.dot(lhs, rhs, preferred_element_type=F32)
                for o in (o_pair[:, :GLA_DV], o_pair[:, GLA_DV:]):
                    ms = jnp.mean(o * o, axis=-1, keepdims=True)
                    outs.append(o * lax.rsqrt(ms + EPS))
            o_all = jnp.concatenate(outs, axis=-1) * gnw_ref[...]
            og_ref[0, rows, :] = (o_all * _silu(r_ref[0, rows, :].astype(F32))).astype(BF16)
            s_b = col_bcast(jnp.exp(gb_ref[c][0:1])) * s_b + ub_ref[c]
        return s_b

    sfb_ref[0] = lax.fori_loop(0, n_blk, pass2, s0b_ref[0])


def _gla(qkv, afab, r, s0f, s0b, wab, bab, gnw):
    B, L, _ = qkv.shape
    n = L // CHUNK
    tok = lambda w: pl.BlockSpec((1, L, w), lambda b: (b, 0, 0))
    st = pl.BlockSpec((1, GLA_QK, GLA_DV), lambda b: (b, 0, 0))
    full = lambda a: pl.BlockSpec(a.shape, lambda b: (0,) * a.ndim)
    small = [wab, bab, gnw.reshape(1, GLA_V)]
    return pl.pallas_call(
        _gla_kernel,
        grid=(B,),
        in_specs=[tok(P_R - P_QKV), tok(AFAB_W), tok(GLA_V), st, st] + [full(a) for a in small],
        out_specs=[tok(GLA_V), st, st],
        out_shape=[jax.ShapeDtypeStruct((B, L, GLA_V), BF16),
                   jax.ShapeDtypeStruct((B, GLA_QK, GLA_DV), F32),
                   jax.ShapeDtypeStruct((B, GLA_QK, GLA_DV), F32)],
        scratch_shapes=[pltpu.VMEM((L, GLA_QK), BF16),
                        pltpu.VMEM((L, GLA_QK), BF16), pltpu.VMEM((L, GLA_QK), BF16),
                        pltpu.VMEM((n, GLA_QK, GLA_DV), BF16),
                        pltpu.VMEM((n, GLA_QK, GLA_DV), F32),
                        pltpu.VMEM((n, 8, GLA_QK), F32)],
        compiler_params=_params(("parallel",)),
        name="gla",
    )(qkv, afab, r, s0f, s0b, *small)


def _outproj_kernel(yc_ref, og_ref, ys_ref, x_ref, wo_ref, g2_ref, nw_ref, sh_ref, sc_ref, rwt_ref,
                    xo_ref, h2_ref, lt_ref):
    y = jnp.dot(yc_ref[0], wo_ref[0:D_CONF, :], preferred_element_type=F32)
    y = y + jnp.dot(og_ref[0], wo_ref[D_CONF:D_CONF + GLA_V, :], preferred_element_type=F32)
    y = y + jnp.dot(ys_ref[0], wo_ref[D_CONF + GLA_V:, :], preferred_element_type=F32)
    x = x_ref[0] + g2_ref[0] * y
    xo_ref[0] = x
    hb = _rms_mod(x, nw_ref[...], sh_ref[0], sc_ref[0]).astype(BF16)
    h2_ref[0] = hb
    lt_ref[0] = lax.dot_general(rwt_ref[...], hb, (((1,), (1,)), ((), ())),
                                preferred_element_type=F32)


def _outproj(yc, og, ys, x, wo, g2, nw, shift, scale, rwt):
    B, L, D = x.shape
    tm = min(L, TOKEN_BLOCK)
    tok = lambda w: pl.BlockSpec((1, tm, w), lambda b, t: (b, t, 0))
    per_b = pl.BlockSpec((1, 1, D), lambda b, t: (b, 0, 0))
    return pl.pallas_call(
        _outproj_kernel,
        grid=(B, L // tm),
        in_specs=[tok(D_CONF), tok(GLA_V), tok(D_SC), tok(D),
                  pl.BlockSpec((D, D), lambda b, t: (0, 0)), per_b,
                  pl.BlockSpec((1, D), lambda b, t: (0, 0)), per_b, per_b,
                  pl.BlockSpec((N_EXPERTS, D), lambda b, t: (0, 0))],
        out_specs=[tok(D), tok(D), pl.BlockSpec((1, N_EXPERTS, tm), lambda b, t: (b, 0, t))],
        out_shape=[jax.ShapeDtypeStruct((B, L, D), F32), jax.ShapeDtypeStruct((B, L, D), BF16),
                   jax.ShapeDtypeStruct((B, N_EXPERTS, L), F32)],
        compiler_params=_params(("parallel", "parallel")),
        name="outproj",
    )(yc, og, ys, x, wo, g2, nw.reshape(1, D), shift, scale, rwt)


def _lane_cumsum(flags):
    rows, n = flags.shape
    i = lax.broadcasted_iota(jnp.int32, (LANES, LANES), 0)
    j = lax.broadcasted_iota(jnp.int32, (LANES, LANES), 1)
    upper = jnp.where(i <= j, 1.0, 0.0).astype(BF16)
    xb = flags.astype(F32).astype(BF16)
    out, offset = [], jnp.zeros((rows, 1), F32)
    for blk in range(n // LANES):
        part = jnp.dot(xb[:, blk * LANES:(blk + 1) * LANES], upper, preferred_element_type=F32) + offset
        out.append(part)
        offset = part[:, LANES - 1:LANES]
    return jnp.concatenate(out, axis=1).astype(jnp.int32)


def _route_kernel(lt_ref, pos_ref, gate_ref, bnd_ref, *, cap):
    bs, E, L = lt_ref.shape
    lt = lt_ref[...]
    e = jnp.exp(lt - jnp.max(lt, axis=1, keepdims=True))
    aff = (e / jnp.sum(e, axis=1, keepdims=True)).reshape(bs * E, L)

    def search(i, t):
        cand = t | lax.shift_left(jnp.int32(1), 30 - i)
        cnt = jnp.sum((aff >= pltpu.bitcast(cand, F32)).astype(jnp.int32), axis=1, keepdims=True)
        return jnp.where(cnt >= cap, cand, t)

    thr_bits = lax.fori_loop(0, 31, search, jnp.zeros((bs * E, 1), jnp.int32))
    thr = pltpu.bitcast(thr_bits, F32)
    gt = aff > thr
    eq = aff == thr
    need = cap - jnp.sum(gt.astype(jnp.int32), axis=1, keepdims=True)
    eq_i = eq.astype(jnp.int32)
    eq_rank = _lane_cumsum(eq_i) - eq_i
    sel = gt | (eq & (eq_rank < need))
    sel_i = sel.astype(jnp.int32)
    slot = _lane_cumsum(sel_i) - sel_i
    pos_ref[...] = jnp.where(sel, slot, -1).reshape(bs, E, L)
    gate_ref[...] = aff.reshape(bs, E, L)
    bnd = jnp.concatenate([slot[:, k * TOK_TILE:k * TOK_TILE + 1] for k in range(L // TOK_TILE)], axis=1)
    bnd_ref[...] = bnd.reshape(bs, E, L // TOK_TILE)


def _route(lt, cap):
    B, E, L = lt.shape
    bs = min(B, ROUTE_SAMPLES)
    spec = pl.BlockSpec((bs, E, L), lambda b: (b, 0, 0))
    nt = L // TOK_TILE
    return pl.pallas_call(
        functools.partial(_route_kernel, cap=cap),
        grid=(B // bs,),
        in_specs=[spec], out_specs=[spec, spec, pl.BlockSpec((bs, E, nt), lambda b: (b, 0, 0))],
        out_shape=[jax.ShapeDtypeStruct((B, E, L), jnp.int32), jax.ShapeDtypeStruct((B, E, L), F32),
                   jax.ShapeDtypeStruct((B, E, nt), jnp.int32)],
        compiler_params=_params(("parallel",)),
        name="route",
    )(lt)


def _window(lo, hi, width, cap):
    a0 = jnp.minimum((lo // SLOT_ALIGN) * SLOT_ALIGN, cap - width)
    n = jnp.where(hi > lo, (hi - a0 + width - 1) // width, 0)
    return a0, n


def _one_hot_rows(pos_row, start, width, first_row=None):
    slot = lax.broadcasted_iota(jnp.int32, (width, pos_row.shape[1]), 0) + start
    on = 1.0 if first_row is None else jnp.where(slot >= first_row, 1.0, 0.0)
    return jnp.where(slot == pos_row, on, 0.0).astype(BF16)


def _tile_windows(bnd_ref, pos_ref, b, t, toks, step, n_tiles, cap, width):
    nt = pos_ref.shape[2] // TOK_TILE
    wins = []
    for e in range(N_EXPERTS):
        base = (b * N_EXPERTS + e) * nt
        lo = bnd_ref[base + t * step]
        hi = jnp.where(t + 1 < n_tiles, bnd_ref[base + jnp.minimum(t + 1, n_tiles - 1) * step], cap)
        a0, n = _window(lo, hi, width, cap)
        wins.append((pl.multiple_of(a0, SLOT_ALIGN), n, pos_ref[0, e:e + 1, toks]))
    return wins


def _window_one_hots(wins, w, cap, width):
    ms, starts = [], []
    for a0, n, pos_row in wins:
        if w is None:
            start, first = a0, None
        else:
            first = a0 + w * width
            start = pl.multiple_of(jnp.minimum(first, cap - width), SLOT_ALIGN)
            first = jnp.where(w < n, first, cap)
        ms.append(_one_hot_rows(pos_row, start, width, first))
        starts.append(start)
    return jnp.concatenate(ms, axis=0), starts


def _max_windows(wins):
    return functools.reduce(jnp.maximum, [n for _, n, _ in wins])


def _gather_kernel(bnd_ref, pos_ref, h_ref, xs_ref, *, cap, tile, width):
    b = pl.program_id(0)
    L = h_ref.shape[1]
    step = tile // TOK_TILE
    n_tiles = L // tile
    unroll = min(TILES_PER_ITER, n_tiles)
    xs_ref[...] = jnp.zeros_like(xs_ref)

    def scatter_windows(m_all, starts, toks):
        part = jnp.dot(m_all, h_ref[0, toks, :], preferred_element_type=F32)
        for e, start in enumerate(starts):
            xs_ref[0, e, pl.ds(start, width), :] += part[e * width:(e + 1) * width].astype(BF16)

    def tiles_body(it, carry):
        tiles = []
        for u in range(unroll):
            k = it * unroll + u
            toks = pl.ds(pl.multiple_of(k * tile, tile), tile)
            wins = _tile_windows(bnd_ref, pos_ref, b, k, toks, step, n_tiles, cap, width)
            scatter_windows(*_window_one_hots(wins, None, cap, width), toks)
            tiles.append((toks, wins))
        for toks, wins in tiles:
            def extra(w, c, toks=toks, wins=wins):
                scatter_windows(*_window_one_hots(wins, w, cap, width), toks)
                return c

            lax.fori_loop(1, _max_windows(wins), extra, 0)
        return carry

    lax.fori_loop(0, n_tiles // unroll, tiles_body, 0)


def _gather(bnd, pos, h2, cap):
    B, L, D = h2.shape
    E = N_EXPERTS
    tile = min(L, 256)
    width = min(cap, 64)
    return pl.pallas_call(
        functools.partial(_gather_kernel, cap=cap, tile=tile, width=width),
        grid_spec=pltpu.PrefetchScalarGridSpec(
            num_scalar_prefetch=1, grid=(B,),
            in_specs=[pl.BlockSpec((1, E, L), lambda b, s: (b, 0, 0)),
                      pl.BlockSpec((1, L, D), lambda b, s: (b, 0, 0))],
            out_specs=pl.BlockSpec((1, E, cap, D), lambda b, s: (b, 0, 0, 0))),
        out_shape=jax.ShapeDtypeStruct((B, E, cap, D), BF16),
        compiler_params=_params(("arbitrary",)),
        name="gather",
    )(bnd.reshape(-1), pos, h2)


def _expert_ffn(pos_ref, gate_ref, xs_ref, y_ref, wb_ref):
    n, _, cap, D = xs_ref.shape
    L = pos_ref.shape[2]
    row = pl.ds(pl.program_id(0) % SUBLANES, 1)
    slot = lax.broadcasted_iota(jnp.int32, (cap, L), 0)
    gates = [jnp.sum(jnp.where(slot == pos_ref[i, row, :], gate_ref[i, row, :], 0.0), axis=1, keepdims=True)
             for i in range(n)]
    gate = jnp.concatenate(gates, axis=0)
    xs = xs_ref[:, 0].reshape(n * cap, D)
    hid = _silu(jnp.dot(xs, wb_ref[0], preferred_element_type=F32))
    hid = (hid * jnp.dot(xs, wb_ref[1], preferred_element_type=F32)).astype(BF16)
    y = (jnp.dot(hid, wb_ref[2], preferred_element_type=F32) * gate).astype(BF16)
    y_ref[:, 0] = y.reshape(n, cap, D)


def _ffn_kernel(*refs, with_ctx):
    if with_ctx:
        (pos_ref, gate_ref, xs_ref, cpos_ref, cgate_ref, cxs_ref, w1_ref, w3_ref, w2_ref,
         y_ref, cy_ref, wb_ref) = refs
    else:
        pos_ref, gate_ref, xs_ref, w1_ref, w3_ref, w2_ref, y_ref, wb_ref = refs
    j = pl.program_id(1)

    @pl.when(j == 0)
    def _():
        for i, w_ref in enumerate((w1_ref, w3_ref, w2_ref)):
            wb_ref[i] = w_ref[0, 0].astype(BF16)

    if with_ctx:
        @pl.when(j == 0)
        def _():
            _expert_ffn(cpos_ref, cgate_ref, cxs_ref, cy_ref, wb_ref)

    @pl.when(j >= (1 if with_ctx else 0))
    def _():
        _expert_ffn(pos_ref, gate_ref, xs_ref, y_ref, wb_ref)


def _ffn(lat, ctx, w1, w3, w2, layer, bb):
    pos, gate, xs = lat
    B, E, cap, D = xs.shape
    L = pos.shape[2]
    bb = min(bb, B)
    nb = B // bb
    with_ctx = ctx is not None
    first_lat = 1 if with_ctx else 0
    lat_idx = lambda e, j: (jnp.maximum(j - first_lat, 0), e, 0, 0)
    lat_rows = lambda e, j: (jnp.maximum(j - first_lat, 0), e // SUBLANES, 0)
    ins = [pos, gate, xs]
    in_specs = [pl.BlockSpec((bb, SUBLANES, L), lat_rows), pl.BlockSpec((bb, SUBLANES, L), lat_rows),
                pl.BlockSpec((bb, 1, cap, D), lat_idx)]
    out_shape = [jax.ShapeDtypeStruct((B, E, cap, D), BF16)]
    out_specs = [pl.BlockSpec((bb, 1, cap, D), lat_idx)]
    if with_ctx:
        cpos, cgate, cxs = ctx
        capc, Lc = cxs.shape[2], cpos.shape[2]
        ctx_idx = lambda e, j: (0, e, 0, 0)
        ctx_rows = lambda e, j: (0, e // SUBLANES, 0)
        ins += [cpos, cgate, cxs]
        in_specs += [pl.BlockSpec((B, SUBLANES, Lc), ctx_rows), pl.BlockSpec((B, SUBLANES, Lc), ctx_rows),
                     pl.BlockSpec((B, 1, capc, D), ctx_idx)]
        out_shape.append(jax.ShapeDtypeStruct((B, E, capc, D), BF16))
        out_specs.append(pl.BlockSpec((B, 1, capc, D), ctx_idx))
    wspec = pl.BlockSpec((1, 1, D, D), lambda e, j: (layer, e, 0, 0))
    outs = pl.pallas_call(
        functools.partial(_ffn_kernel, with_ctx=with_ctx),
        grid=(E, nb + (1 if with_ctx else 0)),
        in_specs=in_specs + [wspec, wspec, wspec],
        out_specs=out_specs, out_shape=out_shape,
        scratch_shapes=[pltpu.VMEM((3, D, D), BF16)],
        compiler_params=_params(("arbitrary", "arbitrary")),
        name="ffn",
    )(*ins, w1, w3, w2)
    return outs if with_ctx else (outs[0], None)


def _combine_kernel(bnd_ref, pos_ref, y_ref, x_ref, g_ref, *rest, cap, width, final_norm):
    nw_ref, out_ref = rest if final_norm else (None, rest[0])
    b = pl.program_id(0)
    L = pos_ref.shape[2]
    tile = min(COMB_TILE, out_ref.shape[1])
    step = tile // TOK_TILE
    n_tiles = L // tile
    tiles_per_step = out_ref.shape[1] // tile
    tn = (((0,), (0,)), ((), ()))

    def gathered_sum(m_all, starts):
        y_all = jnp.concatenate([y_ref[0, e, pl.ds(start, width), :] for e, start in enumerate(starts)],
                                axis=0)
        return lax.dot_general(m_all, y_all, tn, preferred_element_type=F32)

    tiles = []
    for i in range(tiles_per_step):
        t = pl.program_id(1) * tiles_per_step + i
        toks = pl.ds(pl.multiple_of(t * tile, tile), tile)
        rows = slice(i * tile, (i + 1) * tile)
        wins = _tile_windows(bnd_ref, pos_ref, b, t, toks, step, n_tiles, cap, width)
        out_ref[0, rows, :] = gathered_sum(*_window_one_hots(wins, None, cap, width))
        tiles.append((rows, wins))
    for rows, wins in tiles:
        def extra(w, c, rows=rows, wins=wins):
            out_ref[0, rows, :] += gathered_sum(*_window_one_hots(wins, w, cap, width))
            return c

        lax.fori_loop(1, _max_windows(wins), extra, 0)
    for rows, _ in tiles:
        x = x_ref[0, rows, :] + g_ref[0] * out_ref[0, rows, :]
        if final_norm:
            x = x * lax.rsqrt(jnp.mean(x * x, axis=-1, keepdims=True) + EPS) * nw_ref[...]
        out_ref[0, rows, :] = x


def _combine(bnd, pos, y, x, g, final_nw):
    B, E, cap, D = y.shape
    L = x.shape[1]
    width = min(cap, 64)
    tm = min(L, TOKEN_BLOCK)
    final_norm = final_nw is not None
    tok = pl.BlockSpec((1, tm, D), lambda b, t, s: (b, t, 0))
    ins = [bnd.reshape(-1), pos, y, x, g]
    in_specs = [pl.BlockSpec((1, E, L), lambda b, t, s: (b, 0, 0)),
                pl.BlockSpec((1, E, cap, D), lambda b, t, s: (b, 0, 0, 0)),
                tok, pl.BlockSpec((1, 1, D), lambda b, t, s: (b, 0, 0))]
    if final_norm:
        ins.append(final_nw.reshape(1, D))
        in_specs.append(pl.BlockSpec((1, D), lambda b, t, s: (0, 0)))
    return pl.pallas_call(
        functools.partial(_combine_kernel, cap=cap, width=width, final_norm=final_norm),
        grid_spec=pltpu.PrefetchScalarGridSpec(
            num_scalar_prefetch=1, grid=(B, L // tm), in_specs=in_specs, out_specs=tok),
        out_shape=jax.ShapeDtypeStruct((B, L, D), F32),
        compiler_params=_params(("arbitrary", "arbitrary")),
        name="combine",
    )(*ins)


def _moe(lat, ctx, w1, w3, w2, layer, bb, final_nw=None):
    routed = []
    for lt, h2, x, g in (lat, ctx) if ctx is not None else (lat,):
        cap = EC_FACTOR * h2.shape[1] // N_EXPERTS
        pos, gate, bnd = _route(lt, cap)
        routed.append((pos, gate, _gather(bnd, pos, h2, cap), bnd, x, g))
    ys = _ffn(routed[0][:3], routed[1][:3] if ctx is not None else None, w1, w3, w2, layer, bb)
    outs = [_combine(bnd, pos, y, x, g, final_nw if i == 0 else None)
            for i, ((pos, _, _, bnd, x, g), y) in enumerate(zip(routed, ys))]
    return outs[0], (outs[1] if ctx is not None else None)


def _permute_w_in(w):
    cols = jnp.concatenate([w[:, 0:OFF_AF], w[:, OFF_R:OFF_SC], w[:, OFF_SC:D_IN], w[:, OFF_AF:OFF_R]],
                           axis=1)
    return jnp.pad(cols, ((0, 0), (0, P_TOTAL - D_IN))).astype(BF16)


def kernel(x, c, ctx, c_ctx, ada_w, ada_b, norm1_w, norm2_w, w_in, conf_dw_w, conf_dw_b, conf_ln_w,
           conf_ln_b, gla_wa_f, gla_ba_f, gla_wa_b, gla_ba_b, gla_gn_w, sc_w, sc_b, w_out, router_w,
           exp_w1, exp_w3, exp_w2, final_norm_w):
    B, L, D = x.shape
    Lc = ctx.shape[1]

    cond = jnp.concatenate([c, c_ctx[None, :], jnp.zeros((7, D), F32)], axis=0)
    mods = _ada(cond, ada_w, ada_b)

    xc = ctx
    zero_state = jnp.zeros((B, GLA_QK, GLA_DV), F32)
    for i in range(DEPTH):
        last = i == DEPTH - 1
        m_lat = [mods[i, :B, j * D:(j + 1) * D].reshape(B, 1, D) for j in range(6)]
        m_ctx = [jnp.broadcast_to(mods[i, B, j * D:(j + 1) * D].reshape(1, 1, D), (B, 1, D))
                 for j in range(6)]
        w_perm = _permute_w_in(w_in[i])
        wo = w_out[i].astype(BF16)
        rwt = router_w[i].T.astype(BF16)
        zero_wa = jnp.zeros((GLA_RANK, GLA_QK), F32)
        wa = jnp.block([[gla_wa_f[i], zero_wa], [zero_wa, gla_wa_b[i]]]).astype(BF16)
        ba = jnp.concatenate([gla_ba_f[i], gla_ba_b[i]]).reshape(1, 2 * GLA_QK)
        experts = (exp_w1, exp_w3, exp_w2, i)
        conv_w = (conf_dw_w[i], conf_dw_b[i], conf_ln_w[i], conf_ln_b[i], sc_w[i], sc_b[i])

        if last:
            c_qkv, c_r, c_afab = _proj(xc, norm1_w[i], m_ctx[0], m_ctx[1], w_perm, ("qkv", "r", "afab"))
        else:
            c_conf, c_qkv, c_r, c_sc, c_afab = _proj(xc, norm1_w[i], m_ctx[0], m_ctx[1], w_perm)
        c_og, c_sf, c_sb = _gla(c_qkv, c_afab, c_r, zero_state, zero_state, wa, ba, gla_gn_w[i])

        conf, qkv, r, scu, afab = _proj(x, norm1_w[i], m_lat[0], m_lat[1], w_perm)
        yc, ys = _convs(conf, scu, *conv_w, row_len=GRID_W)
        og, _, _ = _gla(qkv, afab, r, c_sf, c_sb, wa, ba, gla_gn_w[i])
        x, h2, lt = _outproj(yc, og, ys, x, wo, m_lat[2], norm2_w[i], m_lat[3], m_lat[4], rwt)
        moe_ctx = None
        if not last:
            c_yc, c_ys = _convs(c_conf, c_sc, *conv_w, row_len=Lc)
            xc, c_h2, c_lt = _outproj(c_yc, c_og, c_ys, xc, wo, m_ctx[2], norm2_w[i], m_ctx[3], m_ctx[4],
                                      rwt)
            moe_ctx = (c_lt, c_h2, xc, m_ctx[5])
        x, xc = _moe((lt, h2, x, m_lat[5]), moe_ctx, *experts, FFN_SAMPLES,
                     final_norm_w if last else None)
    return x
```

```python
import functools

import jax
import jax.numpy as jnp
from jax import lax
from jax.experimental import pallas as pl
from jax.experimental.pallas import tpu as pltpu

F32 = jnp.float32
BF16 = jnp.bfloat16
HI = lax.Precision.HIGHEST

D_MODEL = 1024
DEPTH = 2
GRID_W = 64
D_CONF = 256
D_SC = 256
GLA_HEADS = 4
GLA_DK = 64
GLA_DV = 128
GLA_QK = GLA_HEADS * GLA_DK
GLA_V = GLA_HEADS * GLA_DV
GLA_RANK = 16
GLA_TAU = 16.0
CONF_K = 31
SC_K = 3
N_EXPERTS = 16
EC_FACTOR = 2
EPS = 1e-6

OFF_Q = 2 * D_CONF
OFF_K = OFF_Q + GLA_QK
OFF_V = OFF_K + GLA_QK
OFF_AF = OFF_V + GLA_V
OFF_AB = OFF_AF + GLA_RANK
OFF_R = OFF_AB + GLA_RANK
OFF_SC = OFF_R + GLA_V
D_IN = OFF_SC + 3 * D_SC

P_CONF = 0
P_QKV = 512
P_R = 1536
P_SC = 2048
P_AFAB = 2816
P_TOTAL = 2944
AFAB_W = 128

CHUNK = 64
GLA_BLOCK = 16
TOKEN_BLOCK = 1024
CUMSUM_ROWS = 256
CONV_SUB = 64
CONV_TOKENS = 16
SUBLANES = 8
LANES = 128
CONV_PAD = 16
SC_PAD = 8
TOK_TILE = 128
COMB_TILE = 256
TILES_PER_ITER = 4
ROUTE_SAMPLES = 8
FFN_SAMPLES = 4
SLOT_ALIGN = 16
VMEM_LIMIT = 56 * 1024 * 1024


def _sigmoid(x):
    return 1.0 / (1.0 + jnp.exp(-x))


def _silu(x):
    return x * _sigmoid(x)


def _log_sigmoid(z):
    return jnp.minimum(z, 0.0) - jnp.log(1.0 + jnp.exp(-jnp.abs(z)))


def _params(sem):
    return pltpu.CompilerParams(dimension_semantics=sem, vmem_limit_bytes=VMEM_LIMIT)


def _ada_kernel(s_ref, w_ref, b_ref, o_ref):
    s = _silu(s_ref[...])
    o_ref[0] = jnp.dot(s, w_ref[0], precision=HI, preferred_element_type=F32) + b_ref[0]


def _ada(s_in, ada_w, ada_b):
    rows = s_in.shape[0]
    tn = 1024
    return pl.pallas_call(
        _ada_kernel,
        grid=(DEPTH, 6 * D_MODEL // tn),
        in_specs=[pl.BlockSpec((rows, D_MODEL), lambda l, n: (0, 0)),
                  pl.BlockSpec((1, D_MODEL, tn), lambda l, n: (l, 0, n)),
                  pl.BlockSpec((1, 1, tn), lambda l, n: (l, 0, n))],
        out_specs=pl.BlockSpec((1, rows, tn), lambda l, n: (l, 0, n)),
        out_shape=jax.ShapeDtypeStruct((DEPTH, rows, 6 * D_MODEL), F32),
        compiler_params=_params(("arbitrary", "arbitrary")),
        name="ada",
    )(s_in, ada_w, ada_b.reshape(DEPTH, 1, 6 * D_MODEL))


def _rms_mod(x, nw, shift, scale):
    ms = jnp.mean(x * x, axis=-1, keepdims=True)
    return (x * lax.rsqrt(ms + EPS) * nw) * (1.0 + scale) + shift


PROJ_GROUPS = {"conf": (P_CONF, P_QKV, BF16), "qkv": (P_QKV, P_R, BF16), "r": (P_R, P_SC, BF16),
               "sc": (P_SC, P_AFAB, BF16), "afab": (P_AFAB, P_TOTAL, F32)}


def _proj_kernel(x_ref, nw_ref, sh_ref, sc_ref, w_ref, *out_refs, groups):
    hb = _rms_mod(x_ref[0], nw_ref[...], sh_ref[0], sc_ref[0]).astype(BF16)
    for name, o_ref in zip(groups, out_refs):
        lo, hi, dt = PROJ_GROUPS[name]
        o_ref[0] = jnp.dot(hb, w_ref[0, :, lo:hi], preferred_element_type=F32).astype(dt)


def _proj(x, nw, shift, scale, w_perm, layer, groups=tuple(PROJ_GROUPS)):
    B, L, D = x.shape
    tm = min(L, TOKEN_BLOCK)
    tok = lambda w: pl.BlockSpec((1, tm, w), lambda b, t: (b, t, 0))
    per_b = pl.BlockSpec((1, 1, D), lambda b, t: (b, 0, 0))
    widths = [(PROJ_GROUPS[g][1] - PROJ_GROUPS[g][0], PROJ_GROUPS[g][2]) for g in groups]
    return pl.pallas_call(
        functools.partial(_proj_kernel, groups=groups),
        grid=(B, L // tm),
        in_specs=[tok(D), pl.BlockSpec((1, D), lambda b, t: (0, 0)), per_b, per_b,
                  pl.BlockSpec((1, D, P_TOTAL), lambda b, t: (layer, 0, 0))],
        out_specs=[tok(w) for w, _ in widths],
        out_shape=[jax.ShapeDtypeStruct((B, L, w), dt) for w, dt in widths],
        compiler_params=_params(("parallel", "parallel")),
        name="proj",
    )(x, nw.reshape(1, D), shift, scale, w_perm)


def _conv_kernel(conf_ref, scu_ref, cw_ref, cb_ref, lnw_ref, lnb_ref, sw_ref, sb_ref,
                 yc_ref, ys_ref, yt_ref, ot_ref, pz_ref, *, row_len):
    L = conf_ref.shape[1]
    n_rows = L // row_len
    C = D_CONF
    S = SUBLANES

    split_row = n_rows == 1
    assert split_row or n_rows % S == 0
    t_out = row_len // S if split_row else row_len
    span = t_out + 2 * CONV_PAD

    halves = [slice(h * LANES, (h + 1) * LANES) for h in range(C // LANES)]

    def group(g, carry):
        for h in range(len(halves)):
            if not split_row:
                yt_ref[h, 0:CONV_PAD * S, :] = jnp.zeros((CONV_PAD * S, LANES), F32)
                yt_ref[h, (CONV_PAD + t_out) * S:, :] = jnp.zeros((CONV_PAD * S, LANES), F32)
        for k in range(S):
            if split_row:
                tok0 = k * t_out - CONV_PAD
                lo, hi = max(tok0, 0), min(tok0 + span, L)
                src = conf_ref[0, lo:hi, :]
                for a, b in ((0, lo - tok0), (hi - tok0, span)):
                    for h in range(len(halves)):
                        if b > a:
                            yt_ref[h, pl.ds(k + S * a, b - a, stride=S), :] = jnp.zeros((b - a, LANES), F32)
                first = lo - tok0
            else:
                src = conf_ref[0, pl.ds(pl.multiple_of((g * S + k) * row_len, row_len), row_len), :]
                first = CONV_PAD
            u = src.astype(F32)
            z = u[:, :C] * _sigmoid(u[:, C:])
            for h, ls in enumerate(halves):
                yt_ref[h, pl.ds(k + S * first, z.shape[0], stride=S), :] = z[:, ls]

        def out_rows(t):
            return pl.ds(pl.multiple_of(t * S, S), S)

        for h, ls in enumerate(halves):
            def taps(i, c, h=h, ls=ls):
                t0 = i * CONV_TOKENS
                accs = [None] * CONV_TOKENS
                for m in range(CONV_TOKENS + CONF_K - 1):
                    v = yt_ref[h, out_rows(t0 + CONV_PAD - CONF_K // 2 + m), :]
                    for tt in range(max(0, m - CONF_K + 1), min(CONV_TOKENS, m + 1)):
                        term = cw_ref[m - tt:m - tt + 1, ls] * v
                        accs[tt] = term if accs[tt] is None else accs[tt] + term
                for tt in range(CONV_TOKENS):
                    ot_ref[h, out_rows(t0 + tt), :] = accs[tt] + cb_ref[:, ls]
                return c

            lax.fori_loop(0, t_out // CONV_TOKENS, taps, 0)

        for k in range(S):
            dst = k * t_out if split_row else pl.multiple_of((g * S + k) * row_len, row_len)
            o = jnp.concatenate([ot_ref[h, pl.ds(k, t_out, stride=S), :] for h in range(len(halves))], axis=1)
            mu = jnp.mean(o, axis=-1, keepdims=True)
            cen = o - mu
            var = jnp.mean(cen * cen, axis=-1, keepdims=True)
            y = cen * lax.rsqrt(var + EPS) * lnw_ref[...] + lnb_ref[...]
            yc_ref[0, pl.ds(dst, t_out), :] = _silu(y).astype(BF16)
        return carry

    lax.fori_loop(0, 1 if split_row else n_rows // S, group, 0)

    n_blk = L // CONV_SUB
    pz_ref[0:SC_PAD, :] = jnp.zeros((SC_PAD, D_SC), F32)
    pz_ref[SC_PAD + L:, :] = jnp.zeros((SC_PAD, D_SC), F32)

    def fill_p(i, carry):
        start = pl.multiple_of(i * CONV_SUB, CONV_SUB)
        u = scu_ref[0, pl.ds(start, CONV_SUB), :].astype(F32)
        pz_ref[pl.ds(SC_PAD + start, CONV_SUB), :] = u[:, D_SC:2 * D_SC] * u[:, 2 * D_SC:]
        return carry

    lax.fori_loop(0, n_blk, fill_p, 0)

    def sc_blk(i, carry):
        start = pl.multiple_of(i * CONV_SUB, CONV_SUB)
        win = pz_ref[pl.ds(start, CONV_SUB + 2 * SC_PAD), :]
        acc = sb_ref[...] + sw_ref[0:1, :] * win[SC_PAD - 1:SC_PAD - 1 + CONV_SUB]
        acc = acc + sw_ref[1:2, :] * win[SC_PAD:SC_PAD + CONV_SUB]
        acc = acc + sw_ref[2:3, :] * win[SC_PAD + 1:SC_PAD + 1 + CONV_SUB]
        bg = scu_ref[0, pl.ds(start, CONV_SUB), 0:D_SC].astype(F32)
        ys_ref[0, pl.ds(start, CONV_SUB), :] = (bg * acc).astype(BF16)
        return carry

    lax.fori_loop(0, n_blk, sc_blk, 0)


def _convs(conf, scu, cw, cb, lnw, lnb, sw, sb, row_len):
    B, L, _ = conf.shape
    t_out = row_len // SUBLANES if L == row_len else row_len
    full = lambda a: pl.BlockSpec(a.shape, lambda b: (0,) * a.ndim)
    cw_p = jnp.zeros((32, D_CONF), F32).at[:CONF_K].set(cw)
    sw_p = jnp.zeros((8, D_SC), F32).at[:SC_K].set(sw)
    small = [cw_p, cb.reshape(1, D_CONF), lnw.reshape(1, D_CONF), lnb.reshape(1, D_CONF),
             sw_p, sb.reshape(1, D_SC)]
    return pl.pallas_call(
        functools.partial(_conv_kernel, row_len=row_len),
        grid=(B,),
        in_specs=[pl.BlockSpec((1, L, 2 * D_CONF), lambda b: (b, 0, 0)),
                  pl.BlockSpec((1, L, 3 * D_SC), lambda b: (b, 0, 0))] + [full(a) for a in small],
        out_specs=[pl.BlockSpec((1, L, D_CONF), lambda b: (b, 0, 0)),
                   pl.BlockSpec((1, L, D_SC), lambda b: (b, 0, 0))],
        out_shape=[jax.ShapeDtypeStruct((B, L, D_CONF), BF16),
                   jax.ShapeDtypeStruct((B, L, D_SC), BF16)],
        scratch_shapes=[pltpu.VMEM((D_CONF // LANES, (t_out + 2 * CONV_PAD) * SUBLANES, LANES), F32),
                        pltpu.VMEM((D_CONF // LANES, t_out * SUBLANES, LANES), F32),
                        pltpu.VMEM((L + 2 * SC_PAD, D_SC), F32)],
        compiler_params=_params(("parallel",)),
        name="convs",
    )(conf, scu, *small)


def _gla_kernel(qkv_ref, afab_ref, r_ref, s0f_ref, s0b_ref, wab_ref, bab_ref, gnw_ref,
                og_ref, sff_ref, sfb_ref, p_ref, qif_ref, qib_ref, spf_ref, ub_ref, gb_ref):
    L = qkv_ref.shape[1]
    C = CHUNK
    G = min(GLA_BLOCK, L // C)
    R = G * C
    n_blk = L // R
    mid = C // 2
    scale = GLA_DK ** -0.5
    nt = (((1,), (1,)), ((), ()))
    T = min(R, CUMSUM_ROWS)
    ii = lax.broadcasted_iota(jnp.int32, (T, T), 0)
    jj = lax.broadcasted_iota(jnp.int32, (T, T), 1)
    same = (ii // C) == (jj // C)
    tri_f = jnp.where(same, jnp.where(ii >= jj, 1.0, 0.0), 0.0).astype(BF16)
    tri_b = jnp.where(same, jnp.where(jj >= ii, 1.0, 0.0), 0.0).astype(BF16)
    PK = 2 * GLA_DK
    PV = 2 * GLA_DV
    ci = lax.broadcasted_iota(jnp.int32, (C, 2 * C), 0)
    cj = lax.broadcasted_iota(jnp.int32, (C, 2 * C), 1) % C
    lower = ci >= cj
    upper = cj >= ci
    kr = lax.broadcasted_iota(jnp.int32, (2 * C, PK), 0) // C
    kc = lax.broadcasted_iota(jnp.int32, (2 * C, PK), 1) // GLA_DK
    key_diag = kr == kc
    vr = lax.broadcasted_iota(jnp.int32, (2 * C, PV), 0) // C
    vc = lax.broadcasted_iota(jnp.int32, (2 * C, PV), 1) // GLA_DV
    val_diag = vr == vc

    def col_bcast(row_vec):
        return jnp.broadcast_to(row_vec, (GLA_DV, GLA_QK)).T

    def chunk_cumsum(tri, la):
        hi = la.astype(BF16)
        lo = (la - hi.astype(F32)).astype(BF16)
        parts = jnp.concatenate([hi, lo], axis=1)
        both = jnp.concatenate([jnp.dot(tri, parts[s * T:(s + 1) * T], preferred_element_type=F32)
                                for s in range(R // T)], axis=0)
        return both[:, :GLA_QK] + both[:, GLA_QK:]

    def v_pair(rows, p):
        return qkv_ref[0, rows, 2 * GLA_QK + p * PV:2 * GLA_QK + (p + 1) * PV]

    def pass1(i, s_f):
        r0 = pl.multiple_of(i * R, R)
        ab = afab_ref[0, pl.ds(r0, R), 0:2 * GLA_RANK].astype(BF16)
        z = jnp.dot(ab, wab_ref[...], preferred_element_type=F32) + bab_ref[...]
        la = _log_sigmoid(z) * (1.0 / GLA_TAU)
        b_f = chunk_cumsum(tri_f, la[:, :GLA_QK])
        b_b = chunk_cumsum(tri_b, la[:, GLA_QK:])
        q_all = qkv_ref[0, pl.ds(r0, R), 0:GLA_QK].astype(F32) * scale
        k_all = qkv_ref[0, pl.ds(r0, R), GLA_QK:2 * GLA_QK].astype(F32)
        for g in range(G):
            c = i * G + g
            rows = pl.ds(pl.multiple_of(r0 + g * C, C), C)
            sl = slice(g * C, (g + 1) * C)
            q, k = q_all[sl], k_all[sl]
            scaled = []
            for b, tot_row in ((b_f[sl], C - 1), (b_b[sl], 0)):
                ref_row = b[mid:mid + 1]
                tot = b[tot_row:tot_row + 1]
                q_rel = q * jnp.exp(b - ref_row)
                k_rel = k * jnp.exp(ref_row - b)
                scaled.append((q_rel.astype(BF16), k_rel.astype(BF16),
                               (q_rel * jnp.exp(ref_row)).astype(BF16),
                               (k_rel * jnp.exp(tot - ref_row)).T.astype(BF16),
                               tot))
            (qf, kf, qif, kuf_t, g_f), (qb, kb, qib, kub_t, g_b) = scaled
            qif_ref[rows, :] = qif
            qib_ref[rows, :] = qib
            p_parts, uf, ub = [], [], []
            for p in range(GLA_HEADS // 2):
                ks = slice(p * PK, (p + 1) * PK)
                kbd_f = jnp.where(key_diag, jnp.concatenate([kf[:, ks]] * 2, axis=0), 0)
                kbd_b = jnp.where(key_diag, jnp.concatenate([kb[:, ks]] * 2, axis=0), 0)
                s_fwd = lax.dot_general(qf[:, ks], kbd_f, nt, preferred_element_type=F32)
                s_bwd = lax.dot_general(qb[:, ks], kbd_b, nt, preferred_element_type=F32)
                p_parts.append((jnp.where(lower, s_fwd, 0.0) + jnp.where(upper, s_bwd, 0.0)).astype(BF16))
                u = jnp.dot(jnp.concatenate([kuf_t[ks, :], kub_t[ks, :]], axis=0), v_pair(rows, p),
                            preferred_element_type=F32)
                for blk, dst in ((u[:PK], uf), (u[PK:], ub)):
                    dst += [blk[:GLA_DK, :GLA_DV], blk[GLA_DK:, GLA_DV:]]
            p_ref[rows, :] = jnp.concatenate(p_parts, axis=1)
            ub_ref[c] = jnp.concatenate(ub, axis=0)
            gb_ref[c] = jnp.broadcast_to(g_b, (8, GLA_QK))
            spf_ref[c] = s_f.astype(BF16)
            s_f = col_bcast(jnp.exp(g_f)) * s_f + jnp.concatenate(uf, axis=0)
        return s_f

    sff_ref[0] = lax.fori_loop(0, n_blk, pass1, s0f_ref[0])

    def pass2(t, s_b):
        i = n_blk - 1 - t
        for g in reversed(range(G)):
            c = i * G + g
            rows = pl.ds(pl.multiple_of(i * R + g * C, C), C)
            spf = spf_ref[c]
            snb = s_b.astype(BF16)
            zero_blk = jnp.zeros((GLA_DK, GLA_DV), BF16)

            def pair_state(s, p):
                top = jnp.concatenate([s[2 * p * GLA_DK:(2 * p + 1) * GLA_DK], zero_blk], axis=1)
                bot = jnp.concatenate([zero_blk, s[(2 * p + 1) * GLA_DK:(2 * p + 2) * GLA_DK]], axis=1)
                return jnp.concatenate([top, bot], axis=0)

            outs = []
            for p in range(GLA_HEADS // 2):
                ks = slice(p * PK, (p + 1) * PK)
                vbd = jnp.where(val_diag, jnp.concatenate([v_pair(rows, p)] * 2, axis=0), 0)
                lhs = jnp.concatenate([p_ref[rows, ks], qif_ref[rows, ks], qib_ref[rows, ks]], axis=1)
                rhs = jnp.concatenate([vbd, pair_state(spf, p), pair_state(snb, p)], axis=0)
                o_pair = jnp.dot(lhs, rhs, preferred_element_type=F32)
                for o in (o_pair[:, :GLA_DV], o_pair[:, GLA_DV:]):
                    ms = jnp.mean(o * o, axis=-1, keepdims=True)
                    outs.append(o * lax.rsqrt(ms + EPS))
            o_all = jnp.concatenate(outs, axis=-1) * gnw_ref[...]
            og_ref[0, rows, :] = (o_all * _silu(r_ref[0, rows, :].astype(F32))).astype(BF16)
            s_b = col_bcast(jnp.exp(gb_ref[c][0:1])) * s_b + ub_ref[c]
        return s_b

    sfb_ref[0] = lax.fori_loop(0, n_blk, pass2, s0b_ref[0])


def _gla(qkv, afab, r, s0f, s0b, wab, bab, gnw):
    B, L, _ = qkv.shape
    n = L // CHUNK
    tok = lambda w: pl.BlockSpec((1, L, w), lambda b: (b, 0, 0))
    st = pl.BlockSpec((1, GLA_QK, GLA_DV), lambda b: (b, 0, 0))
    full = lambda a: pl.BlockSpec(a.shape, lambda b: (0,) * a.ndim)
    small = [wab, bab, gnw.reshape(1, GLA_V)]
    return pl.pallas_call(
        _gla_kernel,
        grid=(B,),
        in_specs=[tok(P_R - P_QKV), tok(AFAB_W), tok(GLA_V), st, st] + [full(a) for a in small],
        out_specs=[tok(GLA_V), st, st],
        out_shape=[jax.ShapeDtypeStruct((B, L, GLA_V), BF16),
                   jax.ShapeDtypeStruct((B, GLA_QK, GLA_DV), F32),
                   jax.ShapeDtypeStruct((B, GLA_QK, GLA_DV), F32)],
        scratch_shapes=[pltpu.VMEM((L, GLA_QK), BF16),
                        pltpu.VMEM((L, GLA_QK), BF16), pltpu.VMEM((L, GLA_QK), BF16),
                        pltpu.VMEM((n, GLA_QK, GLA_DV), BF16),
                        pltpu.VMEM((n, GLA_QK, GLA_DV), F32),
                        pltpu.VMEM((n, 8, GLA_QK), F32)],
        compiler_params=_params(("parallel",)),
        name="gla",
    )(qkv, afab, r, s0f, s0b, *small)


def _outproj_kernel(yc_ref, og_ref, ys_ref, x_ref, wo_ref, g2_ref, nw_ref, sh_ref, sc_ref, rwt_ref,
                    xo_ref, h2_ref, lt_ref):
    y = jnp.dot(yc_ref[0], wo_ref[0, 0:D_CONF, :], preferred_element_type=F32)
    y = y + jnp.dot(og_ref[0], wo_ref[0, D_CONF:D_CONF + GLA_V, :], preferred_element_type=F32)
    y = y + jnp.dot(ys_ref[0], wo_ref[0, D_CONF + GLA_V:, :], preferred_element_type=F32)
    x = x_ref[0] + g2_ref[0] * y
    xo_ref[0] = x
    hb = _rms_mod(x, nw_ref[...], sh_ref[0], sc_ref[0]).astype(BF16)
    h2_ref[0] = hb
    lt_ref[0] = lax.dot_general(rwt_ref[...], hb, (((1,), (1,)), ((), ())),
                                preferred_element_type=F32)


def _outproj(yc, og, ys, x, wo, layer, g2, nw, shift, scale, rwt):
    B, L, D = x.shape
    tm = min(L, TOKEN_BLOCK)
    tok = lambda w: pl.BlockSpec((1, tm, w), lambda b, t: (b, t, 0))
    per_b = pl.BlockSpec((1, 1, D), lambda b, t: (b, 0, 0))
    return pl.pallas_call(
        _outproj_kernel,
        grid=(B, L // tm),
        in_specs=[tok(D_CONF), tok(GLA_V), tok(D_SC), tok(D),
                  pl.BlockSpec((1, D, D), lambda b, t: (layer, 0, 0)), per_b,
                  pl.BlockSpec((1, D), lambda b, t: (0, 0)), per_b, per_b,
                  pl.BlockSpec((N_EXPERTS, D), lambda b, t: (0, 0))],
        out_specs=[tok(D), tok(D), pl.BlockSpec((1, N_EXPERTS, tm), lambda b, t: (b, 0, t))],
        out_shape=[jax.ShapeDtypeStruct((B, L, D), F32), jax.ShapeDtypeStruct((B, L, D), BF16),
                   jax.ShapeDtypeStruct((B, N_EXPERTS, L), F32)],
        compiler_params=_params(("parallel", "parallel")),
        name="outproj",
    )(yc, og, ys, x, wo, g2, nw.reshape(1, D), shift, scale, rwt)


def _lane_cumsum(flags):
    rows, n = flags.shape
    i = lax.broadcasted_iota(jnp.int32, (LANES, LANES), 0)
    j = lax.broadcasted_iota(jnp.int32, (LANES, LANES), 1)
    upper = jnp.where(i <= j, 1.0, 0.0).astype(BF16)
    xb = flags.astype(F32).astype(BF16)
    out, offset = [], jnp.zeros((rows, 1), F32)
    for blk in range(n // LANES):
        part = jnp.dot(xb[:, blk * LANES:(blk + 1) * LANES], upper, preferred_element_type=F32) + offset
        out.append(part)
        offset = part[:, LANES - 1:LANES]
    return jnp.concatenate(out, axis=1).astype(jnp.int32)


def _route_kernel(lt_ref, pos_ref, gate_ref, bnd_ref, *, cap):
    bs, E, L = lt_ref.shape
    lt = lt_ref[...]
    e = jnp.exp(lt - jnp.max(lt, axis=1, keepdims=True))
    aff = (e / jnp.sum(e, axis=1, keepdims=True)).reshape(bs * E, L)

    def search(i, t):
        cand = t | lax.shift_left(jnp.int32(1), 30 - i)
        cnt = jnp.sum((aff >= pltpu.bitcast(cand, F32)).astype(jnp.int32), axis=1, keepdims=True)
        return jnp.where(cnt >= cap, cand, t)

    thr_bits = lax.fori_loop(0, 31, search, jnp.zeros((bs * E, 1), jnp.int32))
    thr = pltpu.bitcast(thr_bits, F32)
    gt = aff > thr
    eq = aff == thr
    need = cap - jnp.sum(gt.astype(jnp.int32), axis=1, keepdims=True)
    eq_i = eq.astype(jnp.int32)
    eq_rank = _lane_cumsum(eq_i) - eq_i
    sel = gt | (eq & (eq_rank < need))
    sel_i = sel.astype(jnp.int32)
    slot = _lane_cumsum(sel_i) - sel_i
    pos_ref[...] = jnp.where(sel, slot, -1).reshape(bs, E, L)
    gate_ref[...] = aff.reshape(bs, E, L)
    bnd = jnp.concatenate([slot[:, k * TOK_TILE:k * TOK_TILE + 1] for k in range(L // TOK_TILE)], axis=1)
    bnd_ref[...] = bnd.reshape(bs, E, L // TOK_TILE)


def _route(lt, cap):
    B, E, L = lt.shape
    bs = min(B, ROUTE_SAMPLES)
    spec = pl.BlockSpec((bs, E, L), lambda b: (b, 0, 0))
    nt = L // TOK_TILE
    return pl.pallas_call(
        functools.partial(_route_kernel, cap=cap),
        grid=(B // bs,),
        in_specs=[spec], out_specs=[spec, spec, pl.BlockSpec((bs, E, nt), lambda b: (b, 0, 0))],
        out_shape=[jax.ShapeDtypeStruct((B, E, L), jnp.int32), jax.ShapeDtypeStruct((B, E, L), F32),
                   jax.ShapeDtypeStruct((B, E, nt), jnp.int32)],
        compiler_params=_params(("parallel",)),
        name="route",
    )(lt)


def _window(lo, hi, width, cap):
    a0 = jnp.minimum((lo // SLOT_ALIGN) * SLOT_ALIGN, cap - width)
    n = jnp.where(hi > lo, (hi - a0 + width - 1) // width, 0)
    return a0, n


def _one_hot_rows(pos_row, start, width, first_row=None):
    slot = lax.broadcasted_iota(jnp.int32, (width, pos_row.shape[1]), 0) + start
    on = 1.0 if first_row is None else jnp.where(slot >= first_row, 1.0, 0.0)
    return jnp.where(slot == pos_row, on, 0.0).astype(BF16)


def _tile_windows(bnd_ref, pos_ref, b, t, toks, step, n_tiles, cap, width):
    nt = pos_ref.shape[2] // TOK_TILE
    wins = []
    for e in range(N_EXPERTS):
        base = (b * N_EXPERTS + e) * nt
        lo = bnd_ref[base + t * step]
        hi = jnp.where(t + 1 < n_tiles, bnd_ref[base + jnp.minimum(t + 1, n_tiles - 1) * step], cap)
        a0, n = _window(lo, hi, width, cap)
        wins.append((pl.multiple_of(a0, SLOT_ALIGN), n, pos_ref[0, e:e + 1, toks]))
    return wins


def _window_one_hots(wins, w, cap, width):
    ms, starts = [], []
    for a0, n, pos_row in wins:
        if w is None:
            start, first = a0, None
        else:
            first = a0 + w * width
            start = pl.multiple_of(jnp.minimum(first, cap - width), SLOT_ALIGN)
            first = jnp.where(w < n, first, cap)
        ms.append(_one_hot_rows(pos_row, start, width, first))
        starts.append(start)
    return jnp.concatenate(ms, axis=0), starts


def _max_windows(wins):
    return functools.reduce(jnp.maximum, [n for _, n, _ in wins])


def _gather_kernel(bnd_ref, pos_ref, h_ref, xs_ref, *, cap, tile, width):
    b = pl.program_id(0)
    L = h_ref.shape[1]
    step = tile // TOK_TILE
    n_tiles = L // tile
    unroll = min(TILES_PER_ITER, n_tiles)
    xs_ref[...] = jnp.zeros_like(xs_ref)

    def scatter_windows(m_all, starts, toks):
        part = jnp.dot(m_all, h_ref[0, toks, :], preferred_element_type=F32)
        for e, start in enumerate(starts):
            xs_ref[0, e, pl.ds(start, width), :] += part[e * width:(e + 1) * width].astype(BF16)

    def tiles_body(it, carry):
        tiles = []
        for u in range(unroll):
            k = it * unroll + u
            toks = pl.ds(pl.multiple_of(k * tile, tile), tile)
            wins = _tile_windows(bnd_ref, pos_ref, b, k, toks, step, n_tiles, cap, width)
            scatter_windows(*_window_one_hots(wins, None, cap, width), toks)
            tiles.append((toks, wins))
        for toks, wins in tiles:
            def extra(w, c, toks=toks, wins=wins):
                scatter_windows(*_window_one_hots(wins, w, cap, width), toks)
                return c

            lax.fori_loop(1, _max_windows(wins), extra, 0)
        return carry

    lax.fori_loop(0, n_tiles // unroll, tiles_body, 0)


def _gather(bnd, pos, h2, cap):
    B, L, D = h2.shape
    E = N_EXPERTS
    tile = min(L, 256)
    width = min(cap, 64)
    return pl.pallas_call(
        functools.partial(_gather_kernel, cap=cap, tile=tile, width=width),
        grid_spec=pltpu.PrefetchScalarGridSpec(
            num_scalar_prefetch=1, grid=(B,),
            in_specs=[pl.BlockSpec((1, E, L), lambda b, s: (b, 0, 0)),
                      pl.BlockSpec((1, L, D), lambda b, s: (b, 0, 0))],
            out_specs=pl.BlockSpec((1, E, cap, D), lambda b, s: (b, 0, 0, 0))),
        out_shape=jax.ShapeDtypeStruct((B, E, cap, D), BF16),
        compiler_params=_params(("arbitrary",)),
        name="gather",
    )(bnd.reshape(-1), pos, h2)


def _expert_ffn(pos_ref, gate_ref, xs_ref, y_ref, wb_ref):
    n, _, cap, D = xs_ref.shape
    L = pos_ref.shape[2]
    row = pl.ds(pl.program_id(0) % SUBLANES, 1)
    slot = lax.broadcasted_iota(jnp.int32, (cap, L), 0)
    gates = [jnp.sum(jnp.where(slot == pos_ref[i, row, :], gate_ref[i, row, :], 0.0), axis=1, keepdims=True)
             for i in range(n)]
    gate = jnp.concatenate(gates, axis=0)
    xs = xs_ref[:, 0].reshape(n * cap, D)
    hid = _silu(jnp.dot(xs, wb_ref[0], preferred_element_type=F32))
    hid = (hid * jnp.dot(xs, wb_ref[1], preferred_element_type=F32)).astype(BF16)
    y = (jnp.dot(hid, wb_ref[2], preferred_element_type=F32) * gate).astype(BF16)
    y_ref[:, 0] = y.reshape(n, cap, D)


def _ffn_kernel(*refs, with_ctx):
    if with_ctx:
        (pos_ref, gate_ref, xs_ref, cpos_ref, cgate_ref, cxs_ref, w1_ref, w3_ref, w2_ref,
         y_ref, cy_ref, wb_ref) = refs
    else:
        pos_ref, gate_ref, xs_ref, w1_ref, w3_ref, w2_ref, y_ref, wb_ref = refs
    j = pl.program_id(1)

    @pl.when(j == 0)
    def _():
        for i, w_ref in enumerate((w1_ref, w3_ref, w2_ref)):
            wb_ref[i] = w_ref[0, 0].astype(BF16)

    if with_ctx:
        @pl.when(j == 0)
        def _():
            _expert_ffn(cpos_ref, cgate_ref, cxs_ref, cy_ref, wb_ref)

    @pl.when(j >= (1 if with_ctx else 0))
    def _():
        _expert_ffn(pos_ref, gate_ref, xs_ref, y_ref, wb_ref)


def _ffn(lat, ctx, w1, w3, w2, layer, bb):
    pos, gate, xs = lat
    B, E, cap, D = xs.shape
    L = pos.shape[2]
    bb = min(bb, B)
    nb = B // bb
    with_ctx = ctx is not None
    first_lat = 1 if with_ctx else 0
    lat_idx = lambda e, j: (jnp.maximum(j - first_lat, 0), e, 0, 0)
    lat_rows = lambda e, j: (jnp.maximum(j - first_lat, 0), e // SUBLANES, 0)
    ins = [pos, gate, xs]
    in_specs = [pl.BlockSpec((bb, SUBLANES, L), lat_rows), pl.BlockSpec((bb, SUBLANES, L), lat_rows),
                pl.BlockSpec((bb, 1, cap, D), lat_idx)]
    out_shape = [jax.ShapeDtypeStruct((B, E, cap, D), BF16)]
    out_specs = [pl.BlockSpec((bb, 1, cap, D), lat_idx)]
    if with_ctx:
        cpos, cgate, cxs = ctx
        capc, Lc = cxs.shape[2], cpos.shape[2]
        ctx_idx = lambda e, j: (0, e, 0, 0)
        ctx_rows = lambda e, j: (0, e // SUBLANES, 0)
        ins += [cpos, cgate, cxs]
        in_specs += [pl.BlockSpec((B, SUBLANES, Lc), ctx_rows), pl.BlockSpec((B, SUBLANES, Lc), ctx_rows),
                     pl.BlockSpec((B, 1, capc, D), ctx_idx)]
        out_shape.append(jax.ShapeDtypeStruct((B, E, capc, D), BF16))
        out_specs.append(pl.BlockSpec((B, 1, capc, D), ctx_idx))
    wspec = pl.BlockSpec((1, 1, D, D), lambda e, j: (layer, e, 0, 0))
    outs = pl.pallas_call(
        functools.partial(_ffn_kernel, with_ctx=with_ctx),
        grid=(E, nb + (1 if with_ctx else 0)),
        in_specs=in_specs + [wspec, wspec, wspec],
        out_specs=out_specs, out_shape=out_shape,
        scratch_shapes=[pltpu.VMEM((3, D, D), BF16)],
        compiler_params=_params(("arbitrary", "arbitrary")),
        name="ffn",
    )(*ins, w1, w3, w2)
    return outs if with_ctx else (outs[0], None)


def _combine_kernel(bnd_ref, pos_ref, y_ref, x_ref, g_ref, *rest, cap, width, final_norm):
    nw_ref, out_ref = rest if final_norm else (None, rest[0])
    b = pl.program_id(0)
    L = pos_ref.shape[2]
    tile = min(COMB_TILE, out_ref.shape[1])
    step = tile // TOK_TILE
    n_tiles = L // tile
    tiles_per_step = out_ref.shape[1] // tile
    tn = (((0,), (0,)), ((), ()))

    def gathered_sum(m_all, starts):
        y_all = jnp.concatenate([y_ref[0, e, pl.ds(start, width), :] for e, start in enumerate(starts)],
                                axis=0)
        return lax.dot_general(m_all, y_all, tn, preferred_element_type=F32)

    tiles = []
    for i in range(tiles_per_step):
        t = pl.program_id(1) * tiles_per_step + i
        toks = pl.ds(pl.multiple_of(t * tile, tile), tile)
        rows = slice(i * tile, (i + 1) * tile)
        wins = _tile_windows(bnd_ref, pos_ref, b, t, toks, step, n_tiles, cap, width)
        out_ref[0, rows, :] = gathered_sum(*_window_one_hots(wins, None, cap, width))
        tiles.append((rows, wins))
    for rows, wins in tiles:
        def extra(w, c, rows=rows, wins=wins):
            out_ref[0, rows, :] += gathered_sum(*_window_one_hots(wins, w, cap, width))
            return c

        lax.fori_loop(1, _max_windows(wins), extra, 0)
    for rows, _ in tiles:
        x = x_ref[0, rows, :] + g_ref[0] * out_ref[0, rows, :]
        if final_norm:
            x = x * lax.rsqrt(jnp.mean(x * x, axis=-1, keepdims=True) + EPS) * nw_ref[...]
        out_ref[0, rows, :] = x


def _combine(bnd, pos, y, x, g, final_nw):
    B, E, cap, D = y.shape
    L = x.shape[1]
    width = min(cap, 64)
    tm = min(L, TOKEN_BLOCK)
    final_norm = final_nw is not None
    tok = pl.BlockSpec((1, tm, D), lambda b, t, s: (b, t, 0))
    ins = [bnd.reshape(-1), pos, y, x, g]
    in_specs = [pl.BlockSpec((1, E, L), lambda b, t, s: (b, 0, 0)),
                pl.BlockSpec((1, E, cap, D), lambda b, t, s: (b, 0, 0, 0)),
                tok, pl.BlockSpec((1, 1, D), lambda b, t, s: (b, 0, 0))]
    if final_norm:
        ins.append(final_nw.reshape(1, D))
        in_specs.append(pl.BlockSpec((1, D), lambda b, t, s: (0, 0)))
    return pl.pallas_call(
        functools.partial(_combine_kernel, cap=cap, width=width, final_norm=final_norm),
        grid_spec=pltpu.PrefetchScalarGridSpec(
            num_scalar_prefetch=1, grid=(B, L // tm), in_specs=in_specs, out_specs=tok),
        out_shape=jax.ShapeDtypeStruct((B, L, D), F32),
        compiler_params=_params(("arbitrary", "arbitrary")),
        name="combine",
    )(*ins)


def _moe(lat, ctx, w1, w3, w2, layer, bb, final_nw=None):
    routed = []
    for lt, h2, x, g in (lat, ctx) if ctx is not None else (lat,):
        cap = EC_FACTOR * h2.shape[1] // N_EXPERTS
        pos, gate, bnd = _route(lt, cap)
        routed.append((pos, gate, _gather(bnd, pos, h2, cap), bnd, x, g))
    ys = _ffn(routed[0][:3], routed[1][:3] if ctx is not None else None, w1, w3, w2, layer, bb)
    outs = [_combine(bnd, pos, y, x, g, final_nw if i == 0 else None)
            for i, ((pos, _, _, bnd, x, g), y) in enumerate(zip(routed, ys))]
    return outs[0], (outs[1] if ctx is not None else None)


def _permute_w_in(w):
    cols = jnp.concatenate([w[..., 0:OFF_AF], w[..., OFF_R:OFF_SC], w[..., OFF_SC:D_IN], w[..., OFF_AF:OFF_R]],
                           axis=-1)
    return jnp.pad(cols, ((0, 0), (0, 0), (0, P_TOTAL - D_IN))).astype(BF16)


def kernel(x, c, ctx, c_ctx, ada_w, ada_b, norm1_w, norm2_w, w_in, conf_dw_w, conf_dw_b, conf_ln_w,
           conf_ln_b, gla_wa_f, gla_ba_f, gla_wa_b, gla_ba_b, gla_gn_w, sc_w, sc_b, w_out, router_w,
           exp_w1, exp_w3, exp_w2, final_norm_w):
    B, L, D = x.shape
    Lc = ctx.shape[1]

    cond = jnp.concatenate([c, c_ctx[None, :], jnp.zeros((7, D), F32)], axis=0)
    mods = _ada(cond, ada_w, ada_b)

    w_perm = _permute_w_in(w_in)
    wo = w_out.astype(BF16)
    xc = ctx
    zero_state = jnp.zeros((B, GLA_QK, GLA_DV), F32)
    for i in range(DEPTH):
        last = i == DEPTH - 1
        m_lat = [mods[i, :B, j * D:(j + 1) * D].reshape(B, 1, D) for j in range(6)]
        m_ctx = [jnp.broadcast_to(mods[i, B, j * D:(j + 1) * D].reshape(1, 1, D), (B, 1, D))
                 for j in range(6)]
        rwt = router_w[i].T.astype(BF16)
        zero_wa = jnp.zeros((GLA_RANK, GLA_QK), F32)
        wa = jnp.block([[gla_wa_f[i], zero_wa], [zero_wa, gla_wa_b[i]]]).astype(BF16)
        ba = jnp.concatenate([gla_ba_f[i], gla_ba_b[i]]).reshape(1, 2 * GLA_QK)
        experts = (exp_w1, exp_w3, exp_w2, i)
        conv_w = (conf_dw_w[i], conf_dw_b[i], conf_ln_w[i], conf_ln_b[i], sc_w[i], sc_b[i])

        if last:
            c_qkv, c_r, c_afab = _proj(xc, norm1_w[i], m_ctx[0], m_ctx[1], w_perm, i, ("qkv", "r", "afab"))
        else:
            c_conf, c_qkv, c_r, c_sc, c_afab = _proj(xc, norm1_w[i], m_ctx[0], m_ctx[1], w_perm, i)
        c_og, c_sf, c_sb = _gla(c_qkv, c_afab, c_r, zero_state, zero_state, wa, ba, gla_gn_w[i])

        conf, qkv, r, scu, afab = _proj(x, norm1_w[i], m_lat[0], m_lat[1], w_perm, i)
        yc, ys = _convs(conf, scu, *conv_w, row_len=GRID_W)
        og, _, _ = _gla(qkv, afab, r, c_sf, c_sb, wa, ba, gla_gn_w[i])
        x, h2, lt = _outproj(yc, og, ys, x, wo, i, m_lat[2], norm2_w[i], m_lat[3], m_lat[4], rwt)
        moe_ctx = None
        if not last:
            c_yc, c_ys = _convs(c_conf, c_sc, *conv_w, row_len=Lc)
            xc, c_h2, c_lt = _outproj(c_yc, c_og, c_ys, xc, wo, i, m_ctx[2], norm2_w[i], m_ctx[3], m_ctx[4],
                                      rwt)
            moe_ctx = (c_lt, c_h2, xc, m_ctx[5])
        x, xc = _moe((lt, h2, x, m_lat[5]), moe_ctx, *experts, FFN_SAMPLES,
                     final_norm_w if last else None)
    return x
```

```python
import functools

import jax
import jax.numpy as jnp
from jax import lax
from jax.experimental import pallas as pl
from jax.experimental.pallas import tpu as pltpu

F32 = jnp.float32
BF16 = jnp.bfloat16
HI = lax.Precision.HIGHEST

D_MODEL = 1024
DEPTH = 2
GRID_W = 64
D_CONF = 256
D_SC = 256
GLA_HEADS = 4
GLA_DK = 64
GLA_DV = 128
GLA_QK = GLA_HEADS * GLA_DK
GLA_V = GLA_HEADS * GLA_DV
GLA_RANK = 16
GLA_TAU = 16.0
CONF_K = 31
SC_K = 3
N_EXPERTS = 16
EC_FACTOR = 2
EPS = 1e-6

OFF_Q = 2 * D_CONF
OFF_K = OFF_Q + GLA_QK
OFF_V = OFF_K + GLA_QK
OFF_AF = OFF_V + GLA_V
OFF_AB = OFF_AF + GLA_RANK
OFF_R = OFF_AB + GLA_RANK
OFF_SC = OFF_R + GLA_V
D_IN = OFF_SC + 3 * D_SC

P_CONF = 0
P_QKV = 512
P_R = 1536
P_SC = 2048
P_AFAB = 2816
P_TOTAL = 2944
AFAB_W = 128

CHUNK = 64
GLA_BLOCK = 16
TOKEN_BLOCK = 1024
CUMSUM_ROWS = 256
CONV_SUB = 64
CONV_TOKENS = 32
SUBLANES = 8
LANES = 128
CONV_PAD = 16
SC_PAD = 8
TOK_TILE = 128
COMB_TILE = 256
TILES_PER_ITER = 4
ROUTE_SAMPLES = 16
FFN_SAMPLES = 4
SLOT_ALIGN = 16
VMEM_LIMIT = 56 * 1024 * 1024


def _sigmoid(x):
    return 1.0 / (1.0 + jnp.exp(-x))


def _silu(x):
    return x * _sigmoid(x)


def _log_sigmoid(z):
    return jnp.minimum(z, 0.0) - jnp.log(1.0 + jnp.exp(-jnp.abs(z)))


def _params(sem):
    return pltpu.CompilerParams(dimension_semantics=sem, vmem_limit_bytes=VMEM_LIMIT)


def _ada_kernel(s_ref, w_ref, b_ref, o_ref):
    s = _silu(s_ref[...])
    o_ref[0] = jnp.dot(s, w_ref[0], precision=HI, preferred_element_type=F32) + b_ref[0]


def _ada(s_in, ada_w, ada_b):
    rows = s_in.shape[0]
    tn = 1024
    return pl.pallas_call(
        _ada_kernel,
        grid=(DEPTH, 6 * D_MODEL // tn),
        in_specs=[pl.BlockSpec((rows, D_MODEL), lambda l, n: (0, 0)),
                  pl.BlockSpec((1, D_MODEL, tn), lambda l, n: (l, 0, n)),
                  pl.BlockSpec((1, 1, tn), lambda l, n: (l, 0, n))],
        out_specs=pl.BlockSpec((1, rows, tn), lambda l, n: (l, 0, n)),
        out_shape=jax.ShapeDtypeStruct((DEPTH, rows, 6 * D_MODEL), F32),
        compiler_params=_params(("arbitrary", "arbitrary")),
        name="ada",
    )(s_in, ada_w, ada_b.reshape(DEPTH, 1, 6 * D_MODEL))


def _rms_mod(x, nw, shift, scale):
    ms = jnp.mean(x * x, axis=-1, keepdims=True)
    return (x * lax.rsqrt(ms + EPS) * nw) * (1.0 + scale) + shift


PROJ_GROUPS = {"conf": (P_CONF, P_QKV, BF16), "qkv": (P_QKV, P_R, BF16), "r": (P_R, P_SC, BF16),
               "sc": (P_SC, P_AFAB, BF16), "afab": (P_AFAB, P_TOTAL, F32)}


def _proj_kernel(x_ref, nw_ref, sh_ref, sc_ref, w_ref, *out_refs, groups):
    hb = _rms_mod(x_ref[0], nw_ref[...], sh_ref[0], sc_ref[0]).astype(BF16)
    for name, o_ref in zip(groups, out_refs):
        lo, hi, dt = PROJ_GROUPS[name]
        o_ref[0] = jnp.dot(hb, w_ref[0, :, lo:hi], preferred_element_type=F32).astype(dt)


def _proj(x, nw, shift, scale, w_perm, layer, groups=tuple(PROJ_GROUPS)):
    B, L, D = x.shape
    tm = min(L, TOKEN_BLOCK)
    tok = lambda w: pl.BlockSpec((1, tm, w), lambda b, t: (b, t, 0))
    per_b = pl.BlockSpec((1, 1, D), lambda b, t: (b, 0, 0))
    widths = [(PROJ_GROUPS[g][1] - PROJ_GROUPS[g][0], PROJ_GROUPS[g][2]) for g in groups]
    return pl.pallas_call(
        functools.partial(_proj_kernel, groups=groups),
        grid=(B, L // tm),
        in_specs=[tok(D), pl.BlockSpec((1, D), lambda b, t: (0, 0)), per_b, per_b,
                  pl.BlockSpec((1, D, P_TOTAL), lambda b, t: (layer, 0, 0))],
        out_specs=[tok(w) for w, _ in widths],
        out_shape=[jax.ShapeDtypeStruct((B, L, w), dt) for w, dt in widths],
        compiler_params=_params(("parallel", "parallel")),
        name="proj",
    )(x, nw.reshape(1, D), shift, scale, w_perm)


def _conv_kernel(conf_ref, scu_ref, cw_ref, cb_ref, lnw_ref, lnb_ref, sw_ref, sb_ref,
                 yc_ref, ys_ref, yt_ref, ot_ref, pz_ref, *, row_len):
    L = conf_ref.shape[1]
    n_rows = L // row_len
    C = D_CONF
    S = SUBLANES

    split_row = n_rows == 1
    assert split_row or n_rows % S == 0
    t_out = row_len // S if split_row else row_len
    span = t_out + 2 * CONV_PAD

    halves = [slice(h * LANES, (h + 1) * LANES) for h in range(C // LANES)]

    def group(g, carry):
        for h in range(len(halves)):
            if not split_row:
                yt_ref[h, 0:CONV_PAD * S, :] = jnp.zeros((CONV_PAD * S, LANES), F32)
                yt_ref[h, (CONV_PAD + t_out) * S:, :] = jnp.zeros((CONV_PAD * S, LANES), F32)
        for k in range(S):
            if split_row:
                tok0 = k * t_out - CONV_PAD
                lo, hi = max(tok0, 0), min(tok0 + span, L)
                src = conf_ref[0, lo:hi, :]
                for a, b in ((0, lo - tok0), (hi - tok0, span)):
                    for h in range(len(halves)):
                        if b > a:
                            yt_ref[h, pl.ds(k + S * a, b - a, stride=S), :] = jnp.zeros((b - a, LANES), F32)
                first = lo - tok0
            else:
                src = conf_ref[0, pl.ds(pl.multiple_of((g * S + k) * row_len, row_len), row_len), :]
                first = CONV_PAD
            u = src.astype(F32)
            z = u[:, :C] * _sigmoid(u[:, C:])
            for h, ls in enumerate(halves):
                yt_ref[h, pl.ds(k + S * first, z.shape[0], stride=S), :] = z[:, ls]

        def out_rows(t):
            return pl.ds(pl.multiple_of(t * S, S), S)

        for h, ls in enumerate(halves):
            def taps(i, c, h=h, ls=ls):
                t0 = i * CONV_TOKENS
                accs = [None] * CONV_TOKENS
                for m in range(CONV_TOKENS + CONF_K - 1):
                    v = yt_ref[h, out_rows(t0 + CONV_PAD - CONF_K // 2 + m), :]
                    for tt in range(max(0, m - CONF_K + 1), min(CONV_TOKENS, m + 1)):
                        term = cw_ref[m - tt:m - tt + 1, ls] * v
                        accs[tt] = term if accs[tt] is None else accs[tt] + term
                for tt in range(CONV_TOKENS):
                    ot_ref[h, out_rows(t0 + tt), :] = accs[tt] + cb_ref[:, ls]
                return c

            lax.fori_loop(0, t_out // CONV_TOKENS, taps, 0)

        for k in range(S):
            dst = k * t_out if split_row else pl.multiple_of((g * S + k) * row_len, row_len)
            o = jnp.concatenate([ot_ref[h, pl.ds(k, t_out, stride=S), :] for h in range(len(halves))], axis=1)
            mu = jnp.mean(o, axis=-1, keepdims=True)
            cen = o - mu
            var = jnp.mean(cen * cen, axis=-1, keepdims=True)
            y = cen * lax.rsqrt(var + EPS) * lnw_ref[...] + lnb_ref[...]
            yc_ref[0, pl.ds(dst, t_out), :] = _silu(y).astype(BF16)
        return carry

    lax.fori_loop(0, 1 if split_row else n_rows // S, group, 0)

    n_blk = L // CONV_SUB
    pz_ref[0:SC_PAD, :] = jnp.zeros((SC_PAD, D_SC), F32)
    pz_ref[SC_PAD + L:, :] = jnp.zeros((SC_PAD, D_SC), F32)

    def fill_p(i, carry):
        start = pl.multiple_of(i * CONV_SUB, CONV_SUB)
        u = scu_ref[0, pl.ds(start, CONV_SUB), :].astype(F32)
        pz_ref[pl.ds(SC_PAD + start, CONV_SUB), :] = u[:, D_SC:2 * D_SC] * u[:, 2 * D_SC:]
        return carry

    lax.fori_loop(0, n_blk, fill_p, 0)

    def sc_blk(i, carry):
        start = pl.multiple_of(i * CONV_SUB, CONV_SUB)
        win = pz_ref[pl.ds(start, CONV_SUB + 2 * SC_PAD), :]
        acc = sb_ref[...] + sw_ref[0:1, :] * win[SC_PAD - 1:SC_PAD - 1 + CONV_SUB]
        acc = acc + sw_ref[1:2, :] * win[SC_PAD:SC_PAD + CONV_SUB]
        acc = acc + sw_ref[2:3, :] * win[SC_PAD + 1:SC_PAD + 1 + CONV_SUB]
        bg = scu_ref[0, pl.ds(start, CONV_SUB), 0:D_SC].astype(F32)
        ys_ref[0, pl.ds(start, CONV_SUB), :] = (bg * acc).astype(BF16)
        return carry

    lax.fori_loop(0, n_blk, sc_blk, 0)


def _convs(conf, scu, cw, cb, lnw, lnb, sw, sb, row_len):
    B, L, _ = conf.shape
    t_out = row_len // SUBLANES if L == row_len else row_len
    full = lambda a: pl.BlockSpec(a.shape, lambda b: (0,) * a.ndim)
    cw_p = jnp.zeros((32, D_CONF), F32).at[:CONF_K].set(cw)
    sw_p = jnp.zeros((8, D_SC), F32).at[:SC_K].set(sw)
    small = [cw_p, cb.reshape(1, D_CONF), lnw.reshape(1, D_CONF), lnb.reshape(1, D_CONF),
             sw_p, sb.reshape(1, D_SC)]
    return pl.pallas_call(
        functools.partial(_conv_kernel, row_len=row_len),
        grid=(B,),
        in_specs=[pl.BlockSpec((1, L, 2 * D_CONF), lambda b: (b, 0, 0)),
                  pl.BlockSpec((1, L, 3 * D_SC), lambda b: (b, 0, 0))] + [full(a) for a in small],
        out_specs=[pl.BlockSpec((1, L, D_CONF), lambda b: (b, 0, 0)),
                   pl.BlockSpec((1, L, D_SC), lambda b: (b, 0, 0))],
        out_shape=[jax.ShapeDtypeStruct((B, L, D_CONF), BF16),
                   jax.ShapeDtypeStruct((B, L, D_SC), BF16)],
        scratch_shapes=[pltpu.VMEM((D_CONF // LANES, (t_out + 2 * CONV_PAD) * SUBLANES, LANES), F32),
                        pltpu.VMEM((D_CONF // LANES, t_out * SUBLANES, LANES), F32),
                        pltpu.VMEM((L + 2 * SC_PAD, D_SC), F32)],
        compiler_params=_params(("parallel",)),
        name="convs",
    )(conf, scu, *small)


def _gla_kernel(qkv_ref, afab_ref, r_ref, s0f_ref, s0b_ref, wab_ref, bab_ref, gnw_ref,
                og_ref, sff_ref, sfb_ref, p_ref, qif_ref, qib_ref, spf_ref, ub_ref, gb_ref):
    L = qkv_ref.shape[1]
    C = CHUNK
    G = min(GLA_BLOCK, L // C)
    R = G * C
    n_blk = L // R
    mid = C // 2
    scale = GLA_DK ** -0.5
    nt = (((1,), (1,)), ((), ()))
    T = min(R, CUMSUM_ROWS)
    ii = lax.broadcasted_iota(jnp.int32, (T, T), 0)
    jj = lax.broadcasted_iota(jnp.int32, (T, T), 1)
    same = (ii // C) == (jj // C)
    tri_f = jnp.where(same, jnp.where(ii >= jj, 1.0, 0.0), 0.0).astype(BF16)
    tri_b = jnp.where(same, jnp.where(jj >= ii, 1.0, 0.0), 0.0).astype(BF16)
    PK = 2 * GLA_DK
    PV = 2 * GLA_DV
    ci = lax.broadcasted_iota(jnp.int32, (C, 2 * C), 0)
    cj = lax.broadcasted_iota(jnp.int32, (C, 2 * C), 1) % C
    lower = ci >= cj
    upper = cj >= ci
    kr = lax.broadcasted_iota(jnp.int32, (2 * C, PK), 0) // C
    kc = lax.broadcasted_iota(jnp.int32, (2 * C, PK), 1) // GLA_DK
    key_diag = kr == kc
    vr = lax.broadcasted_iota(jnp.int32, (2 * C, PV), 0) // C
    vc = lax.broadcasted_iota(jnp.int32, (2 * C, PV), 1) // GLA_DV
    val_diag = vr == vc

    def col_bcast(row_vec):
        return jnp.broadcast_to(row_vec, (GLA_DV, GLA_QK)).T

    def chunk_cumsum(tri, la):
        hi = la.astype(BF16)
        lo = (la - hi.astype(F32)).astype(BF16)
        parts = jnp.concatenate([hi, lo], axis=1)
        both = jnp.concatenate([jnp.dot(tri, parts[s * T:(s + 1) * T], preferred_element_type=F32)
                                for s in range(R // T)], axis=0)
        return both[:, :GLA_QK] + both[:, GLA_QK:]

    def v_pair(rows, p):
        return qkv_ref[0, rows, 2 * GLA_QK + p * PV:2 * GLA_QK + (p + 1) * PV]

    def pass1(i, s_f):
        r0 = pl.multiple_of(i * R, R)
        ab = afab_ref[0, pl.ds(r0, R), 0:2 * GLA_RANK].astype(BF16)
        z = jnp.dot(ab, wab_ref[...], preferred_element_type=F32) + bab_ref[...]
        la = _log_sigmoid(z) * (1.0 / GLA_TAU)
        b_f = chunk_cumsum(tri_f, la[:, :GLA_QK])
        b_b = chunk_cumsum(tri_b, la[:, GLA_QK:])
        q_all = qkv_ref[0, pl.ds(r0, R), 0:GLA_QK].astype(F32) * scale
        k_all = qkv_ref[0, pl.ds(r0, R), GLA_QK:2 * GLA_QK].astype(F32)
        for g in range(G):
            c = i * G + g
            rows = pl.ds(pl.multiple_of(r0 + g * C, C), C)
            sl = slice(g * C, (g + 1) * C)
            q, k = q_all[sl], k_all[sl]
            scaled = []
            for b, tot_row in ((b_f[sl], C - 1), (b_b[sl], 0)):
                ref_row = b[mid:mid + 1]
                tot = b[tot_row:tot_row + 1]
                q_rel = q * jnp.exp(b - ref_row)
                k_rel = k * jnp.exp(ref_row - b)
                scaled.append((q_rel.astype(BF16), k_rel.astype(BF16),
                               (q_rel * jnp.exp(ref_row)).astype(BF16),
                               (k_rel * jnp.exp(tot - ref_row)).T.astype(BF16),
                               tot))
            (qf, kf, qif, kuf_t, g_f), (qb, kb, qib, kub_t, g_b) = scaled
            qif_ref[rows, :] = qif
            qib_ref[rows, :] = qib
            p_parts, uf, ub = [], [], []
            for p in range(GLA_HEADS // 2):
                ks = slice(p * PK, (p + 1) * PK)
                kbd_f = jnp.where(key_diag, jnp.concatenate([kf[:, ks]] * 2, axis=0), 0)
                kbd_b = jnp.where(key_diag, jnp.concatenate([kb[:, ks]] * 2, axis=0), 0)
                s_fwd = lax.dot_general(qf[:, ks], kbd_f, nt, preferred_element_type=F32)
                s_bwd = lax.dot_general(qb[:, ks], kbd_b, nt, preferred_element_type=F32)
                p_parts.append((jnp.where(lower, s_fwd, 0.0) + jnp.where(upper, s_bwd, 0.0)).astype(BF16))
                u = jnp.dot(jnp.concatenate([kuf_t[ks, :], kub_t[ks, :]], axis=0), v_pair(rows, p),
                            preferred_element_type=F32)
                for blk, dst in ((u[:PK], uf), (u[PK:], ub)):
                    dst += [blk[:GLA_DK, :GLA_DV], blk[GLA_DK:, GLA_DV:]]
            p_ref[rows, :] = jnp.concatenate(p_parts, axis=1)
            ub_ref[c] = jnp.concatenate(ub, axis=0)
            gb_ref[c] = jnp.broadcast_to(g_b, (8, GLA_QK))
            spf_ref[c] = s_f.astype(BF16)
            s_f = col_bcast(jnp.exp(g_f)) * s_f + jnp.concatenate(uf, axis=0)
        return s_f

    sff_ref[0] = lax.fori_loop(0, n_blk, pass1, s0f_ref[0])

    def pass2(t, s_b):
        i = n_blk - 1 - t
        for g in reversed(range(G)):
            c = i * G + g
            rows = pl.ds(pl.multiple_of(i * R + g * C, C), C)
            spf = spf_ref[c]
            snb = s_b.astype(BF16)
            zero_blk = jnp.zeros((GLA_DK, GLA_DV), BF16)

            def pair_state(s, p):
                top = jnp.concatenate([s[2 * p * GLA_DK:(2 * p + 1) * GLA_DK], zero_blk], axis=1)
                bot = jnp.concatenate([zero_blk, s[(2 * p + 1) * GLA_DK:(2 * p + 2) * GLA_DK]], axis=1)
                return jnp.concatenate([top, bot], axis=0)

            outs = []
            for p in range(GLA_HEADS // 2):
                ks = slice(p * PK, (p + 1) * PK)
                vbd = jnp.where(val_diag, jnp.concatenate([v_pair(rows, p)] * 2, axis=0), 0)
                lhs = jnp.concatenate([p_ref[rows, ks], qif_ref[rows, ks], qib_ref[rows, ks]], axis=1)
                rhs = jnp.concatenate([vbd, pair_state(spf, p), pair_state(snb, p)], axis=0)
                o_pair = jnp.dot(lhs, rhs, preferred_element_type=F32)
                for o in (o_pair[:, :GLA_DV], o_pair[:, GLA_DV:]):
                    ms = jnp.mean(o * o, axis=-1, keepdims=True)
                    outs.append(o * lax.rsqrt(ms + EPS))
            o_all = jnp.concatenate(outs, axis=-1) * gnw_ref[...]
            og_ref[0, rows, :] = (o_all * _silu(r_ref[0, rows, :].astype(F32))).astype(BF16)
            s_b = col_bcast(jnp.exp(gb_ref[c][0:1])) * s_b + ub_ref[c]
        return s_b

    sfb_ref[0] = lax.fori_loop(0, n_blk, pass2, s0b_ref[0])


def _gla(qkv, afab, r, s0f, s0b, wab, bab, gnw):
    B, L, _ = qkv.shape
    n = L // CHUNK
    tok = lambda w: pl.BlockSpec((1, L, w), lambda b: (b, 0, 0))
    st = pl.BlockSpec((1, GLA_QK, GLA_DV), lambda b: (b, 0, 0))
    full = lambda a: pl.BlockSpec(a.shape, lambda b: (0,) * a.ndim)
    small = [wab, bab, gnw.reshape(1, GLA_V)]
    return pl.pallas_call(
        _gla_kernel,
        grid=(B,),
        in_specs=[tok(P_R - P_QKV), tok(AFAB_W), tok(GLA_V), st, st] + [full(a) for a in small],
        out_specs=[tok(GLA_V), st, st],
        out_shape=[jax.ShapeDtypeStruct((B, L, GLA_V), BF16),
                   jax.ShapeDtypeStruct((B, GLA_QK, GLA_DV), F32),
                   jax.ShapeDtypeStruct((B, GLA_QK, GLA_DV), F32)],
        scratch_shapes=[pltpu.VMEM((L, GLA_QK), BF16),
                        pltpu.VMEM((L, GLA_QK), BF16), pltpu.VMEM((L, GLA_QK), BF16),
                        pltpu.VMEM((n, GLA_QK, GLA_DV), BF16),
                        pltpu.VMEM((n, GLA_QK, GLA_DV), F32),
                        pltpu.VMEM((n, 8, GLA_QK), F32)],
        compiler_params=_params(("parallel",)),
        name="gla",
    )(qkv, afab, r, s0f, s0b, *small)


def _outproj_kernel(yc_ref, og_ref, ys_ref, x_ref, wo_ref, g2_ref, nw_ref, sh_ref, sc_ref, rwt_ref,
                    xo_ref, h2_ref, lt_ref):
    y = jnp.dot(yc_ref[0], wo_ref[0, 0:D_CONF, :], preferred_element_type=F32)
    y = y + jnp.dot(og_ref[0], wo_ref[0, D_CONF:D_CONF + GLA_V, :], preferred_element_type=F32)
    y = y + jnp.dot(ys_ref[0], wo_ref[0, D_CONF + GLA_V:, :], preferred_element_type=F32)
    x = x_ref[0] + g2_ref[0] * y
    xo_ref[0] = x
    hb = _rms_mod(x, nw_ref[...], sh_ref[0], sc_ref[0]).astype(BF16)
    h2_ref[0] = hb
    lt_ref[0] = lax.dot_general(rwt_ref[...], hb, (((1,), (1,)), ((), ())),
                                preferred_element_type=F32)


def _outproj(yc, og, ys, x, wo, layer, g2, nw, shift, scale, rwt):
    B, L, D = x.shape
    tm = min(L, TOKEN_BLOCK)
    tok = lambda w: pl.BlockSpec((1, tm, w), lambda b, t: (b, t, 0))
    per_b = pl.BlockSpec((1, 1, D), lambda b, t: (b, 0, 0))
    return pl.pallas_call(
        _outproj_kernel,
        grid=(B, L // tm),
        in_specs=[tok(D_CONF), tok(GLA_V), tok(D_SC), tok(D),
                  pl.BlockSpec((1, D, D), lambda b, t: (layer, 0, 0)), per_b,
                  pl.BlockSpec((1, D), lambda b, t: (0, 0)), per_b, per_b,
                  pl.BlockSpec((N_EXPERTS, D), lambda b, t: (0, 0))],
        out_specs=[tok(D), tok(D), pl.BlockSpec((1, N_EXPERTS, tm), lambda b, t: (b, 0, t))],
        out_shape=[jax.ShapeDtypeStruct((B, L, D), F32), jax.ShapeDtypeStruct((B, L, D), BF16),
                   jax.ShapeDtypeStruct((B, N_EXPERTS, L), F32)],
        compiler_params=_params(("parallel", "parallel")),
        name="outproj",
    )(yc, og, ys, x, wo, g2, nw.reshape(1, D), shift, scale, rwt)


def _lane_cumsum(flags):
    rows, n = flags.shape
    i = lax.broadcasted_iota(jnp.int32, (LANES, LANES), 0)
    j = lax.broadcasted_iota(jnp.int32, (LANES, LANES), 1)
    upper = jnp.where(i <= j, 1.0, 0.0).astype(BF16)
    xb = flags.astype(F32).astype(BF16)
    out, offset = [], jnp.zeros((rows, 1), F32)
    for blk in range(n // LANES):
        part = jnp.dot(xb[:, blk * LANES:(blk + 1) * LANES], upper, preferred_element_type=F32) + offset
        out.append(part)
        offset = part[:, LANES - 1:LANES]
    return jnp.concatenate(out, axis=1).astype(jnp.int32)


def _route_kernel(lt_ref, pos_ref, gate_ref, bnd_ref, *, cap):
    bs, E, L = lt_ref.shape
    lt = lt_ref[...]
    e = jnp.exp(lt - jnp.max(lt, axis=1, keepdims=True))
    aff = (e / jnp.sum(e, axis=1, keepdims=True)).reshape(bs * E, L)

    def search(i, t):
        cand = t | lax.shift_left(jnp.int32(1), 30 - i)
        cnt = jnp.sum((aff >= pltpu.bitcast(cand, F32)).astype(jnp.int32), axis=1, keepdims=True)
        return jnp.where(cnt >= cap, cand, t)

    thr_bits = lax.fori_loop(0, 31, search, jnp.zeros((bs * E, 1), jnp.int32))
    thr = pltpu.bitcast(thr_bits, F32)
    gt = aff > thr
    eq = aff == thr
    need = cap - jnp.sum(gt.astype(jnp.int32), axis=1, keepdims=True)
    eq_i = eq.astype(jnp.int32)
    eq_rank = _lane_cumsum(eq_i) - eq_i
    sel = gt | (eq & (eq_rank < need))
    sel_i = sel.astype(jnp.int32)
    slot = _lane_cumsum(sel_i) - sel_i
    pos_ref[...] = jnp.where(sel, slot, -1).reshape(bs, E, L)
    gate_ref[...] = aff.reshape(bs, E, L)
    bnd = jnp.concatenate([slot[:, k * TOK_TILE:k * TOK_TILE + 1] for k in range(L // TOK_TILE)], axis=1)
    bnd_ref[...] = bnd.reshape(bs, E, L // TOK_TILE)


def _route(lt, cap):
    B, E, L = lt.shape
    bs = min(B, ROUTE_SAMPLES)
    spec = pl.BlockSpec((bs, E, L), lambda b: (b, 0, 0))
    nt = L // TOK_TILE
    return pl.pallas_call(
        functools.partial(_route_kernel, cap=cap),
        grid=(B // bs,),
        in_specs=[spec], out_specs=[spec, spec, pl.BlockSpec((bs, E, nt), lambda b: (b, 0, 0))],
        out_shape=[jax.ShapeDtypeStruct((B, E, L), jnp.int32), jax.ShapeDtypeStruct((B, E, L), F32),
                   jax.ShapeDtypeStruct((B, E, nt), jnp.int32)],
        compiler_params=_params(("parallel",)),
        name="route",
    )(lt)


def _window(lo, hi, width, cap):
    a0 = jnp.minimum((lo // SLOT_ALIGN) * SLOT_ALIGN, cap - width)
    n = jnp.where(hi > lo, (hi - a0 + width - 1) // width, 0)
    return a0, n


def _one_hot_rows(pos_row, start, width, first_row=None):
    slot = lax.broadcasted_iota(jnp.int32, (width, pos_row.shape[1]), 0) + start
    on = 1.0 if first_row is None else jnp.where(slot >= first_row, 1.0, 0.0)
    return jnp.where(slot == pos_row, on, 0.0).astype(BF16)


def _tile_windows(bnd_ref, pos_ref, b, t, toks, step, n_tiles, cap, width):
    nt = pos_ref.shape[2] // TOK_TILE
    wins = []
    for e in range(N_EXPERTS):
        base = (b * N_EXPERTS + e) * nt
        lo = bnd_ref[base + t * step]
        hi = jnp.where(t + 1 < n_tiles, bnd_ref[base + jnp.minimum(t + 1, n_tiles - 1) * step], cap)
        a0, n = _window(lo, hi, width, cap)
        wins.append((pl.multiple_of(a0, SLOT_ALIGN), n, pos_ref[0, e:e + 1, toks]))
    return wins


def _window_one_hots(wins, w, cap, width):
    ms, starts = [], []
    for a0, n, pos_row in wins:
        if w is None:
            start, first = a0, None
        else:
            first = a0 + w * width
            start = pl.multiple_of(jnp.minimum(first, cap - width), SLOT_ALIGN)
            first = jnp.where(w < n, first, cap)
        ms.append(_one_hot_rows(pos_row, start, width, first))
        starts.append(start)
    return jnp.concatenate(ms, axis=0), starts


def _max_windows(wins):
    return functools.reduce(jnp.maximum, [n for _, n, _ in wins])


def _gather_kernel(bnd_ref, pos_ref, h_ref, xs_ref, *, cap, tile, width):
    b = pl.program_id(0)
    L = h_ref.shape[1]
    step = tile // TOK_TILE
    n_tiles = L // tile
    unroll = min(TILES_PER_ITER, n_tiles)
    xs_ref[...] = jnp.zeros_like(xs_ref)

    def scatter_windows(m_all, starts, toks):
        part = jnp.dot(m_all, h_ref[0, toks, :], preferred_element_type=F32)
        for e, start in enumerate(starts):
            xs_ref[0, e, pl.ds(start, width), :] += part[e * width:(e + 1) * width].astype(BF16)

    def tiles_body(it, carry):
        tiles = []
        for u in range(unroll):
            k = it * unroll + u
            toks = pl.ds(pl.multiple_of(k * tile, tile), tile)
            wins = _tile_windows(bnd_ref, pos_ref, b, k, toks, step, n_tiles, cap, width)
            scatter_windows(*_window_one_hots(wins, None, cap, width), toks)
            tiles.append((toks, wins))
        for toks, wins in tiles:
            def extra(w, c, toks=toks, wins=wins):
                scatter_windows(*_window_one_hots(wins, w, cap, width), toks)
                return c

            lax.fori_loop(1, _max_windows(wins), extra, 0)
        return carry

    lax.fori_loop(0, n_tiles // unroll, tiles_body, 0)


def _gather(bnd, pos, h2, cap):
    B, L, D = h2.shape
    E = N_EXPERTS
    tile = min(L, 256)
    width = min(cap, 64)
    return pl.pallas_call(
        functools.partial(_gather_kernel, cap=cap, tile=tile, width=width),
        grid_spec=pltpu.PrefetchScalarGridSpec(
            num_scalar_prefetch=1, grid=(B,),
            in_specs=[pl.BlockSpec((1, E, L), lambda b, s: (b, 0, 0)),
                      pl.BlockSpec((1, L, D), lambda b, s: (b, 0, 0))],
            out_specs=pl.BlockSpec((1, E, cap, D), lambda b, s: (b, 0, 0, 0))),
        out_shape=jax.ShapeDtypeStruct((B, E, cap, D), BF16),
        compiler_params=_params(("arbitrary",)),
        name="gather",
    )(bnd.reshape(-1), pos, h2)


def _expert_ffn(pos_ref, gate_ref, xs_ref, y_ref, wb_ref):
    n, _, cap, D = xs_ref.shape
    L = pos_ref.shape[2]
    row = pl.ds(pl.program_id(0) % SUBLANES, 1)
    slot = lax.broadcasted_iota(jnp.int32, (cap, L), 0)
    gates = [jnp.sum(jnp.where(slot == pos_ref[i, row, :], gate_ref[i, row, :], 0.0), axis=1, keepdims=True)
             for i in range(n)]
    gate = jnp.concatenate(gates, axis=0)
    xs = xs_ref[:, 0].reshape(n * cap, D)
    hid = _silu(jnp.dot(xs, wb_ref[0], preferred_element_type=F32))
    hid = (hid * jnp.dot(xs, wb_ref[1], preferred_element_type=F32)).astype(BF16)
    y = (jnp.dot(hid, wb_ref[2], preferred_element_type=F32) * gate).astype(BF16)
    y_ref[:, 0] = y.reshape(n, cap, D)


def _ffn_kernel(*refs, with_ctx):
    if with_ctx:
        (pos_ref, gate_ref, xs_ref, cpos_ref, cgate_ref, cxs_ref, w1_ref, w3_ref, w2_ref,
         y_ref, cy_ref, wb_ref) = refs
    else:
        pos_ref, gate_ref, xs_ref, w1_ref, w3_ref, w2_ref, y_ref, wb_ref = refs
    j = pl.program_id(1)

    @pl.when(j == 0)
    def _():
        for i, w_ref in enumerate((w1_ref, w3_ref, w2_ref)):
            wb_ref[i] = w_ref[0, 0].astype(BF16)

    if with_ctx:
        @pl.when(j == 0)
        def _():
            _expert_ffn(cpos_ref, cgate_ref, cxs_ref, cy_ref, wb_ref)

    @pl.when(j >= (1 if with_ctx else 0))
    def _():
        _expert_ffn(pos_ref, gate_ref, xs_ref, y_ref, wb_ref)


def _ffn(lat, ctx, w1, w3, w2, layer, bb):
    pos, gate, xs = lat
    B, E, cap, D = xs.shape
    L = pos.shape[2]
    bb = min(bb, B)
    nb = B // bb
    with_ctx = ctx is not None
    first_lat = 1 if with_ctx else 0
    lat_idx = lambda e, j: (jnp.maximum(j - first_lat, 0), e, 0, 0)
    lat_rows = lambda e, j: (jnp.maximum(j - first_lat, 0), e // SUBLANES, 0)
    ins = [pos, gate, xs]
    in_specs = [pl.BlockSpec((bb, SUBLANES, L), lat_rows), pl.BlockSpec((bb, SUBLANES, L), lat_rows),
                pl.BlockSpec((bb, 1, cap, D), lat_idx)]
    out_shape = [jax.ShapeDtypeStruct((B, E, cap, D), BF16)]
    out_specs = [pl.BlockSpec((bb, 1, cap, D), lat_idx)]
    if with_ctx:
        cpos, cgate, cxs = ctx
        capc, Lc = cxs.shape[2], cpos.shape[2]
        ctx_idx = lambda e, j: (0, e, 0, 0)
        ctx_rows = lambda e, j: (0, e // SUBLANES, 0)
        ins += [cpos, cgate, cxs]
        in_specs += [pl.BlockSpec((B, SUBLANES, Lc), ctx_rows), pl.BlockSpec((B, SUBLANES, Lc), ctx_rows),
                     pl.BlockSpec((B, 1, capc, D), ctx_idx)]
        out_shape.append(jax.ShapeDtypeStruct((B, E, capc, D), BF16))
        out_specs.append(pl.BlockSpec((B, 1, capc, D), ctx_idx))
    wspec = pl.BlockSpec((1, 1, D, D), lambda e, j: (layer, e, 0, 0))
    outs = pl.pallas_call(
        functools.partial(_ffn_kernel, with_ctx=with_ctx),
        grid=(E, nb + (1 if with_ctx else 0)),
        in_specs=in_specs + [wspec, wspec, wspec],
        out_specs=out_specs, out_shape=out_shape,
        scratch_shapes=[pltpu.VMEM((3, D, D), BF16)],
        compiler_params=_params(("arbitrary", "arbitrary")),
        name="ffn",
    )(*ins, w1, w3, w2)
    return outs if with_ctx else (outs[0], None)


def _combine_kernel(bnd_ref, pos_ref, y_ref, x_ref, g_ref, *rest, cap, width, final_norm):
    nw_ref, out_ref = rest if final_norm else (None, rest[0])
    b = pl.program_id(0)
    L = pos_ref.shape[2]
    tile = min(COMB_TILE, out_ref.shape[1])
    step = tile // TOK_TILE
    n_tiles = L // tile
    tiles_per_step = out_ref.shape[1] // tile
    tn = (((0,), (0,)), ((), ()))

    def gathered_sum(m_all, starts):
        y_all = jnp.concatenate([y_ref[0, e, pl.ds(start, width), :] for e, start in enumerate(starts)],
                                axis=0)
        return lax.dot_general(m_all, y_all, tn, preferred_element_type=F32)

    tiles = []
    for i in range(tiles_per_step):
        t = pl.program_id(1) * tiles_per_step + i
        toks = pl.ds(pl.multiple_of(t * tile, tile), tile)
        rows = slice(i * tile, (i + 1) * tile)
        wins = _tile_windows(bnd_ref, pos_ref, b, t, toks, step, n_tiles, cap, width)
        out_ref[0, rows, :] = gathered_sum(*_window_one_hots(wins, None, cap, width))
        tiles.append((rows, wins))
    for rows, wins in tiles:
        def extra(w, c, rows=rows, wins=wins):
            out_ref[0, rows, :] += gathered_sum(*_window_one_hots(wins, w, cap, width))
            return c

        lax.fori_loop(1, _max_windows(wins), extra, 0)
    for rows, _ in tiles:
        x = x_ref[0, rows, :] + g_ref[0] * out_ref[0, rows, :]
        if final_norm:
            x = x * lax.rsqrt(jnp.mean(x * x, axis=-1, keepdims=True) + EPS) * nw_ref[...]
        out_ref[0, rows, :] = x


def _combine(bnd, pos, y, x, g, final_nw):
    B, E, cap, D = y.shape
    L = x.shape[1]
    width = min(cap, 64)
    tm = min(L, TOKEN_BLOCK)
    final_norm = final_nw is not None
    tok = pl.BlockSpec((1, tm, D), lambda b, t, s: (b, t, 0))
    ins = [bnd.reshape(-1), pos, y, x, g]
    in_specs = [pl.BlockSpec((1, E, L), lambda b, t, s: (b, 0, 0)),
                pl.BlockSpec((1, E, cap, D), lambda b, t, s: (b, 0, 0, 0)),
                tok, pl.BlockSpec((1, 1, D), lambda b, t, s: (b, 0, 0))]
    if final_norm:
        ins.append(final_nw.reshape(1, D))
        in_specs.append(pl.BlockSpec((1, D), lambda b, t, s: (0, 0)))
    return pl.pallas_call(
        functools.partial(_combine_kernel, cap=cap, width=width, final_norm=final_norm),
        grid_spec=pltpu.PrefetchScalarGridSpec(
            num_scalar_prefetch=1, grid=(B, L // tm), in_specs=in_specs, out_specs=tok),
        out_shape=jax.ShapeDtypeStruct((B, L, D), F32),
        compiler_params=_params(("arbitrary", "arbitrary")),
        name="combine",
    )(*ins)


def _moe(lat, ctx, w1, w3, w2, layer, bb, final_nw=None):
    routed = []
    for lt, h2, x, g in (lat, ctx) if ctx is not None else (lat,):
        cap = EC_FACTOR * h2.shape[1] // N_EXPERTS
        pos, gate, bnd = _route(lt, cap)
        routed.append((pos, gate, _gather(bnd, pos, h2, cap), bnd, x, g))
    ys = _ffn(routed[0][:3], routed[1][:3] if ctx is not None else None, w1, w3, w2, layer, bb)
    outs = [_combine(bnd, pos, y, x, g, final_nw if i == 0 else None)
            for i, ((pos, _, _, bnd, x, g), y) in enumerate(zip(routed, ys))]
    return outs[0], (outs[1] if ctx is not None else None)


def _permute_w_in(w):
    cols = jnp.concatenate([w[..., 0:OFF_AF], w[..., OFF_R:OFF_SC], w[..., OFF_SC:D_IN], w[..., OFF_AF:OFF_R]],
                           axis=-1)
    return jnp.pad(cols, ((0, 0), (0, 0), (0, P_TOTAL - D_IN))).astype(BF16)


def kernel(x, c, ctx, c_ctx, ada_w, ada_b, norm1_w, norm2_w, w_in, conf_dw_w, conf_dw_b, conf_ln_w,
           conf_ln_b, gla_wa_f, gla_ba_f, gla_wa_b, gla_ba_b, gla_gn_w, sc_w, sc_b, w_out, router_w,
           exp_w1, exp_w3, exp_w2, final_norm_w):
    B, L, D = x.shape
    Lc = ctx.shape[1]

    cond = jnp.concatenate([c, c_ctx[None, :], jnp.zeros((7, D), F32)], axis=0)
    mods = _ada(cond, ada_w, ada_b)

    w_perm = _permute_w_in(w_in)
    wo = w_out.astype(BF16)
    xc = ctx
    zero_state = jnp.zeros((B, GLA_QK, GLA_DV), F32)
    for i in range(DEPTH):
        last = i == DEPTH - 1
        m_lat = [mods[i, :B, j * D:(j + 1) * D].reshape(B, 1, D) for j in range(6)]
        m_ctx = [jnp.broadcast_to(mods[i, B, j * D:(j + 1) * D].reshape(1, 1, D), (B, 1, D))
                 for j in range(6)]
        rwt = router_w[i].T.astype(BF16)
        zero_wa = jnp.zeros((GLA_RANK, GLA_QK), F32)
        wa = jnp.block([[gla_wa_f[i], zero_wa], [zero_wa, gla_wa_b[i]]]).astype(BF16)
        ba = jnp.concatenate([gla_ba_f[i], gla_ba_b[i]]).reshape(1, 2 * GLA_QK)
        experts = (exp_w1, exp_w3, exp_w2, i)
        conv_w = (conf_dw_w[i], conf_dw_b[i], conf_ln_w[i], conf_ln_b[i], sc_w[i], sc_b[i])

        if last:
            c_qkv, c_r, c_afab = _proj(xc, norm1_w[i], m_ctx[0], m_ctx[1], w_perm, i, ("qkv", "r", "afab"))
        else:
            c_conf, c_qkv, c_r, c_sc, c_afab = _proj(xc, norm1_w[i], m_ctx[0], m_ctx[1], w_perm, i)
        c_og, c_sf, c_sb = _gla(c_qkv, c_afab, c_r, zero_state, zero_state, wa, ba, gla_gn_w[i])

        conf, qkv, r, scu, afab = _proj(x, norm1_w[i], m_lat[0], m_lat[1], w_perm, i)
        yc, ys = _convs(conf, scu, *conv_w, row_len=GRID_W)
        og, _, _ = _gla(qkv, afab, r, c_sf, c_sb, wa, ba, gla_gn_w[i])
        x, h2, lt = _outproj(yc, og, ys, x, wo, i, m_lat[2], norm2_w[i], m_lat[3], m_lat[4], rwt)
        moe_ctx = None
        if not last:
            c_yc, c_ys = _convs(c_conf, c_sc, *conv_w, row_len=Lc)
            xc, c_h2, c_lt = _outproj(c_yc, c_og, c_ys, xc, wo, i, m_ctx[2], norm2_w[i], m_ctx[3], m_ctx[4],
                                      rwt)
            moe_ctx = (c_lt, c_h2, xc, m_ctx[5])
        x, xc = _moe((lt, h2, x, m_lat[5]), moe_ctx, *experts, FFN_SAMPLES,
                     final_norm_w if last else None)
    return x
```

```python
import functools

import jax
import jax.numpy as jnp
from jax import lax
from jax.experimental import pallas as pl
from jax.experimental.pallas import tpu as pltpu

F32 = jnp.float32
BF16 = jnp.bfloat16
HI = lax.Precision.HIGHEST

D_MODEL = 1024
DEPTH = 2
GRID_W = 64
D_CONF = 256
D_SC = 256
GLA_HEADS = 4
GLA_DK = 64
GLA_DV = 128
GLA_QK = GLA_HEADS * GLA_DK
GLA_V = GLA_HEADS * GLA_DV
GLA_RANK = 16
GLA_TAU = 16.0
CONF_K = 31
SC_K = 3
N_EXPERTS = 16
EC_FACTOR = 2
EPS = 1e-6

OFF_Q = 2 * D_CONF
OFF_K = OFF_Q + GLA_QK
OFF_V = OFF_K + GLA_QK
OFF_AF = OFF_V + GLA_V
OFF_AB = OFF_AF + GLA_RANK
OFF_R = OFF_AB + GLA_RANK
OFF_SC = OFF_R + GLA_V
D_IN = OFF_SC + 3 * D_SC

P_CONF = 0
P_QKV = 512
P_R = 1536
P_SC = 2048
P_AFAB = 2816
P_TOTAL = 2944
AFAB_W = 128

CHUNK = 64
GLA_BLOCK = 16
TOKEN_BLOCK = 1024
CUMSUM_ROWS = 256
CONV_SUB = 64
CONV_TOKENS = 32
SUBLANES = 8
LANES = 128
CONV_PAD = 16
SC_PAD = 8
TOK_TILE = 128
COMB_TILE = 256
TILES_PER_ITER = 4
ROUTE_SAMPLES = 16
FFN_SAMPLES = 4
SLOT_ALIGN = 16
VMEM_LIMIT = 56 * 1024 * 1024


def _sigmoid(x):
    return 1.0 / (1.0 + jnp.exp(-x))


def _silu(x):
    return x * _sigmoid(x)


def _log_sigmoid(z):
    return jnp.minimum(z, 0.0) - jnp.log(1.0 + jnp.exp(-jnp.abs(z)))


def _params(sem):
    return pltpu.CompilerParams(dimension_semantics=sem, vmem_limit_bytes=VMEM_LIMIT)


def _ada_kernel(s_ref, w_ref, b_ref, o_ref):
    s = _silu(s_ref[...])
    o_ref[0] = jnp.dot(s, w_ref[0], precision=HI, preferred_element_type=F32) + b_ref[0]


def _ada(s_in, ada_w, ada_b):
    rows = s_in.shape[0]
    tn = 1024
    return pl.pallas_call(
        _ada_kernel,
        grid=(DEPTH, 6 * D_MODEL // tn),
        in_specs=[pl.BlockSpec((rows, D_MODEL), lambda l, n: (0, 0)),
                  pl.BlockSpec((1, D_MODEL, tn), lambda l, n: (l, 0, n)),
                  pl.BlockSpec((1, 1, tn), lambda l, n: (l, 0, n))],
        out_specs=pl.BlockSpec((1, rows, tn), lambda l, n: (l, 0, n)),
        out_shape=jax.ShapeDtypeStruct((DEPTH, rows, 6 * D_MODEL), F32),
        compiler_params=_params(("arbitrary", "arbitrary")),
        name="ada",
    )(s_in, ada_w, ada_b.reshape(DEPTH, 1, 6 * D_MODEL))


def _rms_mod(x, nw, shift, scale):
    ms = jnp.mean(x * x, axis=-1, keepdims=True)
    return (x * lax.rsqrt(ms + EPS) * nw) * (1.0 + scale) + shift


PROJ_GROUPS = {"conf": (P_CONF, P_QKV, BF16), "qkv": (P_QKV, P_R, BF16), "r": (P_R, P_SC, BF16),
               "sc": (P_SC, P_AFAB, BF16), "afab": (P_AFAB, P_TOTAL, F32)}


def _proj_kernel(x_ref, nw_ref, sh_ref, sc_ref, w_ref, *out_refs, groups):
    hb = _rms_mod(x_ref[0], nw_ref[...], sh_ref[0], sc_ref[0]).astype(BF16)
    for name, o_ref in zip(groups, out_refs):
        lo, hi, dt = PROJ_GROUPS[name]
        o_ref[0] = jnp.dot(hb, w_ref[0, :, lo:hi], preferred_element_type=F32).astype(dt)


def _proj(x, nw, shift, scale, w_perm, layer, groups=tuple(PROJ_GROUPS)):
    B, L, D = x.shape
    tm = min(L, TOKEN_BLOCK)
    tok = lambda w: pl.BlockSpec((1, tm, w), lambda b, t: (b, t, 0))
    per_b = pl.BlockSpec((1, 1, D), lambda b, t: (b, 0, 0))
    widths = [(PROJ_GROUPS[g][1] - PROJ_GROUPS[g][0], PROJ_GROUPS[g][2]) for g in groups]
    return pl.pallas_call(
        functools.partial(_proj_kernel, groups=groups),
        grid=(B, L // tm),
        in_specs=[tok(D), pl.BlockSpec((1, D), lambda b, t: (0, 0)), per_b, per_b,
                  pl.BlockSpec((1, D, P_TOTAL), lambda b, t: (layer, 0, 0))],
        out_specs=[tok(w) for w, _ in widths],
        out_shape=[jax.ShapeDtypeStruct((B, L, w), dt) for w, dt in widths],
        compiler_params=_params(("parallel", "parallel")),
        name="proj",
    )(x, nw.reshape(1, D), shift, scale, w_perm)


def _conv_kernel(conf_ref, scu_ref, cw_ref, cb_ref, lnw_ref, lnb_ref, sw_ref, sb_ref,
                 yc_ref, ys_ref, yt_ref, ot_ref, pz_ref, *, row_len):
    L = conf_ref.shape[1]
    n_rows = L // row_len
    C = D_CONF
    S = SUBLANES

    split_row = n_rows == 1
    assert split_row or n_rows % S == 0
    t_out = row_len // S if split_row else row_len
    span = t_out + 2 * CONV_PAD

    halves = [slice(h * LANES, (h + 1) * LANES) for h in range(C // LANES)]

    def group(g, carry):
        for h in range(len(halves)):
            if not split_row:
                yt_ref[h, 0:CONV_PAD * S, :] = jnp.zeros((CONV_PAD * S, LANES), F32)
                yt_ref[h, (CONV_PAD + t_out) * S:, :] = jnp.zeros((CONV_PAD * S, LANES), F32)
        for k in range(S):
            if split_row:
                tok0 = k * t_out - CONV_PAD
                lo, hi = max(tok0, 0), min(tok0 + span, L)
                src = conf_ref[0, lo:hi, :]
                for a, b in ((0, lo - tok0), (hi - tok0, span)):
                    for h in range(len(halves)):
                        if b > a:
                            yt_ref[h, pl.ds(k + S * a, b - a, stride=S), :] = jnp.zeros((b - a, LANES), F32)
                first = lo - tok0
            else:
                src = conf_ref[0, pl.ds(pl.multiple_of((g * S + k) * row_len, row_len), row_len), :]
                first = CONV_PAD
            u = src.astype(F32)
            z = u[:, :C] * _sigmoid(u[:, C:])
            for h, ls in enumerate(halves):
                yt_ref[h, pl.ds(k + S * first, z.shape[0], stride=S), :] = z[:, ls]

        def out_rows(t):
            return pl.ds(pl.multiple_of(t * S, S), S)

        for h, ls in enumerate(halves):
            def taps(i, c, h=h, ls=ls):
                t0 = i * CONV_TOKENS
                accs = [None] * CONV_TOKENS
                for m in range(CONV_TOKENS + CONF_K - 1):
                    v = yt_ref[h, out_rows(t0 + CONV_PAD - CONF_K // 2 + m), :]
                    for tt in range(max(0, m - CONF_K + 1), min(CONV_TOKENS, m + 1)):
                        term = cw_ref[m - tt:m - tt + 1, ls] * v
                        accs[tt] = term if accs[tt] is None else accs[tt] + term
                for tt in range(CONV_TOKENS):
                    ot_ref[h, out_rows(t0 + tt), :] = accs[tt] + cb_ref[:, ls]
                return c

            lax.fori_loop(0, t_out // CONV_TOKENS, taps, 0)

        for k in range(S):
            dst = k * t_out if split_row else pl.multiple_of((g * S + k) * row_len, row_len)
            o = jnp.concatenate([ot_ref[h, pl.ds(k, t_out, stride=S), :] for h in range(len(halves))], axis=1)
            mu = jnp.mean(o, axis=-1, keepdims=True)
            cen = o - mu
            var = jnp.mean(cen * cen, axis=-1, keepdims=True)
            y = cen * lax.rsqrt(var + EPS) * lnw_ref[...] + lnb_ref[...]
            yc_ref[0, pl.ds(dst, t_out), :] = _silu(y).astype(BF16)
        return carry

    lax.fori_loop(0, 1 if split_row else n_rows // S, group, 0)

    n_blk = L // CONV_SUB
    pz_ref[0:SC_PAD, :] = jnp.zeros((SC_PAD, D_SC), F32)
    pz_ref[SC_PAD + L:, :] = jnp.zeros((SC_PAD, D_SC), F32)

    def fill_p(i, carry):
        start = pl.multiple_of(i * CONV_SUB, CONV_SUB)
        u = scu_ref[0, pl.ds(start, CONV_SUB), :].astype(F32)
        pz_ref[pl.ds(SC_PAD + start, CONV_SUB), :] = u[:, D_SC:2 * D_SC] * u[:, 2 * D_SC:]
        return carry

    lax.fori_loop(0, n_blk, fill_p, 0)

    def sc_blk(i, carry):
        start = pl.multiple_of(i * CONV_SUB, CONV_SUB)
        win = pz_ref[pl.ds(start, CONV_SUB + 2 * SC_PAD), :]
        acc = sb_ref[...] + sw_ref[0:1, :] * win[SC_PAD - 1:SC_PAD - 1 + CONV_SUB]
        acc = acc + sw_ref[1:2, :] * win[SC_PAD:SC_PAD + CONV_SUB]
        acc = acc + sw_ref[2:3, :] * win[SC_PAD + 1:SC_PAD + 1 + CONV_SUB]
        bg = scu_ref[0, pl.ds(start, CONV_SUB), 0:D_SC].astype(F32)
        ys_ref[0, pl.ds(start, CONV_SUB), :] = (bg * acc).astype(BF16)
        return carry

    lax.fori_loop(0, n_blk, sc_blk, 0)


def _convs(conf, scu, cw, cb, lnw, lnb, sw, sb, row_len):
    B, L, _ = conf.shape
    t_out = row_len // SUBLANES if L == row_len else row_len
    full = lambda a: pl.BlockSpec(a.shape, lambda b: (0,) * a.ndim)
    cw_p = jnp.zeros((32, D_CONF), F32).at[:CONF_K].set(cw)
    sw_p = jnp.zeros((8, D_SC), F32).at[:SC_K].set(sw)
    small = [cw_p, cb.reshape(1, D_CONF), lnw.reshape(1, D_CONF), lnb.reshape(1, D_CONF),
             sw_p, sb.reshape(1, D_SC)]
    return pl.pallas_call(
        functools.partial(_conv_kernel, row_len=row_len),
        grid=(B,),
        in_specs=[pl.BlockSpec((1, L, 2 * D_CONF), lambda b: (b, 0, 0)),
                  pl.BlockSpec((1, L, 3 * D_SC), lambda b: (b, 0, 0))] + [full(a) for a in small],
        out_specs=[pl.BlockSpec((1, L, D_CONF), lambda b: (b, 0, 0)),
                   pl.BlockSpec((1, L, D_SC), lambda b: (b, 0, 0))],
        out_shape=[jax.ShapeDtypeStruct((B, L, D_CONF), BF16),
                   jax.ShapeDtypeStruct((B, L, D_SC), BF16)],
        scratch_shapes=[pltpu.VMEM((D_CONF // LANES, (t_out + 2 * CONV_PAD) * SUBLANES, LANES), F32),
                        pltpu.VMEM((D_CONF // LANES, t_out * SUBLANES, LANES), F32),
                        pltpu.VMEM((L + 2 * SC_PAD, D_SC), F32)],
        compiler_params=_params(("parallel",)),
        name="convs",
    )(conf, scu, *small)


def _gla_kernel(qkv_ref, afab_ref, r_ref, s0f_ref, s0b_ref, wab_ref, bab_ref, gnw_ref,
                og_ref, sff_ref, sfb_ref, p_ref, qif_ref, qib_ref, spf_ref, ub_ref, gb_ref):
    L = qkv_ref.shape[1]
    C = CHUNK
    G = min(GLA_BLOCK, L // C)
    R = G * C
    n_blk = L // R
    mid = C // 2
    scale = GLA_DK ** -0.5
    nt = (((1,), (1,)), ((), ()))
    T = min(R, CUMSUM_ROWS)
    ii = lax.broadcasted_iota(jnp.int32, (T, T), 0)
    jj = lax.broadcasted_iota(jnp.int32, (T, T), 1)
    same = (ii // C) == (jj // C)
    tri_f = jnp.where(same, jnp.where(ii >= jj, 1.0, 0.0), 0.0).astype(BF16)
    tri_b = jnp.where(same, jnp.where(jj >= ii, 1.0, 0.0), 0.0).astype(BF16)
    PK = 2 * GLA_DK
    PV = 2 * GLA_DV
    ci = lax.broadcasted_iota(jnp.int32, (C, 2 * C), 0)
    cj = lax.broadcasted_iota(jnp.int32, (C, 2 * C), 1) % C
    lower = ci >= cj
    upper = cj >= ci
    kr = lax.broadcasted_iota(jnp.int32, (2 * C, PK), 0) // C
    kc = lax.broadcasted_iota(jnp.int32, (2 * C, PK), 1) // GLA_DK
    key_diag = kr == kc
    vr = lax.broadcasted_iota(jnp.int32, (2 * C, PV), 0) // C
    vc = lax.broadcasted_iota(jnp.int32, (2 * C, PV), 1) // GLA_DV
    val_diag = vr == vc

    def col_bcast(row_vec):
        return jnp.broadcast_to(row_vec, (GLA_DV, GLA_QK)).T

    def chunk_cumsum(tri, la):
        hi = la.astype(BF16)
        lo = (la - hi.astype(F32)).astype(BF16)
        parts = jnp.concatenate([hi, lo], axis=1)
        both = jnp.concatenate([jnp.dot(tri, parts[s * T:(s + 1) * T], preferred_element_type=F32)
                                for s in range(R // T)], axis=0)
        return both[:, :GLA_QK] + both[:, GLA_QK:]

    def v_pair(rows, p):
        return qkv_ref[0, rows, 2 * GLA_QK + p * PV:2 * GLA_QK + (p + 1) * PV]

    def pass1(i, s_f):
        r0 = pl.multiple_of(i * R, R)
        ab = afab_ref[0, pl.ds(r0, R), 0:2 * GLA_RANK].astype(BF16)
        z = jnp.dot(ab, wab_ref[...], preferred_element_type=F32) + bab_ref[...]
        la = _log_sigmoid(z) * (1.0 / GLA_TAU)
        b_f = chunk_cumsum(tri_f, la[:, :GLA_QK])
        b_b = chunk_cumsum(tri_b, la[:, GLA_QK:])
        q_all = qkv_ref[0, pl.ds(r0, R), 0:GLA_QK].astype(F32) * scale
        k_all = qkv_ref[0, pl.ds(r0, R), GLA_QK:2 * GLA_QK].astype(F32)
        for g in range(G):
            c = i * G + g
            rows = pl.ds(pl.multiple_of(r0 + g * C, C), C)
            sl = slice(g * C, (g + 1) * C)
            q, k = q_all[sl], k_all[sl]
            scaled = []
            for b, tot_row in ((b_f[sl], C - 1), (b_b[sl], 0)):
                ref_row = b[mid:mid + 1]
                tot = b[tot_row:tot_row + 1]
                q_rel = q * jnp.exp(b - ref_row)
                k_rel = k * jnp.exp(ref_row - b)
                scaled.append((q_rel.astype(BF16), k_rel.astype(BF16),
                               (q_rel * jnp.exp(ref_row)).astype(BF16),
                               (k_rel * jnp.exp(tot - ref_row)).T.astype(BF16),
                               tot))
            (qf, kf, qif, kuf_t, g_f), (qb, kb, qib, kub_t, g_b) = scaled
            qif_ref[rows, :] = qif
            qib_ref[rows, :] = qib
            p_parts, uf, ub = [], [], []
            for p in range(GLA_HEADS // 2):
                ks = slice(p * PK, (p + 1) * PK)
                kbd_f = jnp.where(key_diag, jnp.concatenate([kf[:, ks]] * 2, axis=0), 0)
                kbd_b = jnp.where(key_diag, jnp.concatenate([kb[:, ks]] * 2, axis=0), 0)
                s_fwd = lax.dot_general(qf[:, ks], kbd_f, nt, preferred_element_type=F32)
                s_bwd = lax.dot_general(qb[:, ks], kbd_b, nt, preferred_element_type=F32)
                p_parts.append((jnp.where(lower, s_fwd, 0.0) + jnp.where(upper, s_bwd, 0.0)).astype(BF16))
                u = jnp.dot(jnp.concatenate([kuf_t[ks, :], kub_t[ks, :]], axis=0), v_pair(rows, p),
                            preferred_element_type=F32)
                for blk, dst in ((u[:PK], uf), (u[PK:], ub)):
                    dst += [blk[:GLA_DK, :GLA_DV], blk[GLA_DK:, GLA_DV:]]
            p_ref[rows, :] = jnp.concatenate(p_parts, axis=1)
            ub_ref[c] = jnp.concatenate(ub, axis=0)
            gb_ref[c] = jnp.broadcast_to(g_b, (8, GLA_QK))
            spf_ref[c] = s_f.astype(BF16)
            s_f = col_bcast(jnp.exp(g_f)) * s_f + jnp.concatenate(uf, axis=0)
        return s_f

    sff_ref[0] = lax.fori_loop(0, n_blk, pass1, s0f_ref[0])

    def pass2(t, s_b):
        i = n_blk - 1 - t
        for g in reversed(range(G)):
            c = i * G + g
            rows = pl.ds(pl.multiple_of(i * R + g * C, C), C)
            spf = spf_ref[c]
            snb = s_b.astype(BF16)
            zero_blk = jnp.zeros((GLA_DK, GLA_DV), BF16)

            def pair_state(s, p):
                top = jnp.concatenate([s[2 * p * GLA_DK:(2 * p + 1) * GLA_DK], zero_blk], axis=1)
                bot = jnp.concatenate([zero_blk, s[(2 * p + 1) * GLA_DK:(2 * p + 2) * GLA_DK]], axis=1)
                return jnp.concatenate([top, bot], axis=0)

            outs = []
            for p in range(GLA_HEADS // 2):
                ks = slice(p * PK, (p + 1) * PK)
                vbd = jnp.where(val_diag, jnp.concatenate([v_pair(rows, p)] * 2, axis=0), 0)
                lhs = jnp.concatenate([p_ref[rows, ks], qif_ref[rows, ks], qib_ref[rows, ks]], axis=1)
                rhs = jnp.concatenate([vbd, pair_state(spf, p), pair_state(snb, p)], axis=0)
                o_pair = jnp.dot(lhs, rhs, preferred_element_type=F32)
                for o in (o_pair[:, :GLA_DV], o_pair[:, GLA_DV:]):
                    ms = jnp.mean(o * o, axis=-1, keepdims=True)
                    outs.append(o * lax.rsqrt(ms + EPS))
            o_all = jnp.concatenate(outs, axis=-1) * gnw_ref[...]
            og_ref[0, rows, :] = (o_all * _silu(r_ref[0, rows, :].astype(F32))).astype(BF16)
            s_b = col_bcast(jnp.exp(gb_ref[c][0:1])) * s_b + ub_ref[c]
        return s_b

    sfb_ref[0] = lax.fori_loop(0, n_blk, pass2, s0b_ref[0])


def _gla(qkv, afab, r, s0f, s0b, wab, bab, gnw):
    B, L, _ = qkv.shape
    n = L // CHUNK
    tok = lambda w: pl.BlockSpec((1, L, w), lambda b: (b, 0, 0))
    st = pl.BlockSpec((1, GLA_QK, GLA_DV), lambda b: (b, 0, 0))
    full = lambda a: pl.BlockSpec(a.shape, lambda b: (0,) * a.ndim)
    small = [wab, bab, gnw.reshape(1, GLA_V)]
    return pl.pallas_call(
        _gla_kernel,
        grid=(B,),
        in_specs=[tok(P_R - P_QKV), tok(AFAB_W), tok(GLA_V), st, st] + [full(a) for a in small],
        out_specs=[tok(GLA_V), st, st],
        out_shape=[jax.ShapeDtypeStruct((B, L, GLA_V), BF16),
                   jax.ShapeDtypeStruct((B, GLA_QK, GLA_DV), F32),
                   jax.ShapeDtypeStruct((B, GLA_QK, GLA_DV), F32)],
        scratch_shapes=[pltpu.VMEM((L, GLA_QK), BF16),
                        pltpu.VMEM((L, GLA_QK), BF16), pltpu.VMEM((L, GLA_QK), BF16),
                        pltpu.VMEM((n, GLA_QK, GLA_DV), BF16),
                        pltpu.VMEM((n, GLA_QK, GLA_DV), F32),
                        pltpu.VMEM((n, 8, GLA_QK), F32)],
        compiler_params=_params(("parallel",)),
        name="gla",
    )(qkv, afab, r, s0f, s0b, *small)


def _gla_states_kernel(qkv_ref, afab_ref, s0f_ref, s0b_ref, wab_ref, bab_ref, sff_ref, sfb_ref,
                       ub_ref, gb_ref):
    L = qkv_ref.shape[1]
    C = CHUNK
    G = min(GLA_BLOCK, L // C)
    R = G * C
    n_blk = L // R
    T = min(R, CUMSUM_ROWS)
    PK, PV = 2 * GLA_DK, 2 * GLA_DV
    ii = lax.broadcasted_iota(jnp.int32, (T, T), 0)
    jj = lax.broadcasted_iota(jnp.int32, (T, T), 1)
    same = (ii // C) == (jj // C)
    tri_f = jnp.where(same, jnp.where(ii >= jj, 1.0, 0.0), 0.0).astype(BF16)
    tri_b = jnp.where(same, jnp.where(jj >= ii, 1.0, 0.0), 0.0).astype(BF16)

    def col_bcast(row_vec):
        return jnp.broadcast_to(row_vec, (GLA_DV, GLA_QK)).T

    def chunk_cumsum(tri, la):
        hi = la.astype(BF16)
        lo = (la - hi.astype(F32)).astype(BF16)
        parts = jnp.concatenate([hi, lo], axis=1)
        both = jnp.concatenate([jnp.dot(tri, parts[s * T:(s + 1) * T], preferred_element_type=F32)
                                for s in range(R // T)], axis=0)
        return both[:, :GLA_QK] + both[:, GLA_QK:]

    def pass1(i, s_f):
        r0 = pl.multiple_of(i * R, R)
        ab = afab_ref[0, pl.ds(r0, R), 0:2 * GLA_RANK].astype(BF16)
        z = jnp.dot(ab, wab_ref[...], preferred_element_type=F32) + bab_ref[...]
        la = _log_sigmoid(z) * (1.0 / GLA_TAU)
        b_f = chunk_cumsum(tri_f, la[:, :GLA_QK])
        b_b = chunk_cumsum(tri_b, la[:, GLA_QK:])
        k_all = qkv_ref[0, pl.ds(r0, R), GLA_QK:2 * GLA_QK].astype(F32)
        for g in range(G):
            c = i * G + g
            rows = pl.ds(pl.multiple_of(r0 + g * C, C), C)
            sl = slice(g * C, (g + 1) * C)
            tot_f, tot_b = b_f[sl][C - 1:C], b_b[sl][0:1]
            kt_f = (k_all[sl] * jnp.exp(tot_f - b_f[sl])).T.astype(BF16)
            kt_b = (k_all[sl] * jnp.exp(tot_b - b_b[sl])).T.astype(BF16)
            uf, ub = [], []
            for p in range(GLA_HEADS // 2):
                ks = slice(p * PK, (p + 1) * PK)
                v_p = qkv_ref[0, rows, 2 * GLA_QK + p * PV:2 * GLA_QK + (p + 1) * PV]
                u = jnp.dot(jnp.concatenate([kt_f[ks, :], kt_b[ks, :]], axis=0), v_p,
                            preferred_element_type=F32)
                for blk, dst in ((u[:PK], uf), (u[PK:], ub)):
                    dst += [blk[:GLA_DK, :GLA_DV], blk[GLA_DK:, GLA_DV:]]
            ub_ref[c] = jnp.concatenate(ub, axis=0)
            gb_ref[c] = jnp.broadcast_to(tot_b, (8, GLA_QK))
            s_f = col_bcast(jnp.exp(tot_f)) * s_f + jnp.concatenate(uf, axis=0)
        return s_f

    sff_ref[0] = lax.fori_loop(0, n_blk, pass1, s0f_ref[0])

    def pass2(t, s_b):
        c = L // C - 1 - t
        return col_bcast(jnp.exp(gb_ref[c][0:1])) * s_b + ub_ref[c]

    sfb_ref[0] = lax.fori_loop(0, L // C, pass2, s0b_ref[0])


def _gla_states(qkv, afab, s0f, s0b, wab, bab):
    B, L, _ = qkv.shape
    n = L // CHUNK
    tok = lambda w: pl.BlockSpec((1, L, w), lambda b: (b, 0, 0))
    st = pl.BlockSpec((1, GLA_QK, GLA_DV), lambda b: (b, 0, 0))
    full = lambda a: pl.BlockSpec(a.shape, lambda b: (0,) * a.ndim)
    return pl.pallas_call(
        _gla_states_kernel,
        grid=(B,),
        in_specs=[tok(P_R - P_QKV), tok(AFAB_W), st, st, full(wab), full(bab)],
        out_specs=[st, st],
        out_shape=[jax.ShapeDtypeStruct((B, GLA_QK, GLA_DV), F32)] * 2,
        scratch_shapes=[pltpu.VMEM((n, GLA_QK, GLA_DV), F32),
                        pltpu.VMEM((n, 8, GLA_QK), F32)],
        compiler_params=_params(("parallel",)),
        name="gla_states",
    )(qkv, afab, s0f, s0b, wab, bab)


def _outproj_kernel(yc_ref, og_ref, ys_ref, x_ref, wo_ref, g2_ref, nw_ref, sh_ref, sc_ref, rwt_ref,
                    xo_ref, h2_ref, lt_ref):
    y = jnp.dot(yc_ref[0], wo_ref[0, 0:D_CONF, :], preferred_element_type=F32)
    y = y + jnp.dot(og_ref[0], wo_ref[0, D_CONF:D_CONF + GLA_V, :], preferred_element_type=F32)
    y = y + jnp.dot(ys_ref[0], wo_ref[0, D_CONF + GLA_V:, :], preferred_element_type=F32)
    x = x_ref[0] + g2_ref[0] * y
    xo_ref[0] = x
    hb = _rms_mod(x, nw_ref[...], sh_ref[0], sc_ref[0]).astype(BF16)
    h2_ref[0] = hb
    lt_ref[0] = lax.dot_general(rwt_ref[...], hb, (((1,), (1,)), ((), ())),
                                preferred_element_type=F32)


def _outproj(yc, og, ys, x, wo, layer, g2, nw, shift, scale, rwt):
    B, L, D = x.shape
    tm = min(L, TOKEN_BLOCK)
    tok = lambda w: pl.BlockSpec((1, tm, w), lambda b, t: (b, t, 0))
    per_b = pl.BlockSpec((1, 1, D), lambda b, t: (b, 0, 0))
    return pl.pallas_call(
        _outproj_kernel,
        grid=(B, L // tm),
        in_specs=[tok(D_CONF), tok(GLA_V), tok(D_SC), tok(D),
                  pl.BlockSpec((1, D, D), lambda b, t: (layer, 0, 0)), per_b,
                  pl.BlockSpec((1, D), lambda b, t: (0, 0)), per_b, per_b,
                  pl.BlockSpec((N_EXPERTS, D), lambda b, t: (0, 0))],
        out_specs=[tok(D), tok(D), pl.BlockSpec((1, N_EXPERTS, tm), lambda b, t: (b, 0, t))],
        out_shape=[jax.ShapeDtypeStruct((B, L, D), F32), jax.ShapeDtypeStruct((B, L, D), BF16),
                   jax.ShapeDtypeStruct((B, N_EXPERTS, L), F32)],
        compiler_params=_params(("parallel", "parallel")),
        name="outproj",
    )(yc, og, ys, x, wo, g2, nw.reshape(1, D), shift, scale, rwt)


def _lane_cumsum(flags):
    rows, n = flags.shape
    i = lax.broadcasted_iota(jnp.int32, (LANES, LANES), 0)
    j = lax.broadcasted_iota(jnp.int32, (LANES, LANES), 1)
    upper = jnp.where(i <= j, 1.0, 0.0).astype(BF16)
    xb = flags.astype(F32).astype(BF16)
    out, offset = [], jnp.zeros((rows, 1), F32)
    for blk in range(n // LANES):
        part = jnp.dot(xb[:, blk * LANES:(blk + 1) * LANES], upper, preferred_element_type=F32) + offset
        out.append(part)
        offset = part[:, LANES - 1:LANES]
    return jnp.concatenate(out, axis=1).astype(jnp.int32)


def _route_kernel(lt_ref, pos_ref, gate_ref, bnd_ref, *, cap):
    bs, E, L = lt_ref.shape
    lt = lt_ref[...]
    e = jnp.exp(lt - jnp.max(lt, axis=1, keepdims=True))
    aff = (e / jnp.sum(e, axis=1, keepdims=True)).reshape(bs * E, L)

    def search(i, t):
        cand = t | lax.shift_left(jnp.int32(1), 30 - i)
        cnt = jnp.sum((aff >= pltpu.bitcast(cand, F32)).astype(jnp.int32), axis=1, keepdims=True)
        return jnp.where(cnt >= cap, cand, t)

    thr_bits = lax.fori_loop(0, 31, search, jnp.zeros((bs * E, 1), jnp.int32))
    thr = pltpu.bitcast(thr_bits, F32)
    gt = aff > thr
    eq = aff == thr
    need = cap - jnp.sum(gt.astype(jnp.int32), axis=1, keepdims=True)
    eq_i = eq.astype(jnp.int32)
    eq_rank = _lane_cumsum(eq_i) - eq_i
    sel = gt | (eq & (eq_rank < need))
    sel_i = sel.astype(jnp.int32)
    slot = _lane_cumsum(sel_i) - sel_i
    pos_ref[...] = jnp.where(sel, slot, -1).reshape(bs, E, L)
    gate_ref[...] = aff.reshape(bs, E, L)
    bnd = jnp.concatenate([slot[:, k * TOK_TILE:k * TOK_TILE + 1] for k in range(L // TOK_TILE)], axis=1)
    bnd_ref[...] = bnd.reshape(bs, E, L // TOK_TILE)


def _route(lt, cap):
    B, E, L = lt.shape
    bs = min(B, ROUTE_SAMPLES)
    spec = pl.BlockSpec((bs, E, L), lambda b: (b, 0, 0))
    nt = L // TOK_TILE
    return pl.pallas_call(
        functools.partial(_route_kernel, cap=cap),
        grid=(B // bs,),
        in_specs=[spec], out_specs=[spec, spec, pl.BlockSpec((bs, E, nt), lambda b: (b, 0, 0))],
        out_shape=[jax.ShapeDtypeStruct((B, E, L), jnp.int32), jax.ShapeDtypeStruct((B, E, L), F32),
                   jax.ShapeDtypeStruct((B, E, nt), jnp.int32)],
        compiler_params=_params(("parallel",)),
        name="route",
    )(lt)


def _window(lo, hi, width, cap):
    a0 = jnp.minimum((lo // SLOT_ALIGN) * SLOT_ALIGN, cap - width)
    n = jnp.where(hi > lo, (hi - a0 + width - 1) // width, 0)
    return a0, n


def _one_hot_rows(pos_row, start, width, first_row=None):
    slot = lax.broadcasted_iota(jnp.int32, (width, pos_row.shape[1]), 0) + start
    on = 1.0 if first_row is None else jnp.where(slot >= first_row, 1.0, 0.0)
    return jnp.where(slot == pos_row, on, 0.0).astype(BF16)


def _tile_windows(bnd_ref, pos_ref, b, t, toks, step, n_tiles, cap, width):
    nt = pos_ref.shape[2] // TOK_TILE
    wins = []
    for e in range(N_EXPERTS):
        base = (b * N_EXPERTS + e) * nt
        lo = bnd_ref[base + t * step]
        hi = jnp.where(t + 1 < n_tiles, bnd_ref[base + jnp.minimum(t + 1, n_tiles - 1) * step], cap)
        a0, n = _window(lo, hi, width, cap)
        wins.append((pl.multiple_of(a0, SLOT_ALIGN), n, pos_ref[0, e:e + 1, toks]))
    return wins


def _window_one_hots(wins, w, cap, width):
    ms, starts = [], []
    for a0, n, pos_row in wins:
        if w is None:
            start, first = a0, None
        else:
            first = a0 + w * width
            start = pl.multiple_of(jnp.minimum(first, cap - width), SLOT_ALIGN)
            first = jnp.where(w < n, first, cap)
        ms.append(_one_hot_rows(pos_row, start, width, first))
        starts.append(start)
    return jnp.concatenate(ms, axis=0), starts


def _max_windows(wins):
    return functools.reduce(jnp.maximum, [n for _, n, _ in wins])


def _gather_kernel(bnd_ref, pos_ref, h_ref, xs_ref, *, cap, tile, width):
    b = pl.program_id(0)
    L = h_ref.shape[1]
    step = tile // TOK_TILE
    n_tiles = L // tile
    unroll = min(TILES_PER_ITER, n_tiles)
    xs_ref[...] = jnp.zeros_like(xs_ref)

    def scatter_windows(m_all, starts, toks):
        part = jnp.dot(m_all, h_ref[0, toks, :], preferred_element_type=F32)
        for e, start in enumerate(starts):
            xs_ref[0, e, pl.ds(start, width), :] += part[e * width:(e + 1) * width].astype(BF16)

    def tiles_body(it, carry):
        tiles = []
        for u in range(unroll):
            k = it * unroll + u
            toks = pl.ds(pl.multiple_of(k * tile, tile), tile)
            wins = _tile_windows(bnd_ref, pos_ref, b, k, toks, step, n_tiles, cap, width)
            scatter_windows(*_window_one_hots(wins, None, cap, width), toks)
            tiles.append((toks, wins))
        for toks, wins in tiles:
            def extra(w, c, toks=toks, wins=wins):
                scatter_windows(*_window_one_hots(wins, w, cap, width), toks)
                return c

            lax.fori_loop(1, _max_windows(wins), extra, 0)
        return carry

    lax.fori_loop(0, n_tiles // unroll, tiles_body, 0)


def _gather(bnd, pos, h2, cap):
    B, L, D = h2.shape
    E = N_EXPERTS
    tile = min(L, 256)
    width = min(cap, 64)
    return pl.pallas_call(
        functools.partial(_gather_kernel, cap=cap, tile=tile, width=width),
        grid_spec=pltpu.PrefetchScalarGridSpec(
            num_scalar_prefetch=1, grid=(B,),
            in_specs=[pl.BlockSpec((1, E, L), lambda b, s: (b, 0, 0)),
                      pl.BlockSpec((1, L, D), lambda b, s: (b, 0, 0))],
            out_specs=pl.BlockSpec((1, E, cap, D), lambda b, s: (b, 0, 0, 0))),
        out_shape=jax.ShapeDtypeStruct((B, E, cap, D), BF16),
        compiler_params=_params(("arbitrary",)),
        name="gather",
    )(bnd.reshape(-1), pos, h2)


def _expert_ffn(pos_ref, gate_ref, xs_ref, y_ref, wb_ref):
    n, _, cap, D = xs_ref.shape
    L = pos_ref.shape[2]
    row = pl.ds(pl.program_id(0) % SUBLANES, 1)
    slot = lax.broadcasted_iota(jnp.int32, (cap, L), 0)
    gates = [jnp.sum(jnp.where(slot == pos_ref[i, row, :], gate_ref[i, row, :], 0.0), axis=1, keepdims=True)
             for i in range(n)]
    gate = jnp.concatenate(gates, axis=0)
    xs = xs_ref[:, 0].reshape(n * cap, D)
    hid = _silu(jnp.dot(xs, wb_ref[0], preferred_element_type=F32))
    hid = (hid * jnp.dot(xs, wb_ref[1], preferred_element_type=F32)).astype(BF16)
    y = (jnp.dot(hid, wb_ref[2], preferred_element_type=F32) * gate).astype(BF16)
    y_ref[:, 0] = y.reshape(n, cap, D)


def _ffn_kernel(*refs, with_ctx):
    if with_ctx:
        (pos_ref, gate_ref, xs_ref, cpos_ref, cgate_ref, cxs_ref, w1_ref, w3_ref, w2_ref,
         y_ref, cy_ref, wb_ref) = refs
    else:
        pos_ref, gate_ref, xs_ref, w1_ref, w3_ref, w2_ref, y_ref, wb_ref = refs
    j = pl.program_id(1)

    @pl.when(j == 0)
    def _():
        for i, w_ref in enumerate((w1_ref, w3_ref, w2_ref)):
            wb_ref[i] = w_ref[0, 0].astype(BF16)

    if with_ctx:
        @pl.when(j == 0)
        def _():
            _expert_ffn(cpos_ref, cgate_ref, cxs_ref, cy_ref, wb_ref)

    @pl.when(j >= (1 if with_ctx else 0))
    def _():
        _expert_ffn(pos_ref, gate_ref, xs_ref, y_ref, wb_ref)


def _ffn(lat, ctx, w1, w3, w2, layer, bb):
    pos, gate, xs = lat
    B, E, cap, D = xs.shape
    L = pos.shape[2]
    bb = min(bb, B)
    nb = B // bb
    with_ctx = ctx is not None
    first_lat = 1 if with_ctx else 0
    lat_idx = lambda e, j: (jnp.maximum(j - first_lat, 0), e, 0, 0)
    lat_rows = lambda e, j: (jnp.maximum(j - first_lat, 0), e // SUBLANES, 0)
    ins = [pos, gate, xs]
    in_specs = [pl.BlockSpec((bb, SUBLANES, L), lat_rows), pl.BlockSpec((bb, SUBLANES, L), lat_rows),
                pl.BlockSpec((bb, 1, cap, D), lat_idx)]
    out_shape = [jax.ShapeDtypeStruct((B, E, cap, D), BF16)]
    out_specs = [pl.BlockSpec((bb, 1, cap, D), lat_idx)]
    if with_ctx:
        cpos, cgate, cxs = ctx
        capc, Lc = cxs.shape[2], cpos.shape[2]
        ctx_idx = lambda e, j: (0, e, 0, 0)
        ctx_rows = lambda e, j: (0, e // SUBLANES, 0)
        ins += [cpos, cgate, cxs]
        in_specs += [pl.BlockSpec((B, SUBLANES, Lc), ctx_rows), pl.BlockSpec((B, SUBLANES, Lc), ctx_rows),
                     pl.BlockSpec((B, 1, capc, D), ctx_idx)]
        out_shape.append(jax.ShapeDtypeStruct((B, E, capc, D), BF16))
        out_specs.append(pl.BlockSpec((B, 1, capc, D), ctx_idx))
    wspec = pl.BlockSpec((1, 1, D, D), lambda e, j: (layer, e, 0, 0))
    outs = pl.pallas_call(
        functools.partial(_ffn_kernel, with_ctx=with_ctx),
        grid=(E, nb + (1 if with_ctx else 0)),
        in_specs=in_specs + [wspec, wspec, wspec],
        out_specs=out_specs, out_shape=out_shape,
        scratch_shapes=[pltpu.VMEM((3, D, D), BF16)],
        compiler_params=_params(("arbitrary", "arbitrary")),
        name="ffn",
    )(*ins, w1, w3, w2)
    return outs if with_ctx else (outs[0], None)


def _combine_kernel(bnd_ref, pos_ref, y_ref, x_ref, g_ref, *rest, cap, width, final_norm):
    nw_ref, out_ref = rest if final_norm else (None, rest[0])
    b = pl.program_id(0)
    L = pos_ref.shape[2]
    tile = min(COMB_TILE, out_ref.shape[1])
    step = tile // TOK_TILE
    n_tiles = L // tile
    tiles_per_step = out_ref.shape[1] // tile
    tn = (((0,), (0,)), ((), ()))

    def gathered_sum(m_all, starts):
        y_all = jnp.concatenate([y_ref[0, e, pl.ds(start, width), :] for e, start in enumerate(starts)],
                                axis=0)
        return lax.dot_general(m_all, y_all, tn, preferred_element_type=F32)

    tiles = []
    for i in range(tiles_per_step):
        t = pl.program_id(1) * tiles_per_step + i
        toks = pl.ds(pl.multiple_of(t * tile, tile), tile)
        rows = slice(i * tile, (i + 1) * tile)
        wins = _tile_windows(bnd_ref, pos_ref, b, t, toks, step, n_tiles, cap, width)
        out_ref[0, rows, :] = gathered_sum(*_window_one_hots(wins, None, cap, width))
        tiles.append((rows, wins))
    for rows, wins in tiles:
        def extra(w, c, rows=rows, wins=wins):
            out_ref[0, rows, :] += gathered_sum(*_window_one_hots(wins, w, cap, width))
            return c

        lax.fori_loop(1, _max_windows(wins), extra, 0)
    for rows, _ in tiles:
        x = x_ref[0, rows, :] + g_ref[0] * out_ref[0, rows, :]
        if final_norm:
            x = x * lax.rsqrt(jnp.mean(x * x, axis=-1, keepdims=True) + EPS) * nw_ref[...]
        out_ref[0, rows, :] = x


def _combine(bnd, pos, y, x, g, final_nw):
    B, E, cap, D = y.shape
    L = x.shape[1]
    width = min(cap, 64)
    tm = min(L, TOKEN_BLOCK)
    final_norm = final_nw is not None
    tok = pl.BlockSpec((1, tm, D), lambda b, t, s: (b, t, 0))
    ins = [bnd.reshape(-1), pos, y, x, g]
    in_specs = [pl.BlockSpec((1, E, L), lambda b, t, s: (b, 0, 0)),
                pl.BlockSpec((1, E, cap, D), lambda b, t, s: (b, 0, 0, 0)),
                tok, pl.BlockSpec((1, 1, D), lambda b, t, s: (b, 0, 0))]
    if final_norm:
        ins.append(final_nw.reshape(1, D))
        in_specs.append(pl.BlockSpec((1, D), lambda b, t, s: (0, 0)))
    return pl.pallas_call(
        functools.partial(_combine_kernel, cap=cap, width=width, final_norm=final_norm),
        grid_spec=pltpu.PrefetchScalarGridSpec(
            num_scalar_prefetch=1, grid=(B, L // tm), in_specs=in_specs, out_specs=tok),
        out_shape=jax.ShapeDtypeStruct((B, L, D), F32),
        compiler_params=_params(("arbitrary", "arbitrary")),
        name="combine",
    )(*ins)


def _moe(lat, ctx, w1, w3, w2, layer, bb, final_nw=None):
    routed = []
    for lt, h2, x, g in (lat, ctx) if ctx is not None else (lat,):
        cap = EC_FACTOR * h2.shape[1] // N_EXPERTS
        pos, gate, bnd = _route(lt, cap)
        routed.append((pos, gate, _gather(bnd, pos, h2, cap), bnd, x, g))
    ys = _ffn(routed[0][:3], routed[1][:3] if ctx is not None else None, w1, w3, w2, layer, bb)
    outs = [_combine(bnd, pos, y, x, g, final_nw if i == 0 else None)
            for i, ((pos, _, _, bnd, x, g), y) in enumerate(zip(routed, ys))]
    return outs[0], (outs[1] if ctx is not None else None)


def _permute_w_in(w):
    cols = jnp.concatenate([w[..., 0:OFF_AF], w[..., OFF_R:OFF_SC], w[..., OFF_SC:D_IN], w[..., OFF_AF:OFF_R]],
                           axis=-1)
    return jnp.pad(cols, ((0, 0), (0, 0), (0, P_TOTAL - D_IN))).astype(BF16)


def kernel(x, c, ctx, c_ctx, ada_w, ada_b, norm1_w, norm2_w, w_in, conf_dw_w, conf_dw_b, conf_ln_w,
           conf_ln_b, gla_wa_f, gla_ba_f, gla_wa_b, gla_ba_b, gla_gn_w, sc_w, sc_b, w_out, router_w,
           exp_w1, exp_w3, exp_w2, final_norm_w):
    B, L, D = x.shape
    Lc = ctx.shape[1]

    cond = jnp.concatenate([c, c_ctx[None, :], jnp.zeros((7, D), F32)], axis=0)
    mods = _ada(cond, ada_w, ada_b)

    w_perm = _permute_w_in(w_in)
    wo = w_out.astype(BF16)
    xc = ctx
    zero_state = jnp.zeros((B, GLA_QK, GLA_DV), F32)
    for i in range(DEPTH):
        last = i == DEPTH - 1
        m_lat = [mods[i, :B, j * D:(j + 1) * D].reshape(B, 1, D) for j in range(6)]
        m_ctx = [jnp.broadcast_to(mods[i, B, j * D:(j + 1) * D].reshape(1, 1, D), (B, 1, D))
                 for j in range(6)]
        rwt = router_w[i].T.astype(BF16)
        zero_wa = jnp.zeros((GLA_RANK, GLA_QK), F32)
        wa = jnp.block([[gla_wa_f[i], zero_wa], [zero_wa, gla_wa_b[i]]]).astype(BF16)
        ba = jnp.concatenate([gla_ba_f[i], gla_ba_b[i]]).reshape(1, 2 * GLA_QK)
        experts = (exp_w1, exp_w3, exp_w2, i)
        conv_w = (conf_dw_w[i], conf_dw_b[i], conf_ln_w[i], conf_ln_b[i], sc_w[i], sc_b[i])

        if last:
            c_qkv, c_afab = _proj(xc, norm1_w[i], m_ctx[0], m_ctx[1], w_perm, i, ("qkv", "afab"))
            c_sf, c_sb = _gla_states(c_qkv, c_afab, zero_state, zero_state, wa, ba)
        else:
            c_conf, c_qkv, c_r, c_sc, c_afab = _proj(xc, norm1_w[i], m_ctx[0], m_ctx[1], w_perm, i)
            c_og, c_sf, c_sb = _gla(c_qkv, c_afab, c_r, zero_state, zero_state, wa, ba, gla_gn_w[i])

        conf, qkv, r, scu, afab = _proj(x, norm1_w[i], m_lat[0], m_lat[1], w_perm, i)
        yc, ys = _convs(conf, scu, *conv_w, row_len=GRID_W)
        og, _, _ = _gla(qkv, afab, r, c_sf, c_sb, wa, ba, gla_gn_w[i])
        x, h2, lt = _outproj(yc, og, ys, x, wo, i, m_lat[2], norm2_w[i], m_lat[3], m_lat[4], rwt)
        moe_ctx = None
        if not last:
            c_yc, c_ys = _convs(c_conf, c_sc, *conv_w, row_len=Lc)
            xc, c_h2, c_lt = _outproj(c_yc, c_og, c_ys, xc, wo, i, m_ctx[2], norm2_w[i], m_ctx[3], m_ctx[4],
                                      rwt)
            moe_ctx = (c_lt, c_h2, xc, m_ctx[5])
        x, xc = _moe((lt, h2, x, m_lat[5]), moe_ctx, *experts, FFN_SAMPLES,
                     final_norm_w if last else None)
    return x
```
